```python
import jax, jax.numpy as jnp
from jax import lax
import numpy as np

D_MODEL = 1024
BATCH = 8
SEQ = 4096
DEPTH = 2

D_MIX = D_MODEL
D_A = D_MIX // 2
D_B = D_MIX - D_A
CHUNK = 128
H_A = 4
HD_A = D_A // H_A
H_B = 4
HD_B = D_B // H_B
CONV_W = 31
D_FF = 4 * D_MODEL
IN_COLS = 2 * D_A + 2 * D_B
EPS = 1e-6

kernel_name = "hybrid_sgu_conformer_conv_block"


def rms_norm(x, g):
    xf = x.astype(jnp.float32)
    y = xf * lax.rsqrt(jnp.mean(xf * xf, axis=-1, keepdims=True) + EPS)
    return (y * g.astype(jnp.float32)).astype(x.dtype)


def layer_norm(x, g, b):
    xf = x.astype(jnp.float32)
    mu = jnp.mean(xf, axis=-1, keepdims=True)
    var = jnp.mean(jnp.square(xf - mu), axis=-1, keepdims=True)
    y = (xf - mu) * lax.rsqrt(var + EPS)
    return (y * g.astype(jnp.float32) + b.astype(jnp.float32)).astype(x.dtype)


def spatial_gating(u_a, v_a, ln_g, ln_b, w_s, b_s):
    bsz, seq, _ = u_a.shape
    u = jax.nn.gelu(u_a, approximate=False)
    v = layer_norm(jax.nn.gelu(v_a, approximate=False), ln_g, ln_b)
    v = v.reshape(bsz, seq // CHUNK, CHUNK, H_A, HD_A)
    mask = jnp.tril(jnp.ones((CHUNK, CHUNK), dtype=w_s.dtype))
    w = w_s * mask[None]
    mixed = jnp.einsum('hts,bcshd->bcthd', w, v)
    mixed = mixed + jnp.transpose(b_s)[None, None, :, :, None]
    return u * mixed.reshape(bsz, seq, D_A)


def conformer_conv(val_b, gate_b, conv_w, conv_b, ln_g, ln_b):
    bsz, seq, _ = val_b.shape
    g = val_b * jax.nn.sigmoid(gate_b)
    c = lax.conv_general_dilated(
        g, conv_w[:, None, :].astype(g.dtype),
        window_strides=(1,), padding=[(CONV_W - 1, 0)],
        dimension_numbers=('NWC', 'WIO', 'NWC'),
        feature_group_count=D_B)
    c = c + conv_b
    c = layer_norm(c.reshape(bsz, seq, H_B, HD_B),
                   ln_g.reshape(H_B, HD_B), ln_b.reshape(H_B, HD_B))
    return jax.nn.silu(c).reshape(bsz, seq, D_B)


def _fwd_setup_inputs(seed: int = 0) -> dict:
    key = jax.random.key(seed)
    ks = jax.random.split(key, 20)
    f32 = jnp.float32
    nrm = lambda k, shape, scale: jax.random.normal(k, shape, f32) * scale
    return {
        "x": jax.random.normal(ks[0], (BATCH, SEQ, D_MODEL), f32),
        "norm1_g": 1.0 + nrm(ks[1], (DEPTH, D_MODEL), 0.05),
        "w_in": nrm(ks[2], (DEPTH, D_MODEL, IN_COLS), D_MODEL ** -0.5),
        "sgu_ln_g": 1.0 + nrm(ks[3], (DEPTH, D_A), 0.05),
        "sgu_ln_b": nrm(ks[4], (DEPTH, D_A), 0.02),
        "sgu_w": nrm(ks[5], (DEPTH, H_A, CHUNK, CHUNK), CHUNK ** -0.5),
        "sgu_b": 1.0 + nrm(ks[6], (DEPTH, H_A, CHUNK), 0.1),
        "conv_w": nrm(ks[7], (DEPTH, CONV_W, D_B), CONV_W ** -0.5),
        "conv_b": nrm(ks[8], (DEPTH, D_B), 0.02),
        "conv_ln_g": 1.0 + nrm(ks[9], (DEPTH, D_B), 0.05),
        "conv_ln_b": nrm(ks[10], (DEPTH, D_B), 0.02),
        "w_out": nrm(ks[11], (DEPTH, D_MIX, D_MODEL), D_MIX ** -0.5),
        "norm2_g": 1.0 + nrm(ks[12], (DEPTH, D_MODEL), 0.05),
        "w_ff1": nrm(ks[13], (DEPTH, D_MODEL, D_FF), D_MODEL ** -0.5),
        "w_ff2": nrm(ks[14], (DEPTH, D_FF, D_MODEL), D_FF ** -0.5),
        "final_g": 1.0 + nrm(ks[15], (D_MODEL,), 0.05),
    }


def _fwd_reference(x, norm1_g, w_in, sgu_ln_g, sgu_ln_b, sgu_w, sgu_b, conv_w, conv_b,
              conv_ln_g, conv_ln_b, w_out, norm2_g, w_ff1, w_ff2, final_g):
    for l in range(DEPTH):
        h = rms_norm(x, norm1_g[l])
        proj = jnp.einsum('bsd,dc->bsc', h, w_in[l])
        u_a = proj[..., :D_A]
        v_a = proj[..., D_A:2 * D_A]
        val_b = proj[..., 2 * D_A:2 * D_A + D_B]
        gate_b = proj[..., 2 * D_A + D_B:]
        a_out = spatial_gating(u_a, v_a, sgu_ln_g[l], sgu_ln_b[l], sgu_w[l], sgu_b[l])
        b_out = conformer_conv(val_b, gate_b, conv_w[l], conv_b[l],
                               conv_ln_g[l], conv_ln_b[l])
        mix = jnp.concatenate([a_out, b_out], axis=-1)
        x = x + jnp.einsum('bsc,cd->bsd', mix, w_out[l])
        h = rms_norm(x, norm2_g[l])
        f = jnp.square(jax.nn.relu(jnp.einsum('bsd,df->bsf', h, w_ff1[l])))
        x = x + jnp.einsum('bsf,fd->bsd', f, w_ff2[l])
    return rms_norm(x, final_g)


import jax as _jax
import jax.numpy as _jnp

TWIN_FORMAT = 'train_step'
FWD_PARAMS = ['x', 'norm1_g', 'w_in', 'sgu_ln_g', 'sgu_ln_b', 'sgu_w', 'sgu_b', 'conv_w', 'conv_b', 'conv_ln_g', 'conv_ln_b', 'w_out', 'norm2_g', 'w_ff1', 'w_ff2', 'final_g']
TWIN_WEIGHTS = ['norm1_g', 'w_in', 'sgu_ln_g', 'sgu_ln_b', 'sgu_w', 'sgu_b', 'conv_w', 'conv_b', 'conv_ln_g', 'conv_ln_b', 'w_out', 'norm2_g', 'w_ff1', 'w_ff2', 'final_g']
TWIN_DIFF_INPUT = 'x'
TWIN_INPUTS = ['x', 'norm1_g', 'w_in', 'sgu_ln_g', 'sgu_ln_b', 'sgu_w', 'sgu_b', 'conv_w', 'conv_b', 'conv_ln_g', 'conv_ln_b', 'w_out', 'norm2_g', 'w_ff1', 'w_ff2', 'final_g', 'loss_target', 'm_norm1_g', 'm_w_in', 'm_sgu_ln_g', 'm_sgu_ln_b', 'm_sgu_w', 'm_sgu_b', 'm_conv_w', 'm_conv_b', 'm_conv_ln_g', 'm_conv_ln_b', 'm_w_out', 'm_norm2_g', 'm_w_ff1', 'm_w_ff2', 'm_final_g', 'v_norm1_g', 'v_w_in', 'v_sgu_ln_g', 'v_sgu_ln_b', 'v_sgu_w', 'v_sgu_b', 'v_conv_w', 'v_conv_b', 'v_conv_ln_g', 'v_conv_ln_b', 'v_w_out', 'v_norm2_g', 'v_w_ff1', 'v_w_ff2', 'v_final_g']
TWIN_OUTPUTS = ['loss', 'grad_x', 'grad_norm1_g', 'grad_w_in', 'grad_sgu_ln_g', 'grad_sgu_ln_b', 'grad_sgu_w', 'grad_sgu_b', 'grad_conv_w', 'grad_conv_b', 'grad_conv_ln_g', 'grad_conv_ln_b', 'grad_w_out', 'grad_norm2_g', 'grad_w_ff1', 'grad_w_ff2', 'grad_final_g', 'delta_norm1_g', 'delta_w_in', 'delta_sgu_ln_g', 'delta_sgu_ln_b', 'delta_sgu_w', 'delta_sgu_b', 'delta_conv_w', 'delta_conv_b', 'delta_conv_ln_g', 'delta_conv_ln_b', 'delta_w_out', 'delta_norm2_g', 'delta_w_ff1', 'delta_w_ff2', 'delta_final_g', 'new_m_norm1_g', 'new_m_w_in', 'new_m_sgu_ln_g', 'new_m_sgu_ln_b', 'new_m_sgu_w', 'new_m_sgu_b', 'new_m_conv_w', 'new_m_conv_b', 'new_m_conv_ln_g', 'new_m_conv_ln_b', 'new_m_w_out', 'new_m_norm2_g', 'new_m_w_ff1', 'new_m_w_ff2', 'new_m_final_g', 'new_v_norm1_g', 'new_v_w_in', 'new_v_sgu_ln_g', 'new_v_sgu_ln_b', 'new_v_sgu_w', 'new_v_sgu_b', 'new_v_conv_w', 'new_v_conv_b', 'new_v_conv_ln_g', 'new_v_conv_ln_b', 'new_v_w_out', 'new_v_norm2_g', 'new_v_w_ff1', 'new_v_w_ff2', 'new_v_final_g']
TWIN_LEAF_KINDS = {'loss': 'loss', 'grad_x': 'grad_x', 'grad_norm1_g': 'grad_w', 'grad_w_in': 'grad_w', 'grad_sgu_ln_g': 'grad_w', 'grad_sgu_ln_b': 'grad_w', 'grad_sgu_w': 'grad_w', 'grad_sgu_b': 'grad_w', 'grad_conv_w': 'grad_w', 'grad_conv_b': 'grad_w', 'grad_conv_ln_g': 'grad_w', 'grad_conv_ln_b': 'grad_w', 'grad_w_out': 'grad_w', 'grad_norm2_g': 'grad_w', 'grad_w_ff1': 'grad_w', 'grad_w_ff2': 'grad_w', 'grad_final_g': 'grad_w', 'delta_norm1_g': 'delta_w', 'delta_w_in': 'delta_w', 'delta_sgu_ln_g': 'delta_w', 'delta_sgu_ln_b': 'delta_w', 'delta_sgu_w': 'delta_w', 'delta_sgu_b': 'delta_w', 'delta_conv_w': 'delta_w', 'delta_conv_b': 'delta_w', 'delta_conv_ln_g': 'delta_w', 'delta_conv_ln_b': 'delta_w', 'delta_w_out': 'delta_w', 'delta_norm2_g': 'delta_w', 'delta_w_ff1': 'delta_w', 'delta_w_ff2': 'delta_w', 'delta_final_g': 'delta_w', 'new_m_norm1_g': 'new_m', 'new_m_w_in': 'new_m', 'new_m_sgu_ln_g': 'new_m', 'new_m_sgu_ln_b': 'new_m', 'new_m_sgu_w': 'new_m', 'new_m_sgu_b': 'new_m', 'new_m_conv_w': 'new_m', 'new_m_conv_b': 'new_m', 'new_m_conv_ln_g': 'new_m', 'new_m_conv_ln_b': 'new_m', 'new_m_w_out': 'new_m', 'new_m_norm2_g': 'new_m', 'new_m_w_ff1': 'new_m', 'new_m_w_ff2': 'new_m', 'new_m_final_g': 'new_m', 'new_v_norm1_g': 'new_v', 'new_v_w_in': 'new_v', 'new_v_sgu_ln_g': 'new_v', 'new_v_sgu_ln_b': 'new_v', 'new_v_sgu_w': 'new_v', 'new_v_sgu_b': 'new_v', 'new_v_conv_w': 'new_v', 'new_v_conv_b': 'new_v', 'new_v_conv_ln_g': 'new_v', 'new_v_conv_ln_b': 'new_v', 'new_v_w_out': 'new_v', 'new_v_norm2_g': 'new_v', 'new_v_w_ff1': 'new_v', 'new_v_w_ff2': 'new_v', 'new_v_final_g': 'new_v'}


def _forward(args):
    return _fwd_reference(*[args[k] for k in FWD_PARAMS])


def _output_shape():
    out = _jax.eval_shape(lambda: _forward(_fwd_setup_inputs(0)))
    return out.shape, out.dtype

N_MICROBATCH = 1
ADAM_LR = 0.001
ADAM_B1 = 0.9
ADAM_B2 = 0.999
ADAM_EPS = 1e-08
ADAM_WD = 0.01
ADAM_STEP = 10
PER_EXAMPLE_BATCH_AXIS = {'x': 0, 'loss_target': 0}
SHARED_INPUTS = []
_WEIGHT_DTYPES = {'norm1_g': _jnp.float32, 'w_in': _jnp.float32, 'sgu_ln_g': _jnp.float32, 'sgu_ln_b': _jnp.float32, 'sgu_w': _jnp.float32, 'sgu_b': _jnp.float32, 'conv_w': _jnp.float32, 'conv_b': _jnp.float32, 'conv_ln_g': _jnp.float32, 'conv_ln_b': _jnp.float32, 'w_out': _jnp.float32, 'norm2_g': _jnp.float32, 'w_ff1': _jnp.float32, 'w_ff2': _jnp.float32, 'final_g': _jnp.float32}
MOMENT_SCALE = {'norm1_g': 1.196176e-01, 'w_in': 8.743275e-02, 'sgu_ln_g': 6.094559e-02, 'sgu_ln_b': 6.262573e-02, 'sgu_w': 6.289570e-02, 'sgu_b': 9.047284e-02, 'conv_w': 1.025810e-01, 'conv_b': 4.175758e-01, 'conv_ln_g': 1.903900e-01, 'conv_ln_b': 2.623752e-01, 'w_out': 1.475625e-01, 'norm2_g': 1.577963e-01, 'w_ff1': 8.049255e-02, 'w_ff2': 2.464179e-01, 'final_g': 3.260122e+01}


def _to_microbatches(a, axis):
    t = _jnp.moveaxis(a, axis, 0)
    t = t.reshape((N_MICROBATCH, t.shape[0] // N_MICROBATCH) + t.shape[1:])
    return _jnp.moveaxis(t, 1, axis + 1)


def setup_inputs(seed: int = 0) -> dict:
    inp = _fwd_setup_inputs(seed)
    key = _jax.random.fold_in(_jax.random.key(seed), 7919)
    shape, _ = _output_shape()
    out = dict(inp)
    out["loss_target"] = _jax.random.normal(_jax.random.fold_in(key, 0), shape, _jnp.float32)
    for i, name in enumerate(TWIN_WEIGHTS):
        w = inp[name].astype(_jnp.float32)
        if MOMENT_SCALE is None:
            s = _jnp.sqrt(_jnp.mean(_jnp.square(w)) + 1e-30)
        else:
            s = MOMENT_SCALE[name]
        km, kv = _jax.random.split(_jax.random.fold_in(key, i + 1))
        out[name] = w
        out["m_" + name] = s * _jax.random.normal(km, w.shape, _jnp.float32)
        out["v_" + name] = (s * s) * _jax.random.uniform(kv, w.shape, _jnp.float32, 0.5, 1.5)
    if N_MICROBATCH > 1:
        for name, axis in PER_EXAMPLE_BATCH_AXIS.items():
            out[name] = _to_microbatches(out[name], axis)
    return {'x': out['x'], 'norm1_g': out['norm1_g'], 'w_in': out['w_in'], 'sgu_ln_g': out['sgu_ln_g'], 'sgu_ln_b': out['sgu_ln_b'], 'sgu_w': out['sgu_w'], 'sgu_b': out['sgu_b'], 'conv_w': out['conv_w'], 'conv_b': out['conv_b'], 'conv_ln_g': out['conv_ln_g'], 'conv_ln_b': out['conv_ln_b'], 'w_out': out['w_out'], 'norm2_g': out['norm2_g'], 'w_ff1': out['w_ff1'], 'w_ff2': out['w_ff2'], 'final_g': out['final_g'], 'loss_target': out['loss_target'], 'm_norm1_g': out['m_norm1_g'], 'm_w_in': out['m_w_in'], 'm_sgu_ln_g': out['m_sgu_ln_g'], 'm_sgu_ln_b': out['m_sgu_ln_b'], 'm_sgu_w': out['m_sgu_w'], 'm_sgu_b': out['m_sgu_b'], 'm_conv_w': out['m_conv_w'], 'm_conv_b': out['m_conv_b'], 'm_conv_ln_g': out['m_conv_ln_g'], 'm_conv_ln_b': out['m_conv_ln_b'], 'm_w_out': out['m_w_out'], 'm_norm2_g': out['m_norm2_g'], 'm_w_ff1': out['m_w_ff1'], 'm_w_ff2': out['m_w_ff2'], 'm_final_g': out['m_final_g'], 'v_norm1_g': out['v_norm1_g'], 'v_w_in': out['v_w_in'], 'v_sgu_ln_g': out['v_sgu_ln_g'], 'v_sgu_ln_b': out['v_sgu_ln_b'], 'v_sgu_w': out['v_sgu_w'], 'v_sgu_b': out['v_sgu_b'], 'v_conv_w': out['v_conv_w'], 'v_conv_b': out['v_conv_b'], 'v_conv_ln_g': out['v_conv_ln_g'], 'v_conv_ln_b': out['v_conv_ln_b'], 'v_w_out': out['v_w_out'], 'v_norm2_g': out['v_norm2_g'], 'v_w_ff1': out['v_w_ff1'], 'v_w_ff2': out['v_w_ff2'], 'v_final_g': out['v_final_g']}


def _loss(weights, diff, rest, loss_target):
    with _jax.named_scope("forward"):
        args = {**rest, TWIN_DIFF_INPUT: diff, **{k: w.astype(_WEIGHT_DTYPES[k]) for k, w in weights.items()}}
        y = _forward(args)
    with _jax.named_scope("loss_head"):
        err = _jnp.square(y.astype(_jnp.float32) - loss_target)
        return 0.5 * _jnp.sum(_jnp.mean(err, axis=-1)) if err.ndim else 0.5 * err


def _adamw(w, g, m, v):
    m = ADAM_B1 * m + (1.0 - ADAM_B1) * g
    v = ADAM_B2 * v + (1.0 - ADAM_B2) * _jnp.square(g)
    m_hat = m / (1.0 - ADAM_B1 ** ADAM_STEP)
    v_hat = v / (1.0 - ADAM_B2 ** ADAM_STEP)
    delta = -ADAM_LR * (m_hat / (_jnp.sqrt(v_hat) + ADAM_EPS) + ADAM_WD * w)
    return delta, m, v


def reference(x, norm1_g, w_in, sgu_ln_g, sgu_ln_b, sgu_w, sgu_b, conv_w, conv_b, conv_ln_g, conv_ln_b, w_out, norm2_g, w_ff1, w_ff2, final_g, loss_target, m_norm1_g, m_w_in, m_sgu_ln_g, m_sgu_ln_b, m_sgu_w, m_sgu_b, m_conv_w, m_conv_b, m_conv_ln_g, m_conv_ln_b, m_w_out, m_norm2_g, m_w_ff1, m_w_ff2, m_final_g, v_norm1_g, v_w_in, v_sgu_ln_g, v_sgu_ln_b, v_sgu_w, v_sgu_b, v_conv_w, v_conv_b, v_conv_ln_g, v_conv_ln_b, v_w_out, v_norm2_g, v_w_ff1, v_w_ff2, v_final_g):
    given = dict(x=x, norm1_g=norm1_g, w_in=w_in, sgu_ln_g=sgu_ln_g, sgu_ln_b=sgu_ln_b, sgu_w=sgu_w, sgu_b=sgu_b, conv_w=conv_w, conv_b=conv_b, conv_ln_g=conv_ln_g, conv_ln_b=conv_ln_b, w_out=w_out, norm2_g=norm2_g, w_ff1=w_ff1, w_ff2=w_ff2, final_g=final_g, loss_target=loss_target, m_norm1_g=m_norm1_g, m_w_in=m_w_in, m_sgu_ln_g=m_sgu_ln_g, m_sgu_ln_b=m_sgu_ln_b, m_sgu_w=m_sgu_w, m_sgu_b=m_sgu_b, m_conv_w=m_conv_w, m_conv_b=m_conv_b, m_conv_ln_g=m_conv_ln_g, m_conv_ln_b=m_conv_ln_b, m_w_out=m_w_out, m_norm2_g=m_norm2_g, m_w_ff1=m_w_ff1, m_w_ff2=m_w_ff2, m_final_g=m_final_g, v_norm1_g=v_norm1_g, v_w_in=v_w_in, v_sgu_ln_g=v_sgu_ln_g, v_sgu_ln_b=v_sgu_ln_b, v_sgu_w=v_sgu_w, v_sgu_b=v_sgu_b, v_conv_w=v_conv_w, v_conv_b=v_conv_b, v_conv_ln_g=v_conv_ln_g, v_conv_ln_b=v_conv_ln_b, v_w_out=v_w_out, v_norm2_g=v_norm2_g, v_w_ff1=v_w_ff1, v_w_ff2=v_w_ff2, v_final_g=v_final_g)
    weights = {n: given[n] for n in TWIN_WEIGHTS}
    shared = {n: given[n] for n in SHARED_INPUTS}
    per_example = {n: given[n] for n in ['x']}
    grad_fn = _jax.value_and_grad(_loss, argnums=(0, 1))

    def one_microbatch(ex, loss_target):
        ex = dict(ex)
        diff = ex.pop(TWIN_DIFF_INPUT)
        return grad_fn(weights, diff, {**shared, **ex}, loss_target)

    if N_MICROBATCH == 1:
        loss, (grad_w, grad_x) = one_microbatch(per_example, given["loss_target"])
    else:
        def body(carry, xs):
            loss_sum, grad_sum = carry
            l_k, (gw_k, gx_k) = one_microbatch(xs[0], xs[1])
            with _jax.named_scope("update"):
                return (loss_sum + l_k, _jax.tree.map(_jnp.add, grad_sum, gw_k)), gx_k

        init = (_jnp.zeros((), _jnp.float32), _jax.tree.map(_jnp.zeros_like, weights))
        (loss, grad_w), grad_x = _jax.lax.scan(body, init, (per_example, given["loss_target"]))
    with _jax.named_scope("update"):
        delta_w, new_m, new_v = {}, {}, {}
        for n in TWIN_WEIGHTS:
            delta_w[n], new_m[n], new_v[n] = _adamw(weights[n], grad_w[n], given["m_" + n], given["v_" + n])
    return (loss, grad_x, *[grad_w[n] for n in TWIN_WEIGHTS], *[delta_w[n] for n in TWIN_WEIGHTS],
            *[new_m[n] for n in TWIN_WEIGHTS], *[new_v[n] for n in TWIN_WEIGHTS])
```

```python
import functools

import jax
import jax.numpy as jnp
from jax import lax
from jax.experimental import pallas as pl
from jax.experimental.pallas import tpu as pltpu

F32 = jnp.float32
BF16 = jnp.bfloat16
MESH = pl.DeviceIdType.MESH

EPS = 1e-6
CHUNK = 128
CONV_TAPS = 31
HALO = 32
N_CHIPS = 4
N_DEV = 8
LANES = 128

ADAM_LR = 0.001
ADAM_B1 = 0.9
ADAM_B2 = 0.999
ADAM_EPS = 1e-08
ADAM_WD = 0.01
ADAM_STEP = 10

TM_MIX = 256
TM_FFN = 512
V7X_VMEM_BYTES = 64 * 2 ** 20
VMEM_LIMIT = V7X_VMEM_BYTES - 8 * 2 ** 20


def _params(sem=None):
    return pltpu.CompilerParams(dimension_semantics=sem, vmem_limit_bytes=VMEM_LIMIT)


def _erf(x):
    ax = jnp.abs(x)
    t = 1.0 / (1.0 + 0.3275911 * ax)
    poly = t * (0.254829592 + t * (-0.284496736 + t * (1.421413741 + t * (-1.453152027 + t * 1.061405429))))
    y = 1.0 - poly * jnp.exp(-ax * ax)
    return jnp.where(x < 0, -y, y)


def _gelu_and_grad(x):
    cdf = 0.5 * (1.0 + _erf(x * 0.7071067811865476))
    pdf = jnp.exp(-0.5 * x * x) * 0.3989422804014327
    return x * cdf, cdf + x * pdf


def _sigmoid(x):
    return 1.0 / (1.0 + jnp.exp(-x))


def _dot(a, b):
    return jnp.dot(a.astype(BF16), b.astype(BF16), preferred_element_type=F32)


def _dot_nt(a, b):
    return lax.dot_general(a.astype(BF16), b.astype(BF16), (((1,), (1,)), ((), ())), preferred_element_type=F32)


def _rms_fwd(x, g):
    r = lax.rsqrt(jnp.mean(x * x, axis=-1, keepdims=True) + EPS)
    xh = x * r
    return r, xh, xh * g


def _rms_bwd(dh, xh, r, g):
    dxh = dh * g
    return r * (dxh - xh * jnp.mean(dxh * xh, axis=-1, keepdims=True))


def _ln_stats(x):
    mu = jnp.mean(x, axis=-1, keepdims=True)
    xc = x - mu
    rs = lax.rsqrt(jnp.mean(xc * xc, axis=-1, keepdims=True) + EPS)
    return xc * rs, rs


def _ln_bwd(dxh, xh, rs):
    return rs * (dxh - jnp.mean(dxh, axis=-1, keepdims=True) - xh * jnp.mean(dxh * xh, axis=-1, keepdims=True))


def _group_ln_fwd(c):
    parts, rss = [], []
    for j in range(c.shape[1] // CHUNK):
        xh, rs = _ln_stats(c[:, j * CHUNK:(j + 1) * CHUNK])
        parts.append(xh)
        rss.append(rs)
    return jnp.concatenate(parts, axis=1), rss


def _group_ln_bwd(dxh, xh, rss):
    parts = []
    for j, rs in enumerate(rss):
        cols = slice(j * CHUNK, (j + 1) * CHUNK)
        parts.append(_ln_bwd(dxh[:, cols], xh[:, cols], rs))
    return jnp.concatenate(parts, axis=1)


def _rows_from(ext, off, tm):
    if off == 0:
        return ext[:tm]
    return pltpu.roll(ext, ext.shape[0] - off, 0)[:tm]


def _tril_mask():
    t = lax.broadcasted_iota(jnp.int32, (CHUNK, CHUNK), 0)
    s = lax.broadcasted_iota(jnp.int32, (CHUNK, CHUNK), 1)
    return t >= s


def _mix_forward(ua, va, vb, gb, g_halo, ws_ref, bfull_ref, lng, lnb, cw_ref, cb, cg, cbeta, mixed_scr):
    tm, da = ua.shape
    heads = da // CHUNK
    u, du_fac = _gelu_and_grad(ua)
    vg, dvg_fac = _gelu_and_grad(va)
    vh, v_rs = _ln_stats(vg)
    v = vh * lng + lnb
    mask = _tril_mask()
    wm = [jnp.where(mask, ws_ref[h], 0.0).astype(BF16) for h in range(heads)]
    vb16 = v.astype(BF16)
    for ci in range(tm // CHUNK):
        rows = slice(ci * CHUNK, (ci + 1) * CHUNK)
        for h in range(heads):
            cols = slice(h * CHUNK, (h + 1) * CHUNK)
            mixed_scr[rows, cols] = jnp.dot(wm[h], vb16[rows, cols], preferred_element_type=F32) + bfull_ref[h]
    mixed = mixed_scr[...]
    a = u * mixed

    sg = _sigmoid(gb)
    g = vb * sg
    ext = jnp.concatenate([g_halo, g], axis=0)
    cpre = jnp.zeros_like(g) + cb
    for k in range(CONV_TAPS):
        cpre = cpre + _rows_from(ext, HALO - (CONV_TAPS - 1) + k, tm) * cw_ref[k:k + 1, :]
    chh, c_rss = _group_ln_fwd(cpre)
    cn = chh * cg + cbeta
    sc = _sigmoid(cn)
    cout = cn * sc
    return dict(u=u, du_fac=du_fac, dvg_fac=dvg_fac, vh=vh, v_rs=v_rs, v16=vb16, mixed=mixed, a=a, sg=sg, g=g,
                ext=ext, chh=chh, c_rss=c_rss, cn=cn, sc=sc, cout=cout)


def _mix_fwd_call(l, x, g1, win_g, lng, lnb, ws, bfull, cw, cb, cg, cbeta, wout_g):
    T, D = x.shape
    tm = min(TM_MIX, T)
    n_t = T // tm
    da = win_g.shape[-1]
    wo_rows = wout_g.shape[2]

    def body(x_ref, g1_ref, win_ref, lng_ref, lnb_ref, ws_ref, bf_ref, cw_ref, cb_ref, cg_ref, cbeta_ref, wout_ref,
             x1_ref, proj_ref, halo_scr, mixed_scr):
        i = pl.program_id(0)

        @pl.when(i == 0)
        def _():
            halo_scr[...] = jnp.zeros_like(halo_scr)

        x_t = x_ref[...]
        _, _, h1 = _rms_fwd(x_t, g1_ref[...])
        h1b = h1.astype(BF16)
        ps = []
        for p in range(N_CHIPS):
            pp = jnp.dot(h1b, win_ref[p], preferred_element_type=F32)
            proj_ref[:, p * da:(p + 1) * da] = pp
            ps.append(pp)
        f = _mix_forward(ps[0], ps[1], ps[2], ps[3], halo_scr[...], ws_ref, bf_ref, lng_ref[...], lnb_ref[...],
                         cw_ref, cb_ref[...], cg_ref[...], cbeta_ref[...], mixed_scr)
        halo_scr[...] = f["g"][tm - HALO:, :]
        mix = jnp.concatenate([f["a"], f["cout"]], axis=1).astype(BF16)
        acc = x_t
        for q in range(N_CHIPS):
            acc = acc + jnp.dot(mix[:, q * wo_rows:(q + 1) * wo_rows], wout_ref[q], preferred_element_type=F32)
        x1_ref[...] = acc

    full = lambda a: pl.BlockSpec(a.shape, lambda i: (0,) * a.ndim)
    return pl.pallas_call(
        body, name=f"mix_fwd_l{l}",
        grid=(n_t,),
        in_specs=[
            pl.BlockSpec((tm, D), lambda i: (i, 0)),
            full(g1),
            pl.BlockSpec((N_CHIPS, None, D, da), lambda i: (0, l, 0, 0)),
            full(lng), full(lnb), full(ws), full(bfull), full(cw), full(cb), full(cg), full(cbeta),
            pl.BlockSpec((N_CHIPS, None, wo_rows, D), lambda i: (0, l, 0, 0)),
        ],
        out_specs=[
            pl.BlockSpec((tm, D), lambda i: (i, 0)),
            pl.BlockSpec((tm, N_CHIPS * da), lambda i: (i, 0)),
        ],
        out_shape=[jax.ShapeDtypeStruct((T, D), F32), jax.ShapeDtypeStruct((T, N_CHIPS * da), F32)],
        scratch_shapes=[pltpu.VMEM((HALO, da), F32), pltpu.VMEM((tm, da), F32)],
        compiler_params=_params(("arbitrary",)),
    )(x, g1, win_g, lng, lnb, ws, bfull, cw, cb, cg, cbeta, wout_g)


def _ffn_tile(T):
    return min(TM_FFN, max(T // 2, CHUNK))


def _ffn_fwd_call(l, x1, g2, w1_g, w2_g):
    T, D = x1.shape
    tm = _ffn_tile(T)
    n_t = T // tm
    ffb = w1_g.shape[-1]

    def body(x1_ref, g2_ref, w1_ref, w2_ref, x2_ref, z_ref, h2_scr, acc_scr):
        p = pl.program_id(1)

        @pl.when(p == 0)
        def _():
            _, _, h2 = _rms_fwd(x1_ref[...], g2_ref[...])
            h2_scr[...] = h2.astype(BF16)
            acc_scr[...] = jnp.zeros_like(acc_scr)

        z = jnp.dot(h2_scr[...], w1_ref[...], preferred_element_type=F32)
        z_ref[...] = z.astype(BF16)
        rz = jnp.maximum(z, 0.0)
        acc_scr[...] += jnp.dot((rz * rz).astype(BF16), w2_ref[...], preferred_element_type=F32)

        @pl.when(p == N_CHIPS - 1)
        def _():
            x2_ref[...] = x1_ref[...] + acc_scr[...]

    return pl.pallas_call(
        body, name=f"ffn_fwd_l{l}",
        grid=(n_t, N_CHIPS),
        in_specs=[
            pl.BlockSpec((tm, D), lambda i, p: (i, 0)),
            pl.BlockSpec(g2.shape, lambda i, p: (0, 0)),
            pl.BlockSpec((None, None, D, ffb), lambda i, p: (p, l, 0, 0)),
            pl.BlockSpec((None, None, ffb, D), lambda i, p: (p, l, 0, 0)),
        ],
        out_specs=[
            pl.BlockSpec((tm, D), lambda i, p: (i, 0)),
            pl.BlockSpec((tm, ffb), lambda i, p: (i, p)),
        ],
        out_shape=[jax.ShapeDtypeStruct((T, D), F32), jax.ShapeDtypeStruct((T, N_CHIPS * ffb), BF16)],
        scratch_shapes=[pltpu.VMEM((tm, D), BF16), pltpu.VMEM((tm, D), F32)],
        compiler_params=_params(("arbitrary", "arbitrary")),
    )(x1, g2, w1_g, w2_g)


def _loss_call(x, gf, target):
    T, D = x.shape
    tm = min(TM_FFN, T)
    n_t = T // tm

    def body(x_ref, gf_ref, t_ref, dx_ref, dxb_ref, loss_ref, dgf_ref):
        i = pl.program_id(0)

        @pl.when(i == 0)
        def _():
            loss_ref[...] = jnp.zeros_like(loss_ref)
            dgf_ref[...] = jnp.zeros_like(dgf_ref)

        g = gf_ref[...]
        r, xh, y = _rms_fwd(x_ref[...], g)
        e = y - t_ref[...]
        per_tok = jnp.sum(e * e, axis=-1, keepdims=True) * (1.0 / D)
        loss_ref[...] += 0.5 * jnp.sum(per_tok, axis=0, keepdims=True)
        dy = e * (1.0 / D)
        dgf_ref[...] += jnp.sum(dy * xh, axis=0, keepdims=True)
        dx = _rms_bwd(dy, xh, r, g)
        dx_ref[...] = dx
        dxb_ref[...] = dx.astype(BF16)

    return pl.pallas_call(
        body, name="loss_head",
        grid=(n_t,),
        in_specs=[
            pl.BlockSpec((tm, D), lambda i: (i, 0)),
            pl.BlockSpec(gf.shape, lambda i: (0, 0)),
            pl.BlockSpec((tm, D), lambda i: (i, 0)),
        ],
        out_specs=[
            pl.BlockSpec((tm, D), lambda i: (i, 0)),
            pl.BlockSpec((tm, D), lambda i: (i, 0)),
            pl.BlockSpec((1, 1), lambda i: (0, 0)),
            pl.BlockSpec((1, D), lambda i: (0, 0)),
        ],
        out_shape=[jax.ShapeDtypeStruct((T, D), F32), jax.ShapeDtypeStruct((T, D), BF16),
                   jax.ShapeDtypeStruct((1, 1), F32), jax.ShapeDtypeStruct((1, D), F32)],
        compiler_params=_params(("arbitrary",)),
    )(x, gf, target)


def _ffn_bwd_call(l, dx2, dx2b, x1, z, g2, w1_g, w2_g):
    T, D = x1.shape
    tm = _ffn_tile(T)
    n_t = T // tm
    ffb = w1_g.shape[-1]

    def body(dx2_ref, dx2b_ref, x1_ref, z_ref, g2_ref, w1_ref, w2_ref,
             dx1_ref, dx1b_ref, dz_ref, ft_ref, h2t_ref, dg2_ref, acc_scr):
        i = pl.program_id(0)
        p = pl.program_id(1)

        @pl.when(jnp.logical_and(i == 0, p == 0))
        def _():
            dg2_ref[...] = jnp.zeros_like(dg2_ref)

        @pl.when(p == 0)
        def _():
            _, _, h2 = _rms_fwd(x1_ref[...], g2_ref[...])
            h2t_ref[...] = h2.T.astype(BF16)
            acc_scr[...] = jnp.zeros_like(acc_scr)

        rz = jnp.maximum(z_ref[...].astype(F32), 0.0)
        ft_ref[...] = (rz * rz).T.astype(BF16)
        df = _dot_nt(dx2b_ref[...], w2_ref[...])
        dz = (df * (2.0 * rz)).astype(BF16)
        dz_ref[...] = dz
        acc_scr[...] += _dot_nt(dz, w1_ref[...])

        @pl.when(p == N_CHIPS - 1)
        def _():
            g = g2_ref[...]
            r, xh, _ = _rms_fwd(x1_ref[...], g)
            dh2 = acc_scr[...]
            dg2_ref[...] += jnp.sum(dh2 * xh, axis=0, keepdims=True)
            dx1 = dx2_ref[...] + _rms_bwd(dh2, xh, r, g)
            dx1_ref[...] = dx1
            dx1b_ref[...] = dx1.astype(BF16)

    return pl.pallas_call(
        body, name=f"ffn_bwd_l{l}",
        grid=(n_t, N_CHIPS),
        in_specs=[
            pl.BlockSpec((tm, D), lambda i, p: (i, 0)),
            pl.BlockSpec((tm, D), lambda i, p: (i, 0)),
            pl.BlockSpec((tm, D), lambda i, p: (i, 0)),
            pl.BlockSpec((tm, ffb), lambda i, p: (i, p)),
            pl.BlockSpec(g2.shape, lambda i, p: (0, 0)),
            pl.BlockSpec((None, None, D, ffb), lambda i, p: (p, l, 0, 0)),
            pl.BlockSpec((None, None, ffb, D), lambda i, p: (p, l, 0, 0)),
        ],
        out_specs=[
            pl.BlockSpec((tm, D), lambda i, p: (i, 0)),
            pl.BlockSpec((tm, D), lambda i, p: (i, 0)),
            pl.BlockSpec((tm, ffb), lambda i, p: (i, p)),
            pl.BlockSpec((ffb, tm), lambda i, p: (p, i)),
            pl.BlockSpec((D, tm), lambda i, p: (0, i)),
            pl.BlockSpec((1, D), lambda i, p: (0, 0)),
        ],
        out_shape=[jax.ShapeDtypeStruct((T, D), F32), jax.ShapeDtypeStruct((T, D), BF16),
                   jax.ShapeDtypeStruct((T, N_CHIPS * ffb), BF16), jax.ShapeDtypeStruct((N_CHIPS * ffb, T), BF16),
                   jax.ShapeDtypeStruct((D, T), BF16), jax.ShapeDtypeStruct((1, D), F32)],
        scratch_shapes=[pltpu.VMEM((tm, D), F32)],
        compiler_params=_params(("arbitrary", "arbitrary")),
    )(dx2, dx2b, x1, z, g2, w1_g, w2_g)


def _mix_bwd_call(l, dx1, dx1b, x, proj, g1, win_g, lng, lnb, ws, wst, bfull, cw, cb, cg, cbeta, wout_g):
    T, D = x.shape
    tm = min(TM_MIX, T)
    n_t = T // tm
    da = win_g.shape[-1]
    heads = da // CHUNK
    wo_rows = wout_g.shape[2]
    halo_blocks = tm // HALO

    def body(dx1_ref, dx1b_ref, x_ref, proj_ref, vbh_ref, gbh_ref, g1_ref, win_ref, lng_ref, lnb_ref, ws_ref, wst_ref,
             bf_ref, cw_ref, cb_ref, cg_ref, cbeta_ref, wout_ref,
             dx_ref, dxb_ref, dproj_ref, h1t_ref, mixt_ref,
             dg1_ref, dlng_ref, dlnb_ref, dws_ref, dbs_ref, dcw_ref, dcb_ref, dcg_ref, dcbeta_ref,
             carry_scr, mixed_scr, dv_scr):
        i = pl.program_id(0)
        tile = n_t - 1 - i

        @pl.when(i == 0)
        def _():
            carry_scr[...] = jnp.zeros_like(carry_scr)
            for ref in (dg1_ref, dlng_ref, dlnb_ref, dws_ref, dbs_ref, dcw_ref, dcb_ref, dcg_ref, dcbeta_ref):
                ref[...] = jnp.zeros_like(ref)

        g1v = g1_ref[...]
        r, xh, h1 = _rms_fwd(x_ref[...], g1v)
        h1t_ref[...] = h1.T.astype(BF16)

        ua = proj_ref[:, 0 * da:1 * da]
        va = proj_ref[:, 1 * da:2 * da]
        vb = proj_ref[:, 2 * da:3 * da]
        gb = proj_ref[:, 3 * da:4 * da]
        g_halo = jnp.where(tile > 0, vbh_ref[...] * _sigmoid(gbh_ref[...]), 0.0)
        lng_v, cg_v = lng_ref[...], cg_ref[...]
        f = _mix_forward(ua, va, vb, gb, g_halo, ws_ref, bf_ref, lng_v, lnb_ref[...], cw_ref, cb_ref[...], cg_v,
                         cbeta_ref[...], mixed_scr)
        mixt_ref[0:da, :] = f["a"].T.astype(BF16)
        mixt_ref[da:2 * da, :] = f["cout"].T.astype(BF16)

        dxo = dx1b_ref[...]
        dmix = jnp.concatenate([_dot_nt(dxo, wout_ref[q]) for q in range(N_CHIPS)], axis=1)
        da_ = dmix[:, :da]
        dc_ = dmix[:, da:]

        dua = da_ * f["mixed"] * f["du_fac"]
        dmixed = (da_ * f["u"]).astype(BF16)
        mask_t = (lax.broadcasted_iota(jnp.int32, (CHUNK, CHUNK), 1)
                  >= lax.broadcasted_iota(jnp.int32, (CHUNK, CHUNK), 0))
        wmt =[jnp.where(mask_t, wst_ref[h], 0.0).astype(BF16) for h in range(heads)]
        mask = _tril_mask()
        v16 = f["v16"]
        for h in range(heads):
            cols = slice(h * CHUNK, (h + 1) * CHUNK)
            dws_h = jnp.zeros((CHUNK, CHUNK), F32)
            dbs_h = jnp.zeros((CHUNK, CHUNK), F32)
            for ci in range(tm // CHUNK):
                rows = slice(ci * CHUNK, (ci + 1) * CHUNK)
                dm = dmixed[rows, cols]
                dv_scr[rows, cols] = jnp.dot(wmt[h], dm, preferred_element_type=F32)
                dws_h = dws_h + _dot_nt(dm, v16[rows, cols])
                dbs_h = dbs_h + dm.astype(F32)
            dws_ref[h] += jnp.where(mask, dws_h, 0.0)
            dbs_ref[h] += jnp.broadcast_to(jnp.sum(dbs_h, axis=1, keepdims=True), (CHUNK, CHUNK))
        dv = dv_scr[...]
        dlng_ref[...] += jnp.sum(dv * f["vh"], axis=0, keepdims=True)
        dlnb_ref[...] += jnp.sum(dv, axis=0, keepdims=True)
        dva = _ln_bwd(dv * lng_v, f["vh"], f["v_rs"]) * f["dvg_fac"]

        cn, sc = f["cn"], f["sc"]
        dcn = dc_ * (sc * (1.0 + cn * (1.0 - sc)))
        dcg_ref[...] += jnp.sum(dcn * f["chh"], axis=0, keepdims=True)
        dcbeta_ref[...] += jnp.sum(dcn, axis=0, keepdims=True)
        dcpre = _group_ln_bwd(dcn * cg_v, f["chh"], f["c_rss"])
        dcb_ref[...] += jnp.sum(dcpre, axis=0, keepdims=True)
        ext = f["ext"]
        dext = jnp.concatenate([dcpre, carry_scr[...]], axis=0)
        dg = jnp.zeros_like(dcpre)
        for k in range(CONV_TAPS):
            g_k = _rows_from(ext, HALO - (CONV_TAPS - 1) + k, tm)
            dcw_ref[k:k + 1, :] += jnp.sum(dcpre * g_k, axis=0, keepdims=True)
            dg = dg + _rows_from(dext, CONV_TAPS - 1 - k, tm) * cw_ref[k:k + 1, :]
        carry_scr[...] = dcpre[:HALO, :]
        sg = f["sg"]
        dvb = dg * sg
        dgb = dg * vb * sg * (1.0 - sg)

        dps = [dua.astype(BF16), dva.astype(BF16), dvb.astype(BF16), dgb.astype(BF16)]
        dh1 = jnp.zeros((tm, D), F32)
        for p in range(N_CHIPS):
            dproj_ref[:, p * da:(p + 1) * da] = dps[p]
            dh1 = dh1 + _dot_nt(dps[p], win_ref[p])
        dg1_ref[...] += jnp.sum(dh1 * xh, axis=0, keepdims=True)
        dx = dx1_ref[...] + _rms_bwd(dh1, xh, r, g1v)
        dx_ref[...] = dx
        dxb_ref[...] = dx.astype(BF16)

    rev = lambda i: (n_t - 1 - i, 0)
    full = lambda a: pl.BlockSpec(a.shape, lambda i: (0,) * a.ndim)
    acc = lambda shape: pl.BlockSpec(shape, lambda i: (0,) * len(shape))
    halo_idx = lambda col: (lambda i: (jnp.maximum((n_t - 1 - i) * halo_blocks - 1, 0), col))
    small_shapes = [(1, D), (1, da), (1, da), (heads, CHUNK, CHUNK), (heads, CHUNK, CHUNK), (HALO, da),
                    (1, da), (1, da), (1, da)]
    return pl.pallas_call(
        body, name=f"mix_bwd_l{l}",
        grid=(n_t,),
        in_specs=[
            pl.BlockSpec((tm, D), rev),
            pl.BlockSpec((tm, D), rev),
            pl.BlockSpec((tm, D), rev),
            pl.BlockSpec((tm, N_CHIPS * da), rev),
            pl.BlockSpec((HALO, da), halo_idx(2)),
            pl.BlockSpec((HALO, da), halo_idx(3)),
            full(g1),
            pl.BlockSpec((N_CHIPS, None, D, da), lambda i: (0, l, 0, 0)),
            full(lng), full(lnb), full(ws), full(wst), full(bfull), full(cw), full(cb), full(cg), full(cbeta),
            pl.BlockSpec((N_CHIPS, None, wo_rows, D), lambda i: (0, l, 0, 0)),
        ],
        out_specs=[
            pl.BlockSpec((tm, D), rev),
            pl.BlockSpec((tm, D), rev),
            pl.BlockSpec((tm, N_CHIPS * da), rev),
            pl.BlockSpec((D, tm), lambda i: (0, n_t - 1 - i)),
            pl.BlockSpec((2 * da, tm), lambda i: (0, n_t - 1 - i)),
        ] + [acc(s) for s in small_shapes],
        out_shape=[jax.ShapeDtypeStruct((T, D), F32), jax.ShapeDtypeStruct((T, D), BF16),
                   jax.ShapeDtypeStruct((T, N_CHIPS * da), BF16), jax.ShapeDtypeStruct((D, T), BF16),
                   jax.ShapeDtypeStruct((2 * da, T), BF16)] + [jax.ShapeDtypeStruct(s, F32) for s in small_shapes],
        scratch_shapes=[pltpu.VMEM((HALO, da), F32), pltpu.VMEM((tm, da), F32), pltpu.VMEM((tm, da), F32)],
        compiler_params=_params(("arbitrary",)),
    )(dx1, dx1b, x, proj, proj, proj, g1, win_g, lng, lnb, ws, wst, bfull, cw, cb, cg, cbeta, wout_g)


def _wgrad_call(name, at, b, split_cols, steps):
    M, T = at.shape
    N = b.shape[1]
    if split_cols:
        bn = N // (N_CHIPS * steps)
        in_specs = [pl.BlockSpec((M, T), lambda j: (0, 0)), pl.BlockSpec((T, bn), lambda j: (0, j))]
        out_spec = pl.BlockSpec((None, M, bn), lambda j: (j // steps, 0, j % steps))
        out_shape = jax.ShapeDtypeStruct((N_CHIPS, M, N // N_CHIPS), BF16)
    else:
        bm = M // (N_CHIPS * steps)
        in_specs = [pl.BlockSpec((bm, T), lambda j: (j, 0)), pl.BlockSpec((T, N), lambda j: (0, 0))]
        out_spec = pl.BlockSpec((None, bm, N), lambda j: (j // steps, j % steps, 0))
        out_shape = jax.ShapeDtypeStruct((N_CHIPS, M // N_CHIPS, N), BF16)

    def body(at_ref, b_ref, o_ref):
        o_ref[...] = jnp.dot(at_ref[...], b_ref[...], preferred_element_type=F32).astype(BF16)

    return pl.pallas_call(
        body, name=name, grid=(N_CHIPS * steps,), in_specs=in_specs, out_specs=out_spec, out_shape=out_shape,
        compiler_params=_params(("arbitrary",)),
    )(at, b)


def _rows_block(rows, cols):
    br = rows
    while br * cols * 4 > 2 ** 20 and br % 16 == 0:
        br //= 2
    return br


def _pair_sum_call(name, core, g0, g1, got):
    rows, cols = got.shape
    br = _rows_block(rows, cols)

    def body(core_ref, g0_ref, g1_ref, got_ref, o_ref):
        mine = jnp.where(core_ref[0] == 0, g0_ref[...].astype(F32), g1_ref[...].astype(F32))
        o_ref[...] = (mine + got_ref[...].astype(F32)).astype(BF16)

    spec = pl.BlockSpec((br, cols), lambda i, core_ref: (i, 0))
    return pl.pallas_call(
        body, name=name,
        grid_spec=pltpu.PrefetchScalarGridSpec(num_scalar_prefetch=1, grid=(rows // br,),
                                               in_specs=[spec, spec, spec], out_specs=spec),
        out_shape=jax.ShapeDtypeStruct((rows, cols), BF16),
        compiler_params=_params(("arbitrary",)),
    )(core, g0, g1, got)


def _chip_sum_call(name, parts):
    rows = parts.shape[0] // N_CHIPS
    cols = parts.shape[1]
    br = _rows_block(rows, cols)
    n_b = rows // br

    def body(*refs):
        o_ref = refs[N_CHIPS]
        total = refs[0][...].astype(F32)
        for q in range(1, N_CHIPS):
            total = total + refs[q][...].astype(F32)
        o_ref[...] = total

    return pl.pallas_call(
        body, name=name, grid=(n_b,),
        in_specs=[pl.BlockSpec((br, cols), functools.partial(lambda i, q: (q * n_b + i, 0), q=q))
                  for q in range(N_CHIPS)],
        out_specs=pl.BlockSpec((br, cols), lambda i: (i, 0)),
        out_shape=jax.ShapeDtypeStruct((rows, cols), F32),
        compiler_params=_params(("arbitrary",)),
    )(*([parts] * N_CHIPS))


def _adamw_call(name, w, g, m, v):
    rows, cols = w.shape
    br = _rows_block(rows, cols)

    def body(w_ref, g_ref, m_ref, v_ref, d_ref, nm_ref, nv_ref):
        gv = g_ref[...]
        m_new = ADAM_B1 * m_ref[...] + (1.0 - ADAM_B1) * gv
        v_new = ADAM_B2 * v_ref[...] + (1.0 - ADAM_B2) * (gv * gv)
        m_hat = m_new / (1.0 - ADAM_B1 ** ADAM_STEP)
        v_hat = v_new / (1.0 - ADAM_B2 ** ADAM_STEP)
        d_ref[...] = -ADAM_LR * (m_hat / (jnp.sqrt(v_hat) + ADAM_EPS) + ADAM_WD * w_ref[...])
        nm_ref[...] = m_new
        nv_ref[...] = v_new

    spec = pl.BlockSpec((br, cols), lambda i: (i, 0))
    return pl.pallas_call(
        body, name=name, grid=(rows // br,), in_specs=[spec] * 4, out_specs=[spec] * 3,
        out_shape=[jax.ShapeDtypeStruct((rows, cols), F32)] * 3,
        compiler_params=_params(("arbitrary",)),
    )(w, g, m, v)


ANY = pl.BlockSpec(memory_space=pl.ANY)


def _place():
    x, y, c = lax.axis_index("x"), lax.axis_index("y"), lax.axis_index("c")
    chips = [(1 - x, y), (x, 1 - y), (1 - x, 1 - y)]
    return x, y, c, 2 * x + y, chips


def _gather_weights_call(shards):
    n = len(shards)

    def body(*refs):
        ins, outs = refs[:n], refs[n:2 * n]
        send_sems, recv_sems, local_sems = refs[2 * n:]
        x, y, c, p, chips = _place()
        sibling = (x, y, 1 - c)
        qs = [2 * cx + cy for cx, cy in chips]

        def remote(k, j, src, dst, to):
            return pltpu.make_async_remote_copy(src_ref=src, dst_ref=dst, send_sem=send_sems.at[k, j],
                                                recv_sem=recv_sems.at[k, j], device_id=to, device_id_type=MESH)

        own = [pltpu.make_async_copy(ins[k], outs[k].at[p], local_sems.at[k]) for k in range(n)]
        for cp in own:
            cp.start()
        sent = []
        for j, chip in enumerate(chips):
            for k in range(n):
                cp = remote(k, j, ins[k].at[c], outs[k].at[p, c], (*chip, c))
                cp.start()
                sent.append(cp)
        for j in range(len(chips)):
            for k in range(n):
                landed = outs[k].at[qs[j], c]
                remote(k, j, ins[k].at[c], landed, sibling).wait_recv()
                cp = remote(k, len(chips) + j, landed, landed, sibling)
                cp.start()
                sent.append(cp)
        for j in range(len(chips)):
            for k in range(n):
                remote(k, len(chips) + j, ins[k].at[c], outs[k].at[qs[j], 1 - c], sibling).wait_recv()
        for cp in sent:
            cp.wait_send()
        for cp in own:
            cp.wait()

    return pl.pallas_call(
        body, name="gather_weights",
        in_specs=[ANY] * n, out_specs=[ANY] * n,
        out_shape=[jax.ShapeDtypeStruct((N_CHIPS,) + s.shape, s.dtype) for s in shards],
        scratch_shapes=[pltpu.SemaphoreType.DMA((n, 6)), pltpu.SemaphoreType.DMA((n, 6)),
                        pltpu.SemaphoreType.DMA((n,))],
    )(*shards)


def _pair_exchange_call(g0s, g1s):
    n = len(g0s)

    def body(*refs):
        g0, g1, got = refs[:n], refs[n:2 * n], refs[2 * n:3 * n]
        send_sems, recv_sems = refs[3 * n:]
        x, y, c, _, _ = _place()
        sibling = (x, y, 1 - c)

        def remote(k, src):
            return pltpu.make_async_remote_copy(src_ref=src, dst_ref=got[k], send_sem=send_sems.at[k],
                                                recv_sem=recv_sems.at[k], device_id=sibling, device_id_type=MESH)

        @pl.when(c == 0)
        def _():
            for k in range(n):
                remote(k, g1[k]).start()

        @pl.when(c == 1)
        def _():
            for k in range(n):
                remote(k, g0[k]).start()

        for k in range(n):
            cp = remote(k, g0[k])
            cp.wait_send()
            cp.wait_recv()

    return pl.pallas_call(
        body, name="grad_pair_exchange",
        in_specs=[ANY] * (2 * n), out_specs=[ANY] * n,
        out_shape=[jax.ShapeDtypeStruct(g.shape, g.dtype) for g in g0s],
        scratch_shapes=[pltpu.SemaphoreType.DMA((n,)), pltpu.SemaphoreType.DMA((n,))],
    )(*g0s, *g1s)


def _chip_exchange_call(sums):
    n = len(sums)

    def body(*refs):
        ins, outs = refs[:n], refs[n:2 * n]
        send_sems, recv_sems, local_sems = refs[2 * n:]
        x, y, c, p, chips = _place()
        qs = [2 * cx + cy for cx, cy in chips]

        def remote(k, j, src, dst, to):
            return pltpu.make_async_remote_copy(src_ref=src, dst_ref=dst, send_sem=send_sems.at[k, j],
                                                recv_sem=recv_sems.at[k, j], device_id=to, device_id_type=MESH)

        own = [pltpu.make_async_copy(ins[k].at[p], outs[k].at[p], local_sems.at[k]) for k in range(n)]
        for cp in own:
            cp.start()
        sent = []
        for j, chip in enumerate(chips):
            for k in range(n):
                cp = remote(k, j, ins[k].at[qs[j]], outs[k].at[p], (*chip, c))
                cp.start()
                sent.append(cp)
        for j in range(len(chips)):
            for k in range(n):
                remote(k, j, ins[k].at[qs[j]], outs[k].at[qs[j]], (x, y, c)).wait_recv()
        for cp in sent:
            cp.wait_send()
        for cp in own:
            cp.wait()

    return pl.pallas_call(
        body, name="grad_chip_exchange",
        in_specs=[ANY] * n, out_specs=[ANY] * n,
        out_shape=[jax.ShapeDtypeStruct(s.shape, s.dtype) for s in sums],
        scratch_shapes=[pltpu.SemaphoreType.DMA((n, 3)), pltpu.SemaphoreType.DMA((n, 3)),
                        pltpu.SemaphoreType.DMA((n,))],
    )(*sums)


def _pair_gather_call(reduced):
    n = len(reduced)

    def body(*refs):
        ins, outs = refs[:n], refs[n:2 * n]
        send_sems, recv_sems, local_sems = refs[2 * n:]
        x, y, c, _, _ = _place()
        sibling = (x, y, 1 - c)

        def remote(k, dst):
            return pltpu.make_async_remote_copy(src_ref=ins[k], dst_ref=dst, send_sem=send_sems.at[k],
                                                recv_sem=recv_sems.at[k], device_id=sibling, device_id_type=MESH)

        own = [pltpu.make_async_copy(ins[k], outs[k].at[c], local_sems.at[k]) for k in range(n)]
        sent = [remote(k, outs[k].at[c]) for k in range(n)]
        for cp in own + sent:
            cp.start()
        for k in range(n):
            remote(k, outs[k].at[1 - c]).wait_recv()
        for cp in sent:
            cp.wait_send()
        for cp in own:
            cp.wait()

    return pl.pallas_call(
        body, name="grad_pair_gather",
        in_specs=[ANY] * n, out_specs=[ANY] * n,
        out_shape=[jax.ShapeDtypeStruct((2,) + r.shape, r.dtype) for r in reduced],
        scratch_shapes=[pltpu.SemaphoreType.DMA((n,)), pltpu.SemaphoreType.DMA((n,)),
                        pltpu.SemaphoreType.DMA((n,))],
    )(*reduced)


def _all_sum_small_call(block):
    m_per, n = block.shape

    def body(x_ref, sum_ref, all_ref, send_sems, recv_sems, local_sem):
        x, y, c, _, chip_list = _place()
        me, sibling = (x, y, c), (x, y, 1 - c)

        def rows(px, py, pc):
            return all_ref.at[pl.ds((4 * px + 2 * py + pc) * m_per, m_per), :]

        def copy(k, blk, to, src=None):
            return pltpu.make_async_remote_copy(src_ref=rows(*blk) if src is None else src, dst_ref=rows(*blk),
                                                send_sem=send_sems.at[k], recv_sem=recv_sems.at[k],
                                                device_id=to, device_id_type=MESH)

        mine = pltpu.make_async_copy(x_ref, rows(*me), local_sem)
        mine.start()
        first = [copy(0, me, sibling, src=x_ref)]
        first += [copy(1 + j, me, (*chip, c), src=x_ref) for j, chip in enumerate(chip_list)]
        for cp in first:
            cp.start()
        passed = [copy(4 + j, (*chip, c), sibling) for j, chip in enumerate(chip_list)]
        for j, chip in enumerate(chip_list):
            copy(1 + j, (*chip, c), me).wait_recv()
            passed[j].start()
        copy(0, sibling, me).wait_recv()
        for j, chip in enumerate(chip_list):
            copy(4 + j, (*chip, 1 - c), me).wait_recv()
        for cp in first + passed:
            cp.wait_send()
        mine.wait()
        total = all_ref[0:m_per, :]
        for d in range(1, N_DEV):
            total = total + all_ref[d * m_per:(d + 1) * m_per, :]
        sum_ref[...] = total

    vmem = pl.BlockSpec(memory_space=pltpu.VMEM)
    return pl.pallas_call(
        body, name="small_all_sum",
        in_specs=[vmem], out_specs=[vmem, vmem],
        out_shape=[jax.ShapeDtypeStruct((m_per, n), F32), jax.ShapeDtypeStruct((N_DEV * m_per, n), F32)],
        scratch_shapes=[pltpu.SemaphoreType.DMA((7,)), pltpu.SemaphoreType.DMA((7,)), pltpu.SemaphoreType.DMA],
        compiler_params=pltpu.CompilerParams(vmem_limit_bytes=VMEM_LIMIT),
    )(block)[0]


SMALL_NAMES = ["norm1_g", "sgu_ln_g", "sgu_ln_b", "sgu_w", "sgu_b", "conv_b", "conv_ln_g", "conv_ln_b", "norm2_g",
               "final_g"]
BIG_NAMES = ["w_in", "w_out", "w_ff1", "w_ff2"]


def _rows128(a):
    return a.reshape(-1, LANES)


def kernel(x, norm1_g, w_in, sgu_ln_g, sgu_ln_b, sgu_w, sgu_b, conv_w, conv_b, conv_ln_g, conv_ln_b, w_out, norm2_g, w_ff1, w_ff2, final_g, loss_target, m_norm1_g, m_w_in, m_sgu_ln_g, m_sgu_ln_b, m_sgu_w, m_sgu_b, m_conv_w, m_conv_b, m_conv_ln_g, m_conv_ln_b, m_w_out, m_norm2_g, m_w_ff1, m_w_ff2, m_final_g, v_norm1_g, v_w_in, v_sgu_ln_g, v_sgu_ln_b, v_sgu_w, v_sgu_b, v_conv_w, v_conv_b, v_conv_ln_g, v_conv_ln_b, v_w_out, v_norm2_g, v_w_ff1, v_w_ff2, v_final_g):
    w = dict(norm1_g=norm1_g, w_in=w_in, sgu_ln_g=sgu_ln_g, sgu_ln_b=sgu_ln_b, sgu_w=sgu_w, sgu_b=sgu_b,
             conv_w=conv_w, conv_b=conv_b, conv_ln_g=conv_ln_g, conv_ln_b=conv_ln_b, w_out=w_out, norm2_g=norm2_g,
             w_ff1=w_ff1, w_ff2=w_ff2, final_g=final_g)
    m = dict(norm1_g=m_norm1_g, w_in=m_w_in, sgu_ln_g=m_sgu_ln_g, sgu_ln_b=m_sgu_ln_b, sgu_w=m_sgu_w, sgu_b=m_sgu_b,
             conv_w=m_conv_w, conv_b=m_conv_b, conv_ln_g=m_conv_ln_g, conv_ln_b=m_conv_ln_b, w_out=m_w_out,
             norm2_g=m_norm2_g, w_ff1=m_w_ff1, w_ff2=m_w_ff2, final_g=m_final_g)
    v = dict(norm1_g=v_norm1_g, w_in=v_w_in, sgu_ln_g=v_sgu_ln_g, sgu_ln_b=v_sgu_ln_b, sgu_w=v_sgu_w, sgu_b=v_sgu_b,
             conv_w=v_conv_w, conv_b=v_conv_b, conv_ln_g=v_conv_ln_g, conv_ln_b=v_conv_ln_b, w_out=v_w_out,
             norm2_g=v_norm2_g, w_ff1=v_w_ff1, w_ff2=v_w_ff2, final_g=v_final_g)
    depth = w_in.shape[0]
    assert depth == 2, "core c owns layer c of every gradient"
    T, D = x.shape[1], x.shape[2]
    heads = sgu_w.shape[1]
    da = heads * CHUNK
    core = lax.axis_index("c")
    chip = 2 * lax.axis_index("x") + lax.axis_index("y")

    cw_pad = jnp.pad(conv_w, ((0, 0), (0, HALO - CONV_TAPS), (0, 0)))
    win_g, wout_g, w1_g, w2_g, cw_g = _gather_weights_call(
        [w_in.astype(BF16), w_out.astype(BF16), w_ff1.astype(BF16), w_ff2.astype(BF16), cw_pad])
    cw_full = jnp.transpose(cw_g, (1, 2, 0, 3)).reshape(depth, HALO, da)

    ws_t = jnp.swapaxes(sgu_w, -1, -2)
    b_full = jnp.broadcast_to(sgu_b[..., None], sgu_w.shape)
    row = lambda a, l: a[l:l + 1]

    xs, projs, x1s, zs = [], [], [], []
    h = x.reshape(T, D)
    for l in range(depth):
        xs.append(h)
        x1, proj = _mix_fwd_call(l, h, row(norm1_g, l), win_g, row(sgu_ln_g, l), row(sgu_ln_b, l), sgu_w[l],
                                 b_full[l], cw_full[l], row(conv_b, l), row(conv_ln_g, l), row(conv_ln_b, l), wout_g)
        h, z = _ffn_fwd_call(l, x1, row(norm2_g, l), w1_g, w2_g)
        projs.append(proj)
        x1s.append(x1)
        zs.append(z)
    dx, dxb, loss, d_final_g = _loss_call(h, final_g.reshape(1, D), loss_target.reshape(T, D))

    big = {name: [None] * depth for name in BIG_NAMES}
    small = {name: [None] * depth for name in SMALL_NAMES[:-1] + ["conv_w"]}
    for l in reversed(range(depth)):
        dx1, dx1b, dz, f_t, h2_t, dg2 = _ffn_bwd_call(l, dx, dxb, x1s[l], zs[l], row(norm2_g, l), w1_g, w2_g)
        big["w_ff2"][l] = _wgrad_call(f"wgrad_ff2_l{l}", f_t, dxb, False, 2)
        big["w_ff1"][l] = _wgrad_call(f"wgrad_ff1_l{l}", h2_t, dz, True, 2)
        (dx, dxb, dproj, h1_t, mix_t, dg1, dlng, dlnb, dws, dbs, dcw, dcb, dcg, dcbeta) = _mix_bwd_call(
            l, dx1, dx1b, xs[l], projs[l], row(norm1_g, l), win_g, row(sgu_ln_g, l), row(sgu_ln_b, l), sgu_w[l],
            ws_t[l], b_full[l], cw_full[l], row(conv_b, l), row(conv_ln_g, l), row(conv_ln_b, l), wout_g)
        big["w_out"][l] = _wgrad_call(f"wgrad_out_l{l}", mix_t, dx1b, False, 1)
        big["w_in"][l] = _wgrad_call(f"wgrad_in_l{l}", h1_t, dproj, True, 1)
        small["norm1_g"][l] = dg1[0]
        small["sgu_ln_g"][l] = dlng[0]
        small["sgu_ln_b"][l] = dlnb[0]
        small["sgu_w"][l] = dws
        small["sgu_b"][l] = dbs[:, :, 0]
        small["conv_w"][l] = dcw[:CONV_TAPS]
        small["conv_b"][l] = dcb[0]
        small["conv_ln_g"][l] = dcg[0]
        small["conv_ln_b"][l] = dcbeta[0]
        small["norm2_g"][l] = dg2[0]
    grad_x = dx.reshape(x.shape)

    core_arr = core.reshape(1).astype(jnp.int32)
    got = _pair_exchange_call([big[n][0] for n in BIG_NAMES], [big[n][1] for n in BIG_NAMES])
    sums = []
    for name, recv in zip(BIG_NAMES, got):
        shape = recv.shape
        flat = lambda a: a.reshape(-1, shape[-1])
        sums.append(_pair_sum_call(f"pair_sum_{name}", core_arr, flat(big[name][0]), flat(big[name][1]),
                                   flat(recv)).reshape(shape))
    parts = _chip_exchange_call(sums)
    reduced = [_chip_sum_call(f"chip_sum_{name}", part.reshape(-1, part.shape[-1])).reshape(part.shape[1:])
               for name, part in zip(BIG_NAMES, parts)]
    big_grads = dict(zip(BIG_NAMES, _pair_gather_call(reduced)))

    small_local = {name: jnp.stack(small[name]) for name in small}
    small_local["final_g"] = d_final_g[0]
    pieces = [_rows128(small_local[name]) for name in SMALL_NAMES]
    pieces.append(_rows128(small_local["conv_w"]))
    pieces.append(jnp.broadcast_to(loss, (8, LANES)))
    offsets = [0]
    for piece in pieces:
        offsets.append(offsets[-1] + piece.shape[0])
    summed = _all_sum_small_call(jnp.concatenate(pieces, axis=0))
    n_small = offsets[len(SMALL_NAMES)]
    loss_out = summed[offsets[-2], 0]
    small_grads = {name: summed[offsets[k]:offsets[k + 1]].reshape(w[name].shape)
                   for k, name in enumerate(SMALL_NAMES)}
    conv_w_full = summed[offsets[-3]:offsets[-2]].reshape(depth, CONV_TAPS, da)
    conv_w_grad = lax.dynamic_slice_in_dim(conv_w_full, chip * conv_w.shape[-1], conv_w.shape[-1], axis=2)

    grads, delta, new_m, new_v = {}, {}, {}, {}
    for name in BIG_NAMES:
        shape = w[name].shape
        flat = lambda a: a.reshape(-1, shape[-1])
        grads[name] = big_grads[name]
        d_, m_, v_ = _adamw_call(f"adamw_{name}", flat(w[name]), flat(big_grads[name]), flat(m[name]), flat(v[name]))
        delta[name], new_m[name], new_v[name] = d_.reshape(shape), m_.reshape(shape), v_.reshape(shape)
    pack = lambda src: jnp.concatenate([_rows128(src[name]) for name in SMALL_NAMES], axis=0)
    d_, m_, v_ = _adamw_call("adamw_small", pack(w), summed[:n_small], pack(m), pack(v))
    for k, name in enumerate(SMALL_NAMES):
        sl = slice(offsets[k], offsets[k + 1])
        grads[name] = small_grads[name]
        delta[name] = d_[sl].reshape(w[name].shape)
        new_m[name] = m_[sl].reshape(w[name].shape)
        new_v[name] = v_[sl].reshape(w[name].shape)
    cshape = conv_w.shape
    flat = lambda a: a.reshape(-1, cshape[-1])
    d_, m_, v_ = _adamw_call("adamw_conv_w", flat(conv_w), flat(conv_w_grad), flat(m["conv_w"]), flat(v["conv_w"]))
    grads["conv_w"] = conv_w_grad
    delta["conv_w"], new_m["conv_w"], new_v["conv_w"] = d_.reshape(cshape), m_.reshape(cshape), v_.reshape(cshape)

    order = ["norm1_g", "w_in", "sgu_ln_g", "sgu_ln_b", "sgu_w", "sgu_b", "conv_w", "conv_b", "conv_ln_g",
             "conv_ln_b", "w_out", "norm2_g", "w_ff1", "w_ff2", "final_g"]
    return (loss_out, grad_x, *[grads[n] for n in order], *[delta[n] for n in order],
            *[new_m[n] for n in order], *[new_v[n] for n in order])
```

```python
import functools

import jax
import jax.numpy as jnp
from jax import lax
from jax.experimental import pallas as pl
from jax.experimental.pallas import tpu as pltpu

F32 = jnp.float32
BF16 = jnp.bfloat16
MESH = pl.DeviceIdType.MESH

EPS = 1e-6
CHUNK = 128
CONV_TAPS = 31
HALO = 32
N_CHIPS = 4
N_DEV = 8
LANES = 128

ADAM_LR = 0.001
ADAM_B1 = 0.9
ADAM_B2 = 0.999
ADAM_EPS = 1e-08
ADAM_WD = 0.01
ADAM_STEP = 10

TM_MIX = 256
TM_FFN = 512
V7X_VMEM_BYTES = 64 * 2 ** 20
VMEM_LIMIT = V7X_VMEM_BYTES - 8 * 2 ** 20


def _params(sem=None):
    return pltpu.CompilerParams(dimension_semantics=sem, vmem_limit_bytes=VMEM_LIMIT)


def _erf(x):
    ax = jnp.abs(x)
    t = 1.0 / (1.0 + 0.3275911 * ax)
    poly = t * (0.254829592 + t * (-0.284496736 + t * (1.421413741 + t * (-1.453152027 + t * 1.061405429))))
    y = 1.0 - poly * jnp.exp(-ax * ax)
    return jnp.where(x < 0, -y, y)


def _gelu_and_grad(x):
    cdf = 0.5 * (1.0 + _erf(x * 0.7071067811865476))
    pdf = jnp.exp(-0.5 * x * x) * 0.3989422804014327
    return x * cdf, cdf + x * pdf


def _sigmoid(x):
    return 1.0 / (1.0 + jnp.exp(-x))


def _dot(a, b):
    return jnp.dot(a.astype(BF16), b.astype(BF16), preferred_element_type=F32)


def _dot_nt(a, b):
    return lax.dot_general(a.astype(BF16), b.astype(BF16), (((1,), (1,)), ((), ())), preferred_element_type=F32)


def _rms_fwd(x, g):
    r = lax.rsqrt(jnp.mean(x * x, axis=-1, keepdims=True) + EPS)
    xh = x * r
    return r, xh, xh * g


def _rms_bwd(dh, xh, r, g):
    dxh = dh * g
    return r * (dxh - xh * jnp.mean(dxh * xh, axis=-1, keepdims=True))


def _ln_stats(x):
    mu = jnp.mean(x, axis=-1, keepdims=True)
    xc = x - mu
    rs = lax.rsqrt(jnp.mean(xc * xc, axis=-1, keepdims=True) + EPS)
    return xc * rs, rs


def _ln_bwd(dxh, xh, rs):
    return rs * (dxh - jnp.mean(dxh, axis=-1, keepdims=True) - xh * jnp.mean(dxh * xh, axis=-1, keepdims=True))


def _group_ln_fwd(c):
    parts, rss = [], []
    for j in range(c.shape[1] // CHUNK):
        xh, rs = _ln_stats(c[:, j * CHUNK:(j + 1) * CHUNK])
        parts.append(xh)
        rss.append(rs)
    return jnp.concatenate(parts, axis=1), rss


def _group_ln_bwd(dxh, xh, rss):
    parts = []
    for j, rs in enumerate(rss):
        cols = slice(j * CHUNK, (j + 1) * CHUNK)
        parts.append(_ln_bwd(dxh[:, cols], xh[:, cols], rs))
    return jnp.concatenate(parts, axis=1)


def _rows_from(ext, off, tm):
    if off == 0:
        return ext[:tm]
    return pltpu.roll(ext, ext.shape[0] - off, 0)[:tm]


def _tril_mask():
    t = lax.broadcasted_iota(jnp.int32, (CHUNK, CHUNK), 0)
    s = lax.broadcasted_iota(jnp.int32, (CHUNK, CHUNK), 1)
    return t >= s


def _mix_forward(ua, va, vb, gb, g_halo, ws_ref, bfull_ref, lng, lnb, cw_ref, cb, cg, cbeta, mixed_scr):
    tm, da = ua.shape
    heads = da // CHUNK
    u, du_fac = _gelu_and_grad(ua)
    vg, dvg_fac = _gelu_and_grad(va)
    vh, v_rs = _ln_stats(vg)
    v = vh * lng + lnb
    mask = _tril_mask()
    wm = [jnp.where(mask, ws_ref[h], 0.0).astype(BF16) for h in range(heads)]
    vb16 = v.astype(BF16)
    for ci in range(tm // CHUNK):
        rows = slice(ci * CHUNK, (ci + 1) * CHUNK)
        for h in range(heads):
            cols = slice(h * CHUNK, (h + 1) * CHUNK)
            mixed_scr[rows, cols] = jnp.dot(wm[h], vb16[rows, cols], preferred_element_type=F32) + bfull_ref[h]
    mixed = mixed_scr[...]
    a = u * mixed

    sg = _sigmoid(gb)
    g = vb * sg
    ext = jnp.concatenate([g_halo, g], axis=0)
    cpre = jnp.zeros_like(g) + cb
    for k in range(CONV_TAPS):
        cpre = cpre + _rows_from(ext, HALO - (CONV_TAPS - 1) + k, tm) * cw_ref[k:k + 1, :]
    chh, c_rss = _group_ln_fwd(cpre)
    cn = chh * cg + cbeta
    sc = _sigmoid(cn)
    cout = cn * sc
    return dict(u=u, du_fac=du_fac, dvg_fac=dvg_fac, vh=vh, v_rs=v_rs, v16=vb16, mixed=mixed, a=a, sg=sg, g=g,
                ext=ext, chh=chh, c_rss=c_rss, cn=cn, sc=sc, cout=cout)


def _mix_fwd_call(l, x, g1, win_g, lng, lnb, ws, bfull, cw, cb, cg, cbeta, wout_g):
    T, D = x.shape
    tm = min(TM_MIX, T)
    n_t = T // tm
    da = win_g.shape[-1]
    wo_rows = wout_g.shape[1]

    def body(x_ref, g1_ref, win_ref, lng_ref, lnb_ref, ws_ref, bf_ref, cw_ref, cb_ref, cg_ref, cbeta_ref, wout_ref,
             x1_ref, proj_ref, halo_scr, mixed_scr):
        i = pl.program_id(0)

        @pl.when(i == 0)
        def _():
            halo_scr[...] = jnp.zeros_like(halo_scr)

        x_t = x_ref[...]
        _, _, h1 = _rms_fwd(x_t, g1_ref[...])
        h1b = h1.astype(BF16)
        ps = []
        for p in range(N_CHIPS):
            pp = jnp.dot(h1b, win_ref[p], preferred_element_type=F32)
            proj_ref[:, p * da:(p + 1) * da] = pp
            ps.append(pp)
        f = _mix_forward(ps[0], ps[1], ps[2], ps[3], halo_scr[...], ws_ref, bf_ref, lng_ref[...], lnb_ref[...],
                         cw_ref, cb_ref[...], cg_ref[...], cbeta_ref[...], mixed_scr)
        halo_scr[...] = f["g"][tm - HALO:, :]
        mix = jnp.concatenate([f["a"], f["cout"]], axis=1).astype(BF16)
        acc = x_t
        for q in range(N_CHIPS):
            acc = acc + jnp.dot(mix[:, q * wo_rows:(q + 1) * wo_rows], wout_ref[q], preferred_element_type=F32)
        x1_ref[...] = acc

    full = lambda a: pl.BlockSpec(a.shape, lambda i: (0,) * a.ndim)
    return pl.pallas_call(
        body, name=f"mix_fwd_l{l}",
        grid=(n_t,),
        in_specs=[
            pl.BlockSpec((tm, D), lambda i: (i, 0)),
            full(g1),
            pl.BlockSpec((N_CHIPS, D, da), lambda i: (0, 0, 0)),
            full(lng), full(lnb), full(ws), full(bfull), full(cw), full(cb), full(cg), full(cbeta),
            pl.BlockSpec((N_CHIPS, wo_rows, D), lambda i: (0, 0, 0)),
        ],
        out_specs=[
            pl.BlockSpec((tm, D), lambda i: (i, 0)),
            pl.BlockSpec((tm, N_CHIPS * da), lambda i: (i, 0)),
        ],
        out_shape=[jax.ShapeDtypeStruct((T, D), F32), jax.ShapeDtypeStruct((T, N_CHIPS * da), F32)],
        scratch_shapes=[pltpu.VMEM((HALO, da), F32), pltpu.VMEM((tm, da), F32)],
        compiler_params=_params(("arbitrary",)),
    )(x, g1, win_g, lng, lnb, ws, bfull, cw, cb, cg, cbeta, wout_g)


def _ffn_tile(T):
    return min(TM_FFN, max(T // 2, CHUNK))


def _ffn_fwd_call(l, x1, g2, w1_g, w2_g):
    T, D = x1.shape
    tm = _ffn_tile(T)
    n_t = T // tm
    ffb = w1_g.shape[-1]

    def body(x1_ref, g2_ref, w1_ref, w2_ref, x2_ref, z_ref, h2_scr, acc_scr):
        p = pl.program_id(1)

        @pl.when(p == 0)
        def _():
            _, _, h2 = _rms_fwd(x1_ref[...], g2_ref[...])
            h2_scr[...] = h2.astype(BF16)
            acc_scr[...] = jnp.zeros_like(acc_scr)

        z = jnp.dot(h2_scr[...], w1_ref[...], preferred_element_type=F32)
        z_ref[...] = z.astype(BF16)
        rz = jnp.maximum(z, 0.0)
        acc_scr[...] += jnp.dot((rz * rz).astype(BF16), w2_ref[...], preferred_element_type=F32)

        @pl.when(p == N_CHIPS - 1)
        def _():
            x2_ref[...] = x1_ref[...] + acc_scr[...]

    return pl.pallas_call(
        body, name=f"ffn_fwd_l{l}",
        grid=(n_t, N_CHIPS),
        in_specs=[
            pl.BlockSpec((tm, D), lambda i, p: (i, 0)),
            pl.BlockSpec(g2.shape, lambda i, p: (0, 0)),
            pl.BlockSpec((None, D, ffb), lambda i, p: (p, 0, 0)),
            pl.BlockSpec((None, ffb, D), lambda i, p: (p, 0, 0)),
        ],
        out_specs=[
            pl.BlockSpec((tm, D), lambda i, p: (i, 0)),
            pl.BlockSpec((tm, ffb), lambda i, p: (i, p)),
        ],
        out_shape=[jax.ShapeDtypeStruct((T, D), F32), jax.ShapeDtypeStruct((T, N_CHIPS * ffb), BF16)],
        scratch_shapes=[pltpu.VMEM((tm, D), BF16), pltpu.VMEM((tm, D), F32)],
        compiler_params=_params(("arbitrary", "arbitrary")),
    )(x1, g2, w1_g, w2_g)


def _loss_call(x, gf, target):
    T, D = x.shape
    tm = min(TM_FFN, T)
    n_t = T // tm

    def body(x_ref, gf_ref, t_ref, dx_ref, dxb_ref, loss_ref, dgf_ref):
        i = pl.program_id(0)

        @pl.when(i == 0)
        def _():
            loss_ref[...] = jnp.zeros_like(loss_ref)
            dgf_ref[...] = jnp.zeros_like(dgf_ref)

        g = gf_ref[...]
        r, xh, y = _rms_fwd(x_ref[...], g)
        e = y - t_ref[...]
        per_tok = jnp.sum(e * e, axis=-1, keepdims=True) * (1.0 / D)
        loss_ref[...] += 0.5 * jnp.sum(per_tok, axis=0, keepdims=True)
        dy = e * (1.0 / D)
        dgf_ref[...] += jnp.sum(dy * xh, axis=0, keepdims=True)
        dx = _rms_bwd(dy, xh, r, g)
        dx_ref[...] = dx
        dxb_ref[...] = dx.astype(BF16)

    return pl.pallas_call(
        body, name="loss_head",
        grid=(n_t,),
        in_specs=[
            pl.BlockSpec((tm, D), lambda i: (i, 0)),
            pl.BlockSpec(gf.shape, lambda i: (0, 0)),
            pl.BlockSpec((tm, D), lambda i: (i, 0)),
        ],
        out_specs=[
            pl.BlockSpec((tm, D), lambda i: (i, 0)),
            pl.BlockSpec((tm, D), lambda i: (i, 0)),
            pl.BlockSpec((1, 1), lambda i: (0, 0)),
            pl.BlockSpec((1, D), lambda i: (0, 0)),
        ],
        out_shape=[jax.ShapeDtypeStruct((T, D), F32), jax.ShapeDtypeStruct((T, D), BF16),
                   jax.ShapeDtypeStruct((1, 1), F32), jax.ShapeDtypeStruct((1, D), F32)],
        compiler_params=_params(("arbitrary",)),
    )(x, gf, target)


def _ffn_bwd_call(l, dx2, dx2b, x1, z, g2, w1_g, w2_g):
    T, D = x1.shape
    tm = _ffn_tile(T)
    n_t = T // tm
    ffb = w1_g.shape[-1]

    def body(dx2_ref, dx2b_ref, x1_ref, z_ref, g2_ref, w1_ref, w2_ref,
             dx1_ref, dx1b_ref, dz_ref, ft_ref, h2t_ref, dg2_ref, acc_scr):
        i = pl.program_id(0)
        p = pl.program_id(1)

        @pl.when(jnp.logical_and(i == 0, p == 0))
        def _():
            dg2_ref[...] = jnp.zeros_like(dg2_ref)

        @pl.when(p == 0)
        def _():
            _, _, h2 = _rms_fwd(x1_ref[...], g2_ref[...])
            h2t_ref[...] = h2.T.astype(BF16)
            acc_scr[...] = jnp.zeros_like(acc_scr)

        rz = jnp.maximum(z_ref[...].astype(F32), 0.0)
        ft_ref[...] = (rz * rz).T.astype(BF16)
        df = _dot_nt(dx2b_ref[...], w2_ref[...])
        dz = (df * (2.0 * rz)).astype(BF16)
        dz_ref[...] = dz
        acc_scr[...] += _dot_nt(dz, w1_ref[...])

        @pl.when(p == N_CHIPS - 1)
        def _():
            g = g2_ref[...]
            r, xh, _ = _rms_fwd(x1_ref[...], g)
            dh2 = acc_scr[...]
            dg2_ref[...] += jnp.sum(dh2 * xh, axis=0, keepdims=True)
            dx1 = dx2_ref[...] + _rms_bwd(dh2, xh, r, g)
            dx1_ref[...] = dx1
            dx1b_ref[...] = dx1.astype(BF16)

    return pl.pallas_call(
        body, name=f"ffn_bwd_l{l}",
        grid=(n_t, N_CHIPS),
        in_specs=[
            pl.BlockSpec((tm, D), lambda i, p: (i, 0)),
            pl.BlockSpec((tm, D), lambda i, p: (i, 0)),
            pl.BlockSpec((tm, D), lambda i, p: (i, 0)),
            pl.BlockSpec((tm, ffb), lambda i, p: (i, p)),
            pl.BlockSpec(g2.shape, lambda i, p: (0, 0)),
            pl.BlockSpec((None, D, ffb), lambda i, p: (p, 0, 0)),
            pl.BlockSpec((None, ffb, D), lambda i, p: (p, 0, 0)),
        ],
        out_specs=[
            pl.BlockSpec((tm, D), lambda i, p: (i, 0)),
            pl.BlockSpec((tm, D), lambda i, p: (i, 0)),
            pl.BlockSpec((tm, ffb), lambda i, p: (i, p)),
            pl.BlockSpec((ffb, tm), lambda i, p: (p, i)),
            pl.BlockSpec((D, tm), lambda i, p: (0, i)),
            pl.BlockSpec((1, D), lambda i, p: (0, 0)),
        ],
        out_shape=[jax.ShapeDtypeStruct((T, D), F32), jax.ShapeDtypeStruct((T, D), BF16),
                   jax.ShapeDtypeStruct((T, N_CHIPS * ffb), BF16), jax.ShapeDtypeStruct((N_CHIPS * ffb, T), BF16),
                   jax.ShapeDtypeStruct((D, T), BF16), jax.ShapeDtypeStruct((1, D), F32)],
        scratch_shapes=[pltpu.VMEM((tm, D), F32)],
        compiler_params=_params(("arbitrary", "arbitrary")),
    )(dx2, dx2b, x1, z, g2, w1_g, w2_g)


def _mix_bwd_call(l, dx1, dx1b, x, proj, g1, win_g, lng, lnb, ws, wst, bfull, cw, cb, cg, cbeta, wout_g):
    T, D = x.shape
    tm = min(TM_MIX, T)
    n_t = T // tm
    da = win_g.shape[-1]
    heads = da // CHUNK
    wo_rows = wout_g.shape[1]
    halo_blocks = tm // HALO

    def body(dx1_ref, dx1b_ref, x_ref, proj_ref, vbh_ref, gbh_ref, g1_ref, win_ref, lng_ref, lnb_ref, ws_ref, wst_ref,
             bf_ref, cw_ref, cb_ref, cg_ref, cbeta_ref, wout_ref,
             dx_ref, dxb_ref, dproj_ref, h1t_ref, mixt_ref,
             dg1_ref, dlng_ref, dlnb_ref, dws_ref, dbs_ref, dcw_ref, dcb_ref, dcg_ref, dcbeta_ref,
             carry_scr, mixed_scr, dv_scr):
        i = pl.program_id(0)
        tile = n_t - 1 - i

        @pl.when(i == 0)
        def _():
            carry_scr[...] = jnp.zeros_like(carry_scr)
            for ref in (dg1_ref, dlng_ref, dlnb_ref, dws_ref, dbs_ref, dcw_ref, dcb_ref, dcg_ref, dcbeta_ref):
                ref[...] = jnp.zeros_like(ref)

        g1v = g1_ref[...]
        r, xh, h1 = _rms_fwd(x_ref[...], g1v)
        h1t_ref[...] = h1.T.astype(BF16)

        ua = proj_ref[:, 0 * da:1 * da]
        va = proj_ref[:, 1 * da:2 * da]
        vb = proj_ref[:, 2 * da:3 * da]
        gb = proj_ref[:, 3 * da:4 * da]
        g_halo = jnp.where(tile > 0, vbh_ref[...] * _sigmoid(gbh_ref[...]), 0.0)
        lng_v, cg_v = lng_ref[...], cg_ref[...]
        f = _mix_forward(ua, va, vb, gb, g_halo, ws_ref, bf_ref, lng_v, lnb_ref[...], cw_ref, cb_ref[...], cg_v,
                         cbeta_ref[...], mixed_scr)
        mixt_ref[0:da, :] = f["a"].T.astype(BF16)
        mixt_ref[da:2 * da, :] = f["cout"].T.astype(BF16)

        dxo = dx1b_ref[...]
        dmix = jnp.concatenate([_dot_nt(dxo, wout_ref[q]) for q in range(N_CHIPS)], axis=1)
        da_ = dmix[:, :da]
        dc_ = dmix[:, da:]

        dua = da_ * f["mixed"] * f["du_fac"]
        dmixed = (da_ * f["u"]).astype(BF16)
        mask_t = (lax.broadcasted_iota(jnp.int32, (CHUNK, CHUNK), 1)
                  >= lax.broadcasted_iota(jnp.int32, (CHUNK, CHUNK), 0))
        wmt =[jnp.where(mask_t, wst_ref[h], 0.0).astype(BF16) for h in range(heads)]
        mask = _tril_mask()
        v16 = f["v16"]
        for h in range(heads):
            cols = slice(h * CHUNK, (h + 1) * CHUNK)
            dws_h = jnp.zeros((CHUNK, CHUNK), F32)
            dbs_h = jnp.zeros((CHUNK, CHUNK), F32)
            for ci in range(tm // CHUNK):
                rows = slice(ci * CHUNK, (ci + 1) * CHUNK)
                dm = dmixed[rows, cols]
                dv_scr[rows, cols] = jnp.dot(wmt[h], dm, preferred_element_type=F32)
                dws_h = dws_h + _dot_nt(dm, v16[rows, cols])
                dbs_h = dbs_h + dm.astype(F32)
            dws_ref[h] += jnp.where(mask, dws_h, 0.0)
            dbs_ref[h] += jnp.broadcast_to(jnp.sum(dbs_h, axis=1, keepdims=True), (CHUNK, CHUNK))
        dv = dv_scr[...]
        dlng_ref[...] += jnp.sum(dv * f["vh"], axis=0, keepdims=True)
        dlnb_ref[...] += jnp.sum(dv, axis=0, keepdims=True)
        dva = _ln_bwd(dv * lng_v, f["vh"], f["v_rs"]) * f["dvg_fac"]

        cn, sc = f["cn"], f["sc"]
        dcn = dc_ * (sc * (1.0 + cn * (1.0 - sc)))
        dcg_ref[...] += jnp.sum(dcn * f["chh"], axis=0, keepdims=True)
        dcbeta_ref[...] += jnp.sum(dcn, axis=0, keepdims=True)
        dcpre = _group_ln_bwd(dcn * cg_v, f["chh"], f["c_rss"])
        dcb_ref[...] += jnp.sum(dcpre, axis=0, keepdims=True)
        ext = f["ext"]
        dext = jnp.concatenate([dcpre, carry_scr[...]], axis=0)
        dg = jnp.zeros_like(dcpre)
        for k in range(CONV_TAPS):
            g_k = _rows_from(ext, HALO - (CONV_TAPS - 1) + k, tm)
            dcw_ref[k:k + 1, :] += jnp.sum(dcpre * g_k, axis=0, keepdims=True)
            dg = dg + _rows_from(dext, CONV_TAPS - 1 - k, tm) * cw_ref[k:k + 1, :]
        carry_scr[...] = dcpre[:HALO, :]
        sg = f["sg"]
        dvb = dg * sg
        dgb = dg * vb * sg * (1.0 - sg)

        dps = [dua.astype(BF16), dva.astype(BF16), dvb.astype(BF16), dgb.astype(BF16)]
        dh1 = jnp.zeros((tm, D), F32)
        for p in range(N_CHIPS):
            dproj_ref[:, p * da:(p + 1) * da] = dps[p]
            dh1 = dh1 + _dot_nt(dps[p], win_ref[p])
        dg1_ref[...] += jnp.sum(dh1 * xh, axis=0, keepdims=True)
        dx = dx1_ref[...] + _rms_bwd(dh1, xh, r, g1v)
        dx_ref[...] = dx
        dxb_ref[...] = dx.astype(BF16)

    rev = lambda i: (n_t - 1 - i, 0)
    full = lambda a: pl.BlockSpec(a.shape, lambda i: (0,) * a.ndim)
    acc = lambda shape: pl.BlockSpec(shape, lambda i: (0,) * len(shape))
    halo_idx = lambda col: (lambda i: (jnp.maximum((n_t - 1 - i) * halo_blocks - 1, 0), col))
    small_shapes = [(1, D), (1, da), (1, da), (heads, CHUNK, CHUNK), (heads, CHUNK, CHUNK), (HALO, da),
                    (1, da), (1, da), (1, da)]
    return pl.pallas_call(
        body, name=f"mix_bwd_l{l}",
        grid=(n_t,),
        in_specs=[
            pl.BlockSpec((tm, D), rev),
            pl.BlockSpec((tm, D), rev),
            pl.BlockSpec((tm, D), rev),
            pl.BlockSpec((tm, N_CHIPS * da), rev),
            pl.BlockSpec((HALO, da), halo_idx(2)),
            pl.BlockSpec((HALO, da), halo_idx(3)),
            full(g1),
            pl.BlockSpec((N_CHIPS, D, da), lambda i: (0, 0, 0)),
            full(lng), full(lnb), full(ws), full(wst), full(bfull), full(cw), full(cb), full(cg), full(cbeta),
            pl.BlockSpec((N_CHIPS, wo_rows, D), lambda i: (0, 0, 0)),
        ],
        out_specs=[
            pl.BlockSpec((tm, D), rev),
            pl.BlockSpec((tm, D), rev),
            pl.BlockSpec((tm, N_CHIPS * da), rev),
            pl.BlockSpec((D, tm), lambda i: (0, n_t - 1 - i)),
            pl.BlockSpec((2 * da, tm), lambda i: (0, n_t - 1 - i)),
        ] + [acc(s) for s in small_shapes],
        out_shape=[jax.ShapeDtypeStruct((T, D), F32), jax.ShapeDtypeStruct((T, D), BF16),
                   jax.ShapeDtypeStruct((T, N_CHIPS * da), BF16), jax.ShapeDtypeStruct((D, T), BF16),
                   jax.ShapeDtypeStruct((2 * da, T), BF16)] + [jax.ShapeDtypeStruct(s, F32) for s in small_shapes],
        scratch_shapes=[pltpu.VMEM((HALO, da), F32), pltpu.VMEM((tm, da), F32), pltpu.VMEM((tm, da), F32)],
        compiler_params=_params(("arbitrary",)),
    )(dx1, dx1b, x, proj, proj, proj, g1, win_g, lng, lnb, ws, wst, bfull, cw, cb, cg, cbeta, wout_g)


def _wgrad_call(name, at, b, split_cols, steps):
    M, T = at.shape
    N = b.shape[1]
    if split_cols:
        bn = N // (N_CHIPS * steps)
        in_specs = [pl.BlockSpec((M, T), lambda j: (0, 0)), pl.BlockSpec((T, bn), lambda j: (0, j))]
        out_spec = pl.BlockSpec((None, M, bn), lambda j: (j // steps, 0, j % steps))
        out_shape = jax.ShapeDtypeStruct((N_CHIPS, M, N // N_CHIPS), BF16)
    else:
        bm = M // (N_CHIPS * steps)
        in_specs = [pl.BlockSpec((bm, T), lambda j: (j, 0)), pl.BlockSpec((T, N), lambda j: (0, 0))]
        out_spec = pl.BlockSpec((None, bm, N), lambda j: (j // steps, j % steps, 0))
        out_shape = jax.ShapeDtypeStruct((N_CHIPS, M // N_CHIPS, N), BF16)

    def body(at_ref, b_ref, o_ref):
        o_ref[...] = jnp.dot(at_ref[...], b_ref[...], preferred_element_type=F32).astype(BF16)

    return pl.pallas_call(
        body, name=name, grid=(N_CHIPS * steps,), in_specs=in_specs, out_specs=out_spec, out_shape=out_shape,
        compiler_params=_params(("arbitrary",)),
    )(at, b)


def _rows_block(rows, cols):
    br = rows
    while br * cols * 4 > 2 ** 20 and br % 16 == 0:
        br //= 2
    return br


def _cast_own_call(name, chip, w, l):
    _, rows, cols = w.shape
    br = _rows_block(rows, cols)
    n_b = rows // br

    def body(chip_ref, w_ref, o_ref):
        o_ref[...] = w_ref[...].astype(BF16)

    return pl.pallas_call(
        body, name=name,
        grid_spec=pltpu.PrefetchScalarGridSpec(
            num_scalar_prefetch=1, grid=(n_b,),
            in_specs=[pl.BlockSpec((None, br, cols), lambda i, chip_ref: (l, i, 0))],
            out_specs=pl.BlockSpec((None, br, cols), lambda i, chip_ref: (chip_ref[0], i, 0))),
        out_shape=jax.ShapeDtypeStruct((N_CHIPS, rows, cols), BF16),
        compiler_params=_params(("arbitrary",)),
    )(chip, w)


def _pair_sum_call(name, core, g, got):
    _, rh, cols = got.shape
    br = _rows_block(rh, cols)
    n_b = rh // br

    def body(core_ref, g_ref, got_ref, o_ref):
        o_ref[...] = (g_ref[...].astype(F32) + got_ref[...].astype(F32)).astype(BF16)

    half = pl.BlockSpec((None, br, cols), lambda q, i, core_ref: (q, i, 0))
    return pl.pallas_call(
        body, name=name,
        grid_spec=pltpu.PrefetchScalarGridSpec(
            num_scalar_prefetch=1, grid=(N_CHIPS, n_b),
            in_specs=[pl.BlockSpec((None, br, cols), lambda q, i, core_ref: (q, core_ref[0] * n_b + i, 0)), half],
            out_specs=half),
        out_shape=jax.ShapeDtypeStruct(got.shape, BF16),
        compiler_params=_params(("arbitrary", "arbitrary")),
    )(core, g, got)


def _chip_sum_call(name, core, parts):
    _, rh, cols = parts.shape
    br = _rows_block(rh, cols)
    n_b = rh // br

    def body(core_ref, *refs):
        o_ref = refs[N_CHIPS]
        total = refs[0][...].astype(F32)
        for q in range(1, N_CHIPS):
            total = total + refs[q][...].astype(F32)
        o_ref[...] = total

    return pl.pallas_call(
        body, name=name,
        grid_spec=pltpu.PrefetchScalarGridSpec(
            num_scalar_prefetch=1, grid=(n_b,),
            in_specs=[pl.BlockSpec((None, br, cols), functools.partial(lambda i, core_ref, q: (q, i, 0), q=q))
                      for q in range(N_CHIPS)],
            out_specs=pl.BlockSpec((br, cols), lambda i, core_ref: (core_ref[0] * n_b + i, 0))),
        out_shape=jax.ShapeDtypeStruct((2 * rh, cols), F32),
        compiler_params=_params(("arbitrary",)),
    )(core, *([parts] * N_CHIPS))


def _adamw_layers_call(name, w, g0, g1, m, v):
    _, rows, cols = w.shape
    br = _rows_block(rows, cols)

    def body(w_ref, g0_ref, g1_ref, m_ref, v_ref, g_ref, d_ref, nm_ref, nv_ref):
        gv = jnp.where(pl.program_id(0) == 0, g0_ref[...], g1_ref[...])
        g_ref[...] = gv
        m_new = ADAM_B1 * m_ref[...] + (1.0 - ADAM_B1) * gv
        v_new = ADAM_B2 * v_ref[...] + (1.0 - ADAM_B2) * (gv * gv)
        m_hat = m_new / (1.0 - ADAM_B1 ** ADAM_STEP)
        v_hat = v_new / (1.0 - ADAM_B2 ** ADAM_STEP)
        d_ref[...] = -ADAM_LR * (m_hat / (jnp.sqrt(v_hat) + ADAM_EPS) + ADAM_WD * w_ref[...])
        nm_ref[...] = m_new
        nv_ref[...] = v_new

    both = pl.BlockSpec((None, br, cols), lambda l, i: (l, i, 0))
    one = pl.BlockSpec((br, cols), lambda l, i: (i, 0))
    return pl.pallas_call(
        body, name=name, grid=(2, rows // br), in_specs=[both, one, one, both, both], out_specs=[both] * 4,
        out_shape=[jax.ShapeDtypeStruct(w.shape, F32)] * 4,
        compiler_params=_params(("arbitrary", "arbitrary")),
    )(w, g0, g1, m, v)


def _adamw_call(name, w, g, m, v):
    rows, cols = w.shape
    br = _rows_block(rows, cols)

    def body(w_ref, g_ref, m_ref, v_ref, d_ref, nm_ref, nv_ref):
        gv = g_ref[...]
        m_new = ADAM_B1 * m_ref[...] + (1.0 - ADAM_B1) * gv
        v_new = ADAM_B2 * v_ref[...] + (1.0 - ADAM_B2) * (gv * gv)
        m_hat = m_new / (1.0 - ADAM_B1 ** ADAM_STEP)
        v_hat = v_new / (1.0 - ADAM_B2 ** ADAM_STEP)
        d_ref[...] = -ADAM_LR * (m_hat / (jnp.sqrt(v_hat) + ADAM_EPS) + ADAM_WD * w_ref[...])
        nm_ref[...] = m_new
        nv_ref[...] = v_new

    spec = pl.BlockSpec((br, cols), lambda i: (i, 0))
    return pl.pallas_call(
        body, name=name, grid=(rows // br,), in_specs=[spec] * 4, out_specs=[spec] * 3,
        out_shape=[jax.ShapeDtypeStruct((rows, cols), F32)] * 3,
        compiler_params=_params(("arbitrary",)),
    )(w, g, m, v)


ANY = pl.BlockSpec(memory_space=pl.ANY)


def _place():
    x, y, c = lax.axis_index("x"), lax.axis_index("y"), lax.axis_index("c")
    chips = [(1 - x, y), (x, 1 - y), (1 - x, 1 - y)]
    return x, y, c, 2 * x + y, chips


def _half_rows(ref, core):
    rh = ref.shape[-2] // 2
    rows = pl.ds(pl.multiple_of(core * rh, rh), rh)
    return ref.at[rows] if len(ref.shape) == 2 else ref.at[:, rows]


def _gather_layer_call(l, bufs, cw_shard=None):
    n = len(bufs)
    with_cw = cw_shard is not None

    def body(*refs):
        ins = refs[:n]
        refs = refs[n:]
        if with_cw:
            cw_in, refs = refs[0], refs[1:]
        outs = refs[:n]
        refs = refs[n:]
        if with_cw:
            cw_out, refs = refs[0], refs[1:]
        send_sems, recv_sems, cw_send, cw_recv, cw_local = refs
        x, y, c, p, chips = _place()
        sibling = (x, y, 1 - c)
        qs = [2 * cx + cy for cx, cy in chips]

        def remote(k, j, src, dst, to):
            return pltpu.make_async_remote_copy(src_ref=src, dst_ref=dst, send_sem=send_sems.at[k, j],
                                                recv_sem=recv_sems.at[k, j], device_id=to, device_id_type=MESH)

        def cw_copy(j, dst, to):
            return pltpu.make_async_remote_copy(src_ref=cw_in, dst_ref=dst, send_sem=cw_send.at[j],
                                                recv_sem=cw_recv.at[j], device_id=to, device_id_type=MESH)

        sent = []
        for j, chip in enumerate(chips):
            for k in range(n):
                cp = remote(k, j, _half_rows(ins[k].at[p], c), _half_rows(outs[k].at[p], c), (*chip, c))
                cp.start()
                sent.append(cp)
        if with_cw:
            cw_own = pltpu.make_async_copy(cw_in, cw_out.at[p], cw_local)
            cw_own.start()
            for j, chip in enumerate(chips):
                cp = cw_copy(j, cw_out.at[p], (*chip, c))
                cp.start()
                sent.append(cp)
        for j in range(len(chips)):
            for k in range(n):
                landed = _half_rows(outs[k].at[qs[j]], c)
                remote(k, j, landed, landed, sibling).wait_recv()
                cp = remote(k, len(chips) + j, landed, landed, sibling)
                cp.start()
                sent.append(cp)
        for j in range(len(chips)):
            for k in range(n):
                other = _half_rows(outs[k].at[qs[j]], 1 - c)
                remote(k, len(chips) + j, other, other, sibling).wait_recv()
        if with_cw:
            for j in range(len(chips)):
                cw_copy(j, cw_out.at[qs[j]], sibling).wait_recv()
            cw_own.wait()
        for cp in sent:
            cp.wait_send()

    operands = list(bufs) + ([cw_shard] if with_cw else [])
    out_shape = [jax.ShapeDtypeStruct(b.shape, b.dtype) for b in bufs]
    if with_cw:
        out_shape.append(jax.ShapeDtypeStruct((N_CHIPS,) + cw_shard.shape, cw_shard.dtype))
    return pl.pallas_call(
        body, name=f"gather_weights_l{l}",
        in_specs=[ANY] * len(operands), out_specs=[ANY] * len(out_shape), out_shape=out_shape,
        input_output_aliases={k: k for k in range(n)},
        scratch_shapes=[pltpu.SemaphoreType.DMA((n, 6)), pltpu.SemaphoreType.DMA((n, 6)),
                        pltpu.SemaphoreType.DMA((3,)), pltpu.SemaphoreType.DMA((3,)), pltpu.SemaphoreType.DMA],
    )(*operands)


def _pair_exchange_call(l, gs):
    n = len(gs)

    def body(*refs):
        g, got = refs[:n], refs[n:2 * n]
        send_sems, recv_sems = refs[2 * n:]
        x, y, c, _, _ = _place()
        sibling = (x, y, 1 - c)
        copies = [pltpu.make_async_remote_copy(src_ref=_half_rows(g[k], 1 - c), dst_ref=got[k],
                                               send_sem=send_sems.at[k], recv_sem=recv_sems.at[k],
                                               device_id=sibling, device_id_type=MESH) for k in range(n)]
        for cp in copies:
            cp.start()
        for cp in copies:
            cp.wait_send()
            cp.wait_recv()

    return pl.pallas_call(
        body, name=f"grad_pair_exchange_l{l}",
        in_specs=[ANY] * n, out_specs=[ANY] * n,
        out_shape=[jax.ShapeDtypeStruct((g.shape[0], g.shape[1] // 2, g.shape[2]), g.dtype) for g in gs],
        scratch_shapes=[pltpu.SemaphoreType.DMA((n,)), pltpu.SemaphoreType.DMA((n,))],
    )(*gs)


def _chip_exchange_call(l, sums):
    n = len(sums)

    def body(*refs):
        ins, outs = refs[:n], refs[n:2 * n]
        send_sems, recv_sems, local_sems = refs[2 * n:]
        x, y, c, p, chips = _place()
        qs = [2 * cx + cy for cx, cy in chips]

        def remote(k, j, src, dst, to):
            return pltpu.make_async_remote_copy(src_ref=src, dst_ref=dst, send_sem=send_sems.at[k, j],
                                                recv_sem=recv_sems.at[k, j], device_id=to, device_id_type=MESH)

        own = [pltpu.make_async_copy(ins[k].at[p], outs[k].at[p], local_sems.at[k]) for k in range(n)]
        for cp in own:
            cp.start()
        sent = []
        for j, chip in enumerate(chips):
            for k in range(n):
                cp = remote(k, j, ins[k].at[qs[j]], outs[k].at[p], (*chip, c))
                cp.start()
                sent.append(cp)
        for j in range(len(chips)):
            for k in range(n):
                remote(k, j, ins[k].at[qs[j]], outs[k].at[qs[j]], (x, y, c)).wait_recv()
        for cp in sent:
            cp.wait_send()
        for cp in own:
            cp.wait()

    return pl.pallas_call(
        body, name=f"grad_chip_exchange_l{l}",
        in_specs=[ANY] * n, out_specs=[ANY] * n,
        out_shape=[jax.ShapeDtypeStruct(s.shape, s.dtype) for s in sums],
        scratch_shapes=[pltpu.SemaphoreType.DMA((n, 3)), pltpu.SemaphoreType.DMA((n, 3)),
                        pltpu.SemaphoreType.DMA((n,))],
    )(*sums)


def _pair_gather_call(l, halves):
    n = len(halves)

    def body(*refs):
        ins, outs = refs[:n], refs[n:2 * n]
        send_sems, recv_sems = refs[2 * n:]
        x, y, c, _, _ = _place()
        sibling = (x, y, 1 - c)

        def remote(k, src, dst):
            return pltpu.make_async_remote_copy(src_ref=src, dst_ref=dst, send_sem=send_sems.at[k],
                                                recv_sem=recv_sems.at[k], device_id=sibling, device_id_type=MESH)

        sent = [remote(k, _half_rows(ins[k], c), _half_rows(outs[k], c)) for k in range(n)]
        for cp in sent:
            cp.start()
        for k in range(n):
            other = _half_rows(outs[k], 1 - c)
            remote(k, other, other).wait_recv()
        for cp in sent:
            cp.wait_send()

    return pl.pallas_call(
        body, name=f"grad_pair_gather_l{l}",
        in_specs=[ANY] * n, out_specs=[ANY] * n,
        out_shape=[jax.ShapeDtypeStruct(h.shape, h.dtype) for h in halves],
        input_output_aliases={k: k for k in range(n)},
        scratch_shapes=[pltpu.SemaphoreType.DMA((n,)), pltpu.SemaphoreType.DMA((n,))],
    )(*halves)


def _all_sum_small_call(block):
    m_per, n = block.shape

    def body(x_ref, sum_ref, all_ref, send_sems, recv_sems, local_sem):
        x, y, c, _, chip_list = _place()
        me, sibling = (x, y, c), (x, y, 1 - c)

        def rows(px, py, pc):
            return all_ref.at[pl.ds((4 * px + 2 * py + pc) * m_per, m_per), :]

        def copy(k, blk, to, src=None):
            return pltpu.make_async_remote_copy(src_ref=rows(*blk) if src is None else src, dst_ref=rows(*blk),
                                                send_sem=send_sems.at[k], recv_sem=recv_sems.at[k],
                                                device_id=to, device_id_type=MESH)

        mine = pltpu.make_async_copy(x_ref, rows(*me), local_sem)
        mine.start()
        first = [copy(0, me, sibling, src=x_ref)]
        first += [copy(1 + j, me, (*chip, c), src=x_ref) for j, chip in enumerate(chip_list)]
        for cp in first:
            cp.start()
        passed = [copy(4 + j, (*chip, c), sibling) for j, chip in enumerate(chip_list)]
        for j, chip in enumerate(chip_list):
            copy(1 + j, (*chip, c), me).wait_recv()
            passed[j].start()
        copy(0, sibling, me).wait_recv()
        for j, chip in enumerate(chip_list):
            copy(4 + j, (*chip, 1 - c), me).wait_recv()
        for cp in first + passed:
            cp.wait_send()
        mine.wait()
        total = all_ref[0:m_per, :]
        for d in range(1, N_DEV):
            total = total + all_ref[d * m_per:(d + 1) * m_per, :]
        sum_ref[...] = total

    vmem = pl.BlockSpec(memory_space=pltpu.VMEM)
    return pl.pallas_call(
        body, name="small_all_sum",
        in_specs=[vmem], out_specs=[vmem, vmem],
        out_shape=[jax.ShapeDtypeStruct((m_per, n), F32), jax.ShapeDtypeStruct((N_DEV * m_per, n), F32)],
        scratch_shapes=[pltpu.SemaphoreType.DMA((7,)), pltpu.SemaphoreType.DMA((7,)), pltpu.SemaphoreType.DMA],
        compiler_params=pltpu.CompilerParams(vmem_limit_bytes=VMEM_LIMIT),
    )(block)[0]


SMALL_NAMES = ["norm1_g", "sgu_ln_g", "sgu_ln_b", "sgu_w", "sgu_b", "conv_b", "conv_ln_g", "conv_ln_b", "norm2_g",
               "final_g"]
BIG_NAMES = ["w_in", "w_out", "w_ff1", "w_ff2"]


def _rows128(a):
    return a.reshape(-1, LANES)


def kernel(x, norm1_g, w_in, sgu_ln_g, sgu_ln_b, sgu_w, sgu_b, conv_w, conv_b, conv_ln_g, conv_ln_b, w_out, norm2_g, w_ff1, w_ff2, final_g, loss_target, m_norm1_g, m_w_in, m_sgu_ln_g, m_sgu_ln_b, m_sgu_w, m_sgu_b, m_conv_w, m_conv_b, m_conv_ln_g, m_conv_ln_b, m_w_out, m_norm2_g, m_w_ff1, m_w_ff2, m_final_g, v_norm1_g, v_w_in, v_sgu_ln_g, v_sgu_ln_b, v_sgu_w, v_sgu_b, v_conv_w, v_conv_b, v_conv_ln_g, v_conv_ln_b, v_w_out, v_norm2_g, v_w_ff1, v_w_ff2, v_final_g):
    w = dict(norm1_g=norm1_g, w_in=w_in, sgu_ln_g=sgu_ln_g, sgu_ln_b=sgu_ln_b, sgu_w=sgu_w, sgu_b=sgu_b,
             conv_w=conv_w, conv_b=conv_b, conv_ln_g=conv_ln_g, conv_ln_b=conv_ln_b, w_out=w_out, norm2_g=norm2_g,
             w_ff1=w_ff1, w_ff2=w_ff2, final_g=final_g)
    m = dict(norm1_g=m_norm1_g, w_in=m_w_in, sgu_ln_g=m_sgu_ln_g, sgu_ln_b=m_sgu_ln_b, sgu_w=m_sgu_w, sgu_b=m_sgu_b,
             conv_w=m_conv_w, conv_b=m_conv_b, conv_ln_g=m_conv_ln_g, conv_ln_b=m_conv_ln_b, w_out=m_w_out,
             norm2_g=m_norm2_g, w_ff1=m_w_ff1, w_ff2=m_w_ff2, final_g=m_final_g)
    v = dict(norm1_g=v_norm1_g, w_in=v_w_in, sgu_ln_g=v_sgu_ln_g, sgu_ln_b=v_sgu_ln_b, sgu_w=v_sgu_w, sgu_b=v_sgu_b,
             conv_w=v_conv_w, conv_b=v_conv_b, conv_ln_g=v_conv_ln_g, conv_ln_b=v_conv_ln_b, w_out=v_w_out,
             norm2_g=v_norm2_g, w_ff1=v_w_ff1, w_ff2=v_w_ff2, final_g=v_final_g)
    depth = w_in.shape[0]
    assert depth == 2, "core c owns layer c of every gradient"
    T, D = x.shape[1], x.shape[2]
    heads = sgu_w.shape[1]
    da = heads * CHUNK
    core = lax.axis_index("c")
    chip = 2 * lax.axis_index("x") + lax.axis_index("y")

    core_arr = core.reshape(1).astype(jnp.int32)
    chip_arr = chip.reshape(1).astype(jnp.int32)
    cw_pad = jnp.pad(conv_w, ((0, 0), (0, HALO - CONV_TAPS), (0, 0)))
    gathered = []
    for l in range(depth):
        own = [_cast_own_call(f"cast_{name}_l{l}", chip_arr, w[name], l) for name in BIG_NAMES]
        res = _gather_layer_call(l, own, cw_pad if l == 0 else None)
        gathered.append(res[:len(BIG_NAMES)])
        if l == 0:
            cw_g = res[-1]
    cw_full = jnp.transpose(cw_g, (1, 2, 0, 3)).reshape(depth, HALO, da)

    ws_t = jnp.swapaxes(sgu_w, -1, -2)
    b_full = jnp.broadcast_to(sgu_b[..., None], sgu_w.shape)
    row = lambda a, l: a[l:l + 1]

    xs, projs, x1s, zs = [], [], [], []
    h = x.reshape(T, D)
    for l in range(depth):
        win_g, wout_g, w1_g, w2_g = gathered[l]
        xs.append(h)
        x1, proj = _mix_fwd_call(l, h, row(norm1_g, l), win_g, row(sgu_ln_g, l), row(sgu_ln_b, l), sgu_w[l],
                                 b_full[l], cw_full[l], row(conv_b, l), row(conv_ln_g, l), row(conv_ln_b, l), wout_g)
        h, z = _ffn_fwd_call(l, x1, row(norm2_g, l), w1_g, w2_g)
        projs.append(proj)
        x1s.append(x1)
        zs.append(z)
    dx, dxb, loss, d_final_g = _loss_call(h, final_g.reshape(1, D), loss_target.reshape(T, D))

    big = {name: [None] * depth for name in BIG_NAMES}
    reduced = {name: [None] * depth for name in BIG_NAMES}
    small = {name: [None] * depth for name in SMALL_NAMES[:-1] + ["conv_w"]}
    for l in reversed(range(depth)):
        win_g, wout_g, w1_g, w2_g = gathered[l]
        dx1, dx1b, dz, f_t, h2_t, dg2 = _ffn_bwd_call(l, dx, dxb, x1s[l], zs[l], row(norm2_g, l), w1_g, w2_g)
        big["w_ff2"][l] = _wgrad_call(f"wgrad_ff2_l{l}", f_t, dxb, False, 2)
        big["w_ff1"][l] = _wgrad_call(f"wgrad_ff1_l{l}", h2_t, dz, True, 2)
        (dx, dxb, dproj, h1_t, mix_t, dg1, dlng, dlnb, dws, dbs, dcw, dcb, dcg, dcbeta) = _mix_bwd_call(
            l, dx1, dx1b, xs[l], projs[l], row(norm1_g, l), win_g, row(sgu_ln_g, l), row(sgu_ln_b, l), sgu_w[l],
            ws_t[l], b_full[l], cw_full[l], row(conv_b, l), row(conv_ln_g, l), row(conv_ln_b, l), wout_g)
        big["w_out"][l] = _wgrad_call(f"wgrad_out_l{l}", mix_t, dx1b, False, 1)
        big["w_in"][l] = _wgrad_call(f"wgrad_in_l{l}", h1_t, dproj, True, 1)
        small["norm1_g"][l] = dg1[0]
        small["sgu_ln_g"][l] = dlng[0]
        small["sgu_ln_b"][l] = dlnb[0]
        small["sgu_w"][l] = dws
        small["sgu_b"][l] = dbs[:, :, 0]
        small["conv_w"][l] = dcw[:CONV_TAPS]
        small["conv_b"][l] = dcb[0]
        small["conv_ln_g"][l] = dcg[0]
        small["conv_ln_b"][l] = dcbeta[0]
        small["norm2_g"][l] = dg2[0]

        got = _pair_exchange_call(l, [big[name][l] for name in BIG_NAMES])
        sums = [_pair_sum_call(f"pair_sum_{name}_l{l}", core_arr, big[name][l], recv)
                for name, recv in zip(BIG_NAMES, got)]
        parts = _chip_exchange_call(l, sums)
        halves = [_chip_sum_call(f"chip_sum_{name}_l{l}", core_arr, part) for name, part in zip(BIG_NAMES, parts)]
        for name, full in zip(BIG_NAMES, _pair_gather_call(l, halves)):
            reduced[name][l] = full
    grad_x = dx.reshape(x.shape)

    small_local = {name: jnp.stack(small[name]) for name in small}
    small_local["final_g"] = d_final_g[0]
    pieces = [_rows128(small_local[name]) for name in SMALL_NAMES]
    pieces.append(_rows128(small_local["conv_w"]))
    pieces.append(jnp.broadcast_to(loss, (8, LANES)))
    offsets = [0]
    for piece in pieces:
        offsets.append(offsets[-1] + piece.shape[0])
    summed = _all_sum_small_call(jnp.concatenate(pieces, axis=0))
    n_small = offsets[len(SMALL_NAMES)]
    loss_out = summed[offsets[-2], 0]
    small_grads = {name: summed[offsets[k]:offsets[k + 1]].reshape(w[name].shape)
                   for k, name in enumerate(SMALL_NAMES)}
    conv_w_full = summed[offsets[-3]:offsets[-2]].reshape(depth, CONV_TAPS, da)
    conv_w_grad = lax.dynamic_slice_in_dim(conv_w_full, chip * conv_w.shape[-1], conv_w.shape[-1], axis=2)

    grads, delta, new_m, new_v = {}, {}, {}, {}
    for name in BIG_NAMES:
        grads[name], delta[name], new_m[name], new_v[name] = _adamw_layers_call(
            f"adamw_{name}", w[name], reduced[name][0], reduced[name][1], m[name], v[name])
    pack = lambda src: jnp.concatenate([_rows128(src[name]) for name in SMALL_NAMES], axis=0)
    d_, m_, v_ = _adamw_call("adamw_small", pack(w), summed[:n_small], pack(m), pack(v))
    for k, name in enumerate(SMALL_NAMES):
        sl = slice(offsets[k], offsets[k + 1])
        grads[name] = small_grads[name]
        delta[name] = d_[sl].reshape(w[name].shape)
        new_m[name] = m_[sl].reshape(w[name].shape)
        new_v[name] = v_[sl].reshape(w[name].shape)
    cshape = conv_w.shape
    flat = lambda a: a.reshape(-1, cshape[-1])
    d_, m_, v_ = _adamw_call("adamw_conv_w", flat(conv_w), flat(conv_w_grad), flat(m["conv_w"]), flat(v["conv_w"]))
    grads["conv_w"] = conv_w_grad
    delta["conv_w"], new_m["conv_w"], new_v["conv_w"] = d_.reshape(cshape), m_.reshape(cshape), v_.reshape(cshape)

    order = ["norm1_g", "w_in", "sgu_ln_g", "sgu_ln_b", "sgu_w", "sgu_b", "conv_w", "conv_b", "conv_ln_g",
             "conv_ln_b", "w_out", "norm2_g", "w_ff1", "w_ff2", "final_g"]
    return (loss_out, grad_x, *[grads[n] for n in order], *[delta[n] for n in order],
            *[new_m[n] for n in order], *[new_v[n] for n in order])
```

```python
import functools

import jax
import jax.numpy as jnp
from jax import lax
from jax.experimental import pallas as pl
from jax.experimental.pallas import tpu as pltpu

F32 = jnp.float32
BF16 = jnp.bfloat16
MESH = pl.DeviceIdType.MESH

EPS = 1e-6
CHUNK = 128
CONV_TAPS = 31
HALO = 32
N_CHIPS = 4
N_DEV = 8
LANES = 128

ADAM_LR = 0.001
ADAM_B1 = 0.9
ADAM_B2 = 0.999
ADAM_EPS = 1e-08
ADAM_WD = 0.01
ADAM_STEP = 10

TM_MIX = 256
TM_FFN = 512
V7X_VMEM_BYTES = 64 * 2 ** 20
VMEM_LIMIT = V7X_VMEM_BYTES - 8 * 2 ** 20


def _params(sem=None):
    return pltpu.CompilerParams(dimension_semantics=sem, vmem_limit_bytes=VMEM_LIMIT)


def _erf(x):
    ax = jnp.abs(x)
    t = 1.0 / (1.0 + 0.3275911 * ax)
    poly = t * (0.254829592 + t * (-0.284496736 + t * (1.421413741 + t * (-1.453152027 + t * 1.061405429))))
    y = 1.0 - poly * jnp.exp(-ax * ax)
    return jnp.where(x < 0, -y, y)


def _gelu_and_grad(x):
    cdf = 0.5 * (1.0 + _erf(x * 0.7071067811865476))
    pdf = jnp.exp(-0.5 * x * x) * 0.3989422804014327
    return x * cdf, cdf + x * pdf


def _sigmoid(x):
    return 1.0 / (1.0 + jnp.exp(-x))


def _dot(a, b):
    return jnp.dot(a.astype(BF16), b.astype(BF16), preferred_element_type=F32)


def _dot_nt(a, b):
    return lax.dot_general(a.astype(BF16), b.astype(BF16), (((1,), (1,)), ((), ())), preferred_element_type=F32)


def _rms_fwd(x, g):
    r = lax.rsqrt(jnp.mean(x * x, axis=-1, keepdims=True) + EPS)
    xh = x * r
    return r, xh, xh * g


def _rms_bwd(dh, xh, r, g):
    dxh = dh * g
    return r * (dxh - xh * jnp.mean(dxh * xh, axis=-1, keepdims=True))


def _ln_stats(x):
    mu = jnp.mean(x, axis=-1, keepdims=True)
    xc = x - mu
    rs = lax.rsqrt(jnp.mean(xc * xc, axis=-1, keepdims=True) + EPS)
    return xc * rs, rs


def _ln_bwd(dxh, xh, rs):
    return rs * (dxh - jnp.mean(dxh, axis=-1, keepdims=True) - xh * jnp.mean(dxh * xh, axis=-1, keepdims=True))


def _group_ln_fwd(c):
    parts, rss = [], []
    for j in range(c.shape[1] // CHUNK):
        xh, rs = _ln_stats(c[:, j * CHUNK:(j + 1) * CHUNK])
        parts.append(xh)
        rss.append(rs)
    return jnp.concatenate(parts, axis=1), rss


def _group_ln_bwd(dxh, xh, rss):
    parts = []
    for j, rs in enumerate(rss):
        cols = slice(j * CHUNK, (j + 1) * CHUNK)
        parts.append(_ln_bwd(dxh[:, cols], xh[:, cols], rs))
    return jnp.concatenate(parts, axis=1)


def _rows_from(ext, off, tm):
    if off == 0:
        return ext[:tm]
    return pltpu.roll(ext, ext.shape[0] - off, 0)[:tm]


def _tril_mask():
    t = lax.broadcasted_iota(jnp.int32, (CHUNK, CHUNK), 0)
    s = lax.broadcasted_iota(jnp.int32, (CHUNK, CHUNK), 1)
    return t >= s


def _mix_forward(ua, va, vb, gb, g_halo, ws_ref, bfull_ref, lng, lnb, cw_ref, cb, cg, cbeta, mixed_scr):
    tm, da = ua.shape
    heads = da // CHUNK
    u, du_fac = _gelu_and_grad(ua)
    vg, dvg_fac = _gelu_and_grad(va)
    vh, v_rs = _ln_stats(vg)
    v = vh * lng + lnb
    mask = _tril_mask()
    wm = [jnp.where(mask, ws_ref[h], 0.0).astype(BF16) for h in range(heads)]
    vb16 = v.astype(BF16)
    for ci in range(tm // CHUNK):
        rows = slice(ci * CHUNK, (ci + 1) * CHUNK)
        for h in range(heads):
            cols = slice(h * CHUNK, (h + 1) * CHUNK)
            mixed_scr[rows, cols] = jnp.dot(wm[h], vb16[rows, cols], preferred_element_type=F32) + bfull_ref[h]
    mixed = mixed_scr[...]
    a = u * mixed

    sg = _sigmoid(gb)
    g = vb * sg
    ext = jnp.concatenate([g_halo, g], axis=0)
    cpre = jnp.zeros_like(g) + cb
    for k in range(CONV_TAPS):
        cpre = cpre + _rows_from(ext, HALO - (CONV_TAPS - 1) + k, tm) * cw_ref[k:k + 1, :]
    chh, c_rss = _group_ln_fwd(cpre)
    cn = chh * cg + cbeta
    sc = _sigmoid(cn)
    cout = cn * sc
    return dict(u=u, du_fac=du_fac, dvg_fac=dvg_fac, vh=vh, v_rs=v_rs, v16=vb16, mixed=mixed, a=a, sg=sg, g=g,
                ext=ext, chh=chh, c_rss=c_rss, cn=cn, sc=sc, cout=cout)


ANY = pl.BlockSpec(memory_space=pl.ANY)


def _call(body, name, grid, in_specs, out_specs, out_shape, scratch_shapes, sem, operands, comm=None):
    if comm is None:
        res = pl.pallas_call(body, name=name, grid=grid, in_specs=in_specs, out_specs=out_specs, out_shape=out_shape,
                             scratch_shapes=scratch_shapes, compiler_params=_params(sem))(*operands)
        return res, []
    n_in, n_out, n_scr = len(in_specs), len(out_specs), len(scratch_shapes)
    n_cin, n_cout = len(comm["operands"]), len(comm["out_shape"])

    def fused(*refs):
        ins, refs = refs[:n_in], refs[n_in:]
        c_ins, refs = refs[:n_cin], refs[n_cin:]
        outs, refs = refs[:n_out], refs[n_out:]
        c_outs, refs = refs[:n_cout], refs[n_cout:]
        scr, c_sems = refs[:n_scr], refs[n_scr:]
        first = functools.reduce(jnp.logical_and, [pl.program_id(a) == 0 for a in range(len(grid))])
        last = functools.reduce(jnp.logical_and, [pl.program_id(a) == grid[a] - 1 for a in range(len(grid))])
        start, finish = comm["make"](c_ins, c_outs, c_sems)
        pl.when(first)(start)
        body(*ins, *outs, *scr)
        pl.when(last)(finish)

    res = pl.pallas_call(
        fused, name=name, grid=grid,
        in_specs=list(in_specs) + [ANY] * n_cin, out_specs=list(out_specs) + [ANY] * n_cout,
        out_shape=list(out_shape) + list(comm["out_shape"]),
        input_output_aliases={n_in + a: n_out + b for a, b in comm["aliases"].items()},
        scratch_shapes=list(scratch_shapes) + list(comm["scratch"]),
        compiler_params=_params(sem),
    )(*operands, *comm["operands"])
    return res[:n_out], res[n_out:]


def _alone(name, comm):
    n_cin, n_cout = len(comm["operands"]), len(comm["out_shape"])

    def body(*refs):
        start, finish = comm["make"](refs[:n_cin], refs[n_cin:n_cin + n_cout], refs[n_cin + n_cout:])
        start()
        finish()

    return pl.pallas_call(
        body, name=name, in_specs=[ANY] * n_cin, out_specs=[ANY] * n_cout, out_shape=list(comm["out_shape"]),
        input_output_aliases=dict(comm["aliases"]), scratch_shapes=list(comm["scratch"]),
    )(*comm["operands"])


def _mix_fwd_call(l, x, g1, win_g, lng, lnb, ws, bfull, cw, cb, cg, cbeta, wout_g, comm=None):
    T, D = x.shape
    tm = min(TM_MIX, T)
    n_t = T // tm
    da = win_g.shape[-1]
    wo_rows = wout_g.shape[1]

    def body(x_ref, g1_ref, win_ref, lng_ref, lnb_ref, ws_ref, bf_ref, cw_ref, cb_ref, cg_ref, cbeta_ref, wout_ref,
             x1_ref, proj_ref, halo_scr, mixed_scr):
        i = pl.program_id(0)

        @pl.when(i == 0)
        def _():
            halo_scr[...] = jnp.zeros_like(halo_scr)

        x_t = x_ref[...]
        _, _, h1 = _rms_fwd(x_t, g1_ref[...])
        h1b = h1.astype(BF16)
        ps = []
        for p in range(N_CHIPS):
            pp = jnp.dot(h1b, win_ref[p], preferred_element_type=F32)
            proj_ref[:, p * da:(p + 1) * da] = pp
            ps.append(pp)
        f = _mix_forward(ps[0], ps[1], ps[2], ps[3], halo_scr[...], ws_ref, bf_ref, lng_ref[...], lnb_ref[...],
                         cw_ref, cb_ref[...], cg_ref[...], cbeta_ref[...], mixed_scr)
        halo_scr[...] = f["g"][tm - HALO:, :]
        mix = jnp.concatenate([f["a"], f["cout"]], axis=1).astype(BF16)
        acc = x_t
        for q in range(N_CHIPS):
            acc = acc + jnp.dot(mix[:, q * wo_rows:(q + 1) * wo_rows], wout_ref[q], preferred_element_type=F32)
        x1_ref[...] = acc

    full = lambda a: pl.BlockSpec(a.shape, lambda i: (0,) * a.ndim)
    return _call(
        body, f"mix_fwd_l{l}", (n_t,),
        in_specs=[
            pl.BlockSpec((tm, D), lambda i: (i, 0)),
            full(g1),
            pl.BlockSpec((N_CHIPS, D, da), lambda i: (0, 0, 0)),
            full(lng), full(lnb), full(ws), full(bfull), full(cw), full(cb), full(cg), full(cbeta),
            pl.BlockSpec((N_CHIPS, wo_rows, D), lambda i: (0, 0, 0)),
        ],
        out_specs=[
            pl.BlockSpec((tm, D), lambda i: (i, 0)),
            pl.BlockSpec((tm, N_CHIPS * da), lambda i: (i, 0)),
        ],
        out_shape=[jax.ShapeDtypeStruct((T, D), F32), jax.ShapeDtypeStruct((T, N_CHIPS * da), F32)],
        scratch_shapes=[pltpu.VMEM((HALO, da), F32), pltpu.VMEM((tm, da), F32)],
        sem=("arbitrary",),
        operands=(x, g1, win_g, lng, lnb, ws, bfull, cw, cb, cg, cbeta, wout_g), comm=comm)


def _ffn_tile(T):
    return min(TM_FFN, max(T // 2, CHUNK))


def _ffn_fwd_call(l, x1, g2, w1_g, w2_g, comm=None):
    T, D = x1.shape
    tm = _ffn_tile(T)
    n_t = T // tm
    ffb = w1_g.shape[-1]

    def body(x1_ref, g2_ref, w1_ref, w2_ref, x2_ref, z_ref, h2_scr, acc_scr):
        p = pl.program_id(1)

        @pl.when(p == 0)
        def _():
            _, _, h2 = _rms_fwd(x1_ref[...], g2_ref[...])
            h2_scr[...] = h2.astype(BF16)
            acc_scr[...] = jnp.zeros_like(acc_scr)

        z = jnp.dot(h2_scr[...], w1_ref[...], preferred_element_type=F32)
        z_ref[...] = z.astype(BF16)
        rz = jnp.maximum(z, 0.0)
        acc_scr[...] += jnp.dot((rz * rz).astype(BF16), w2_ref[...], preferred_element_type=F32)

        @pl.when(p == N_CHIPS - 1)
        def _():
            x2_ref[...] = x1_ref[...] + acc_scr[...]

    return _call(
        body, f"ffn_fwd_l{l}", (n_t, N_CHIPS),
        in_specs=[
            pl.BlockSpec((tm, D), lambda i, p: (i, 0)),
            pl.BlockSpec(g2.shape, lambda i, p: (0, 0)),
            pl.BlockSpec((None, D, ffb), lambda i, p: (p, 0, 0)),
            pl.BlockSpec((None, ffb, D), lambda i, p: (p, 0, 0)),
        ],
        out_specs=[
            pl.BlockSpec((tm, D), lambda i, p: (i, 0)),
            pl.BlockSpec((tm, ffb), lambda i, p: (i, p)),
        ],
        out_shape=[jax.ShapeDtypeStruct((T, D), F32), jax.ShapeDtypeStruct((T, N_CHIPS * ffb), BF16)],
        scratch_shapes=[pltpu.VMEM((tm, D), BF16), pltpu.VMEM((tm, D), F32)],
        sem=("arbitrary", "arbitrary"), operands=(x1, g2, w1_g, w2_g), comm=comm)


def _loss_call(x, gf, target):
    T, D = x.shape
    tm = min(TM_FFN, T)
    n_t = T // tm

    def body(x_ref, gf_ref, t_ref, dx_ref, dxb_ref, loss_ref, dgf_ref):
        i = pl.program_id(0)

        @pl.when(i == 0)
        def _():
            loss_ref[...] = jnp.zeros_like(loss_ref)
            dgf_ref[...] = jnp.zeros_like(dgf_ref)

        g = gf_ref[...]
        r, xh, y = _rms_fwd(x_ref[...], g)
        e = y - t_ref[...]
        per_tok = jnp.sum(e * e, axis=-1, keepdims=True) * (1.0 / D)
        loss_ref[...] += 0.5 * jnp.sum(per_tok, axis=0, keepdims=True)
        dy = e * (1.0 / D)
        dgf_ref[...] += jnp.sum(dy * xh, axis=0, keepdims=True)
        dx = _rms_bwd(dy, xh, r, g)
        dx_ref[...] = dx
        dxb_ref[...] = dx.astype(BF16)

    return pl.pallas_call(
        body, name="loss_head",
        grid=(n_t,),
        in_specs=[
            pl.BlockSpec((tm, D), lambda i: (i, 0)),
            pl.BlockSpec(gf.shape, lambda i: (0, 0)),
            pl.BlockSpec((tm, D), lambda i: (i, 0)),
        ],
        out_specs=[
            pl.BlockSpec((tm, D), lambda i: (i, 0)),
            pl.BlockSpec((tm, D), lambda i: (i, 0)),
            pl.BlockSpec((1, 1), lambda i: (0, 0)),
            pl.BlockSpec((1, D), lambda i: (0, 0)),
        ],
        out_shape=[jax.ShapeDtypeStruct((T, D), F32), jax.ShapeDtypeStruct((T, D), BF16),
                   jax.ShapeDtypeStruct((1, 1), F32), jax.ShapeDtypeStruct((1, D), F32)],
        compiler_params=_params(("arbitrary",)),
    )(x, gf, target)


def _ffn_bwd_call(l, dx2, dx2b, x1, z, g2, w1_g, w2_g, comm=None):
    T, D = x1.shape
    tm = _ffn_tile(T)
    n_t = T // tm
    ffb = w1_g.shape[-1]

    def body(dx2_ref, dx2b_ref, x1_ref, z_ref, g2_ref, w1_ref, w2_ref,
             dx1_ref, dx1b_ref, dz_ref, ft_ref, h2t_ref, dg2_ref, acc_scr):
        i = pl.program_id(0)
        p = pl.program_id(1)

        @pl.when(jnp.logical_and(i == 0, p == 0))
        def _():
            dg2_ref[...] = jnp.zeros_like(dg2_ref)

        @pl.when(p == 0)
        def _():
            _, _, h2 = _rms_fwd(x1_ref[...], g2_ref[...])
            h2t_ref[...] = h2.T.astype(BF16)
            acc_scr[...] = jnp.zeros_like(acc_scr)

        rz = jnp.maximum(z_ref[...].astype(F32), 0.0)
        ft_ref[...] = (rz * rz).T.astype(BF16)
        df = _dot_nt(dx2b_ref[...], w2_ref[...])
        dz = (df * (2.0 * rz)).astype(BF16)
        dz_ref[...] = dz
        acc_scr[...] += _dot_nt(dz, w1_ref[...])

        @pl.when(p == N_CHIPS - 1)
        def _():
            g = g2_ref[...]
            r, xh, _ = _rms_fwd(x1_ref[...], g)
            dh2 = acc_scr[...]
            dg2_ref[...] += jnp.sum(dh2 * xh, axis=0, keepdims=True)
            dx1 = dx2_ref[...] + _rms_bwd(dh2, xh, r, g)
            dx1_ref[...] = dx1
            dx1b_ref[...] = dx1.astype(BF16)

    return _call(
        body, f"ffn_bwd_l{l}", (n_t, N_CHIPS),
        in_specs=[
            pl.BlockSpec((tm, D), lambda i, p: (i, 0)),
            pl.BlockSpec((tm, D), lambda i, p: (i, 0)),
            pl.BlockSpec((tm, D), lambda i, p: (i, 0)),
            pl.BlockSpec((tm, ffb), lambda i, p: (i, p)),
            pl.BlockSpec(g2.shape, lambda i, p: (0, 0)),
            pl.BlockSpec((None, D, ffb), lambda i, p: (p, 0, 0)),
            pl.BlockSpec((None, ffb, D), lambda i, p: (p, 0, 0)),
        ],
        out_specs=[
            pl.BlockSpec((tm, D), lambda i, p: (i, 0)),
            pl.BlockSpec((tm, D), lambda i, p: (i, 0)),
            pl.BlockSpec((tm, ffb), lambda i, p: (i, p)),
            pl.BlockSpec((ffb, tm), lambda i, p: (p, i)),
            pl.BlockSpec((D, tm), lambda i, p: (0, i)),
            pl.BlockSpec((1, D), lambda i, p: (0, 0)),
        ],
        out_shape=[jax.ShapeDtypeStruct((T, D), F32), jax.ShapeDtypeStruct((T, D), BF16),
                   jax.ShapeDtypeStruct((T, N_CHIPS * ffb), BF16), jax.ShapeDtypeStruct((N_CHIPS * ffb, T), BF16),
                   jax.ShapeDtypeStruct((D, T), BF16), jax.ShapeDtypeStruct((1, D), F32)],
        scratch_shapes=[pltpu.VMEM((tm, D), F32)],
        sem=("arbitrary", "arbitrary"), operands=(dx2, dx2b, x1, z, g2, w1_g, w2_g), comm=comm)


def _mix_bwd_call(l, dx1, dx1b, x, proj, g1, win_g, lng, lnb, ws, wst, bfull, cw, cb, cg, cbeta, wout_g, comm=None):
    T, D = x.shape
    tm = min(TM_MIX, T)
    n_t = T // tm
    da = win_g.shape[-1]
    heads = da // CHUNK
    wo_rows = wout_g.shape[1]
    halo_blocks = tm // HALO

    def body(dx1_ref, dx1b_ref, x_ref, proj_ref, vbh_ref, gbh_ref, g1_ref, win_ref, lng_ref, lnb_ref, ws_ref, wst_ref,
             bf_ref, cw_ref, cb_ref, cg_ref, cbeta_ref, wout_ref,
             dx_ref, dxb_ref, dproj_ref, h1t_ref, mixt_ref,
             dg1_ref, dlng_ref, dlnb_ref, dws_ref, dbs_ref, dcw_ref, dcb_ref, dcg_ref, dcbeta_ref,
             carry_scr, mixed_scr, dv_scr):
        i = pl.program_id(0)
        tile = n_t - 1 - i

        @pl.when(i == 0)
        def _():
            carry_scr[...] = jnp.zeros_like(carry_scr)
            for ref in (dg1_ref, dlng_ref, dlnb_ref, dws_ref, dbs_ref, dcw_ref, dcb_ref, dcg_ref, dcbeta_ref):
                ref[...] = jnp.zeros_like(ref)

        g1v = g1_ref[...]
        r, xh, h1 = _rms_fwd(x_ref[...], g1v)
        h1t_ref[...] = h1.T.astype(BF16)

        ua = proj_ref[:, 0 * da:1 * da]
        va = proj_ref[:, 1 * da:2 * da]
        vb = proj_ref[:, 2 * da:3 * da]
        gb = proj_ref[:, 3 * da:4 * da]
        g_halo = jnp.where(tile > 0, vbh_ref[...] * _sigmoid(gbh_ref[...]), 0.0)
        lng_v, cg_v = lng_ref[...], cg_ref[...]
        f = _mix_forward(ua, va, vb, gb, g_halo, ws_ref, bf_ref, lng_v, lnb_ref[...], cw_ref, cb_ref[...], cg_v,
                         cbeta_ref[...], mixed_scr)
        mixt_ref[0:da, :] = f["a"].T.astype(BF16)
        mixt_ref[da:2 * da, :] = f["cout"].T.astype(BF16)

        dxo = dx1b_ref[...]
        dmix = jnp.concatenate([_dot_nt(dxo, wout_ref[q]) for q in range(N_CHIPS)], axis=1)
        da_ = dmix[:, :da]
        dc_ = dmix[:, da:]

        dua = da_ * f["mixed"] * f["du_fac"]
        dmixed = (da_ * f["u"]).astype(BF16)
        mask_t = (lax.broadcasted_iota(jnp.int32, (CHUNK, CHUNK), 1)
                  >= lax.broadcasted_iota(jnp.int32, (CHUNK, CHUNK), 0))
        wmt =[jnp.where(mask_t, wst_ref[h], 0.0).astype(BF16) for h in range(heads)]
        mask = _tril_mask()
        v16 = f["v16"]
        for h in range(heads):
            cols = slice(h * CHUNK, (h + 1) * CHUNK)
            dws_h = jnp.zeros((CHUNK, CHUNK), F32)
            dbs_h = jnp.zeros((CHUNK, CHUNK), F32)
            for ci in range(tm // CHUNK):
                rows = slice(ci * CHUNK, (ci + 1) * CHUNK)
                dm = dmixed[rows, cols]
                dv_scr[rows, cols] = jnp.dot(wmt[h], dm, preferred_element_type=F32)
                dws_h = dws_h + _dot_nt(dm, v16[rows, cols])
                dbs_h = dbs_h + dm.astype(F32)
            dws_ref[h] += jnp.where(mask, dws_h, 0.0)
            dbs_ref[h] += jnp.broadcast_to(jnp.sum(dbs_h, axis=1, keepdims=True), (CHUNK, CHUNK))
        dv = dv_scr[...]
        dlng_ref[...] += jnp.sum(dv * f["vh"], axis=0, keepdims=True)
        dlnb_ref[...] += jnp.sum(dv, axis=0, keepdims=True)
        dva = _ln_bwd(dv * lng_v, f["vh"], f["v_rs"]) * f["dvg_fac"]

        cn, sc = f["cn"], f["sc"]
        dcn = dc_ * (sc * (1.0 + cn * (1.0 - sc)))
        dcg_ref[...] += jnp.sum(dcn * f["chh"], axis=0, keepdims=True)
        dcbeta_ref[...] += jnp.sum(dcn, axis=0, keepdims=True)
        dcpre = _group_ln_bwd(dcn * cg_v, f["chh"], f["c_rss"])
        dcb_ref[...] += jnp.sum(dcpre, axis=0, keepdims=True)
        ext = f["ext"]
        dext = jnp.concatenate([dcpre, carry_scr[...]], axis=0)
        dg = jnp.zeros_like(dcpre)
        for k in range(CONV_TAPS):
            g_k = _rows_from(ext, HALO - (CONV_TAPS - 1) + k, tm)
            dcw_ref[k:k + 1, :] += jnp.sum(dcpre * g_k, axis=0, keepdims=True)
            dg = dg + _rows_from(dext, CONV_TAPS - 1 - k, tm) * cw_ref[k:k + 1, :]
        carry_scr[...] = dcpre[:HALO, :]
        sg = f["sg"]
        dvb = dg * sg
        dgb = dg * vb * sg * (1.0 - sg)

        dps = [dua.astype(BF16), dva.astype(BF16), dvb.astype(BF16), dgb.astype(BF16)]
        dh1 = jnp.zeros((tm, D), F32)
        for p in range(N_CHIPS):
            dproj_ref[:, p * da:(p + 1) * da] = dps[p]
            dh1 = dh1 + _dot_nt(dps[p], win_ref[p])
        dg1_ref[...] += jnp.sum(dh1 * xh, axis=0, keepdims=True)
        dx = dx1_ref[...] + _rms_bwd(dh1, xh, r, g1v)
        dx_ref[...] = dx
        dxb_ref[...] = dx.astype(BF16)

    rev = lambda i: (n_t - 1 - i, 0)
    full = lambda a: pl.BlockSpec(a.shape, lambda i: (0,) * a.ndim)
    acc = lambda shape: pl.BlockSpec(shape, lambda i: (0,) * len(shape))
    halo_idx = lambda col: (lambda i: (jnp.maximum((n_t - 1 - i) * halo_blocks - 1, 0), col))
    small_shapes = [(1, D), (1, da), (1, da), (heads, CHUNK, CHUNK), (heads, CHUNK, CHUNK), (HALO, da),
                    (1, da), (1, da), (1, da)]
    return _call(
        body, f"mix_bwd_l{l}", (n_t,),
        in_specs=[
            pl.BlockSpec((tm, D), rev),
            pl.BlockSpec((tm, D), rev),
            pl.BlockSpec((tm, D), rev),
            pl.BlockSpec((tm, N_CHIPS * da), rev),
            pl.BlockSpec((HALO, da), halo_idx(2)),
            pl.BlockSpec((HALO, da), halo_idx(3)),
            full(g1),
            pl.BlockSpec((N_CHIPS, D, da), lambda i: (0, 0, 0)),
            full(lng), full(lnb), full(ws), full(wst), full(bfull), full(cw), full(cb), full(cg), full(cbeta),
            pl.BlockSpec((N_CHIPS, wo_rows, D), lambda i: (0, 0, 0)),
        ],
        out_specs=[
            pl.BlockSpec((tm, D), rev),
            pl.BlockSpec((tm, D), rev),
            pl.BlockSpec((tm, N_CHIPS * da), rev),
            pl.BlockSpec((D, tm), lambda i: (0, n_t - 1 - i)),
            pl.BlockSpec((2 * da, tm), lambda i: (0, n_t - 1 - i)),
        ] + [acc(s) for s in small_shapes],
        out_shape=[jax.ShapeDtypeStruct((T, D), F32), jax.ShapeDtypeStruct((T, D), BF16),
                   jax.ShapeDtypeStruct((T, N_CHIPS * da), BF16), jax.ShapeDtypeStruct((D, T), BF16),
                   jax.ShapeDtypeStruct((2 * da, T), BF16)] + [jax.ShapeDtypeStruct(s, F32) for s in small_shapes],
        scratch_shapes=[pltpu.VMEM((HALO, da), F32), pltpu.VMEM((tm, da), F32), pltpu.VMEM((tm, da), F32)],
        sem=("arbitrary",),
        operands=(dx1, dx1b, x, proj, proj, proj, g1, win_g, lng, lnb, ws, wst, bfull, cw, cb, cg, cbeta, wout_g),
        comm=comm)


def _wgrad_call(name, at, b, split_cols, steps):
    M, T = at.shape
    N = b.shape[1]
    if split_cols:
        bn = N // (N_CHIPS * steps)
        in_specs = [pl.BlockSpec((M, T), lambda j: (0, 0)), pl.BlockSpec((T, bn), lambda j: (0, j))]
        out_spec = pl.BlockSpec((None, M, bn), lambda j: (j // steps, 0, j % steps))
        out_shape = jax.ShapeDtypeStruct((N_CHIPS, M, N // N_CHIPS), BF16)
    else:
        bm = M // (N_CHIPS * steps)
        in_specs = [pl.BlockSpec((bm, T), lambda j: (j, 0)), pl.BlockSpec((T, N), lambda j: (0, 0))]
        out_spec = pl.BlockSpec((None, bm, N), lambda j: (j // steps, j % steps, 0))
        out_shape = jax.ShapeDtypeStruct((N_CHIPS, M // N_CHIPS, N), BF16)

    def body(at_ref, b_ref, o_ref):
        o_ref[...] = jnp.dot(at_ref[...], b_ref[...], preferred_element_type=F32).astype(BF16)

    return pl.pallas_call(
        body, name=name, grid=(N_CHIPS * steps,), in_specs=in_specs, out_specs=out_spec, out_shape=out_shape,
        compiler_params=_params(("arbitrary",)),
    )(at, b)


def _rows_block(rows, cols):
    br = rows
    while br * cols * 4 > 2 ** 20 and br % 16 == 0:
        br //= 2
    return br


def _cast_own_call(name, chip, w, l):
    _, rows, cols = w.shape
    br = _rows_block(rows, cols)
    n_b = rows // br

    def body(chip_ref, w_ref, o_ref):
        o_ref[...] = w_ref[...].astype(BF16)

    return pl.pallas_call(
        body, name=name,
        grid_spec=pltpu.PrefetchScalarGridSpec(
            num_scalar_prefetch=1, grid=(n_b,),
            in_specs=[pl.BlockSpec((None, br, cols), lambda i, chip_ref: (l, i, 0))],
            out_specs=pl.BlockSpec((None, br, cols), lambda i, chip_ref: (chip_ref[0], i, 0))),
        out_shape=jax.ShapeDtypeStruct((N_CHIPS, rows, cols), BF16),
        compiler_params=_params(("arbitrary",)),
    )(chip, w)


def _pair_sum_call(name, core, g, got):
    _, rh, cols = got.shape
    br = _rows_block(rh, cols)
    n_b = rh // br

    def body(core_ref, g_ref, got_ref, o_ref):
        o_ref[...] = (g_ref[...].astype(F32) + got_ref[...].astype(F32)).astype(BF16)

    half = pl.BlockSpec((None, br, cols), lambda q, i, core_ref: (q, i, 0))
    return pl.pallas_call(
        body, name=name,
        grid_spec=pltpu.PrefetchScalarGridSpec(
            num_scalar_prefetch=1, grid=(N_CHIPS, n_b),
            in_specs=[pl.BlockSpec((None, br, cols), lambda q, i, core_ref: (q, core_ref[0] * n_b + i, 0)), half],
            out_specs=half),
        out_shape=jax.ShapeDtypeStruct(got.shape, BF16),
        compiler_params=_params(("arbitrary", "arbitrary")),
    )(core, g, got)


def _chip_sum_call(name, core, parts):
    _, rh, cols = parts.shape
    br = _rows_block(rh, cols)
    n_b = rh // br

    def body(core_ref, *refs):
        o_ref = refs[N_CHIPS]
        total = refs[0][...].astype(F32)
        for q in range(1, N_CHIPS):
            total = total + refs[q][...].astype(F32)
        o_ref[...] = total

    return pl.pallas_call(
        body, name=name,
        grid_spec=pltpu.PrefetchScalarGridSpec(
            num_scalar_prefetch=1, grid=(n_b,),
            in_specs=[pl.BlockSpec((None, br, cols), functools.partial(lambda i, core_ref, q: (q, i, 0), q=q))
                      for q in range(N_CHIPS)],
            out_specs=pl.BlockSpec((br, cols), lambda i, core_ref: (core_ref[0] * n_b + i, 0))),
        out_shape=jax.ShapeDtypeStruct((2 * rh, cols), F32),
        compiler_params=_params(("arbitrary",)),
    )(core, *([parts] * N_CHIPS))


def _adamw_layers_call(name, w, g0, g1, m, v):
    _, rows, cols = w.shape
    br = _rows_block(rows, cols)

    def body(w_ref, g0_ref, g1_ref, m_ref, v_ref, g_ref, d_ref, nm_ref, nv_ref):
        gv = jnp.where(pl.program_id(0) == 0, g0_ref[...], g1_ref[...])
        g_ref[...] = gv
        m_new = ADAM_B1 * m_ref[...] + (1.0 - ADAM_B1) * gv
        v_new = ADAM_B2 * v_ref[...] + (1.0 - ADAM_B2) * (gv * gv)
        m_hat = m_new / (1.0 - ADAM_B1 ** ADAM_STEP)
        v_hat = v_new / (1.0 - ADAM_B2 ** ADAM_STEP)
        d_ref[...] = -ADAM_LR * (m_hat / (jnp.sqrt(v_hat) + ADAM_EPS) + ADAM_WD * w_ref[...])
        nm_ref[...] = m_new
        nv_ref[...] = v_new

    both = pl.BlockSpec((None, br, cols), lambda l, i: (l, i, 0))
    one = pl.BlockSpec((br, cols), lambda l, i: (i, 0))
    return pl.pallas_call(
        body, name=name, grid=(2, rows // br), in_specs=[both, one, one, both, both], out_specs=[both] * 4,
        out_shape=[jax.ShapeDtypeStruct(w.shape, F32)] * 4,
        compiler_params=_params(("arbitrary", "arbitrary")),
    )(w, g0, g1, m, v)


def _adamw_call(name, w, g, m, v):
    rows, cols = w.shape
    br = _rows_block(rows, cols)

    def body(w_ref, g_ref, m_ref, v_ref, d_ref, nm_ref, nv_ref):
        gv = g_ref[...]
        m_new = ADAM_B1 * m_ref[...] + (1.0 - ADAM_B1) * gv
        v_new = ADAM_B2 * v_ref[...] + (1.0 - ADAM_B2) * (gv * gv)
        m_hat = m_new / (1.0 - ADAM_B1 ** ADAM_STEP)
        v_hat = v_new / (1.0 - ADAM_B2 ** ADAM_STEP)
        d_ref[...] = -ADAM_LR * (m_hat / (jnp.sqrt(v_hat) + ADAM_EPS) + ADAM_WD * w_ref[...])
        nm_ref[...] = m_new
        nv_ref[...] = v_new

    spec = pl.BlockSpec((br, cols), lambda i: (i, 0))
    return pl.pallas_call(
        body, name=name, grid=(rows // br,), in_specs=[spec] * 4, out_specs=[spec] * 3,
        out_shape=[jax.ShapeDtypeStruct((rows, cols), F32)] * 3,
        compiler_params=_params(("arbitrary",)),
    )(w, g, m, v)


def _place():
    x, y, c = lax.axis_index("x"), lax.axis_index("y"), lax.axis_index("c")
    chips = [(1 - x, y), (x, 1 - y), (1 - x, 1 - y)]
    return x, y, c, 2 * x + y, chips


def _half_rows(ref, core):
    rh = ref.shape[-2] // 2
    rows = pl.ds(pl.multiple_of(core * rh, rh), rh)
    return ref.at[rows] if len(ref.shape) == 2 else ref.at[:, rows]


def _gather_comm(bufs, cw_shard=None):
    n = len(bufs)
    with_cw = cw_shard is not None

    def make(c_ins, c_outs, sems):
        ins, outs = c_ins[:n], c_outs[:n]
        send_sems, recv_sems, cw_send, cw_recv, cw_local = sems
        x, y, c, p, chips = _place()
        sibling = (x, y, 1 - c)
        qs = [2 * cx + cy for cx, cy in chips]
        hops = len(chips)

        def remote(k, j, src, dst, to):
            return pltpu.make_async_remote_copy(src_ref=src, dst_ref=dst, send_sem=send_sems.at[k, j],
                                                recv_sem=recv_sems.at[k, j], device_id=to, device_id_type=MESH)

        def cw_copy(j, dst, to):
            return pltpu.make_async_remote_copy(src_ref=c_ins[n], dst_ref=dst, send_sem=cw_send.at[j],
                                                recv_sem=cw_recv.at[j], device_id=to, device_id_type=MESH)

        def over_ici():
            return [remote(k, j, _half_rows(ins[k].at[p], c), _half_rows(outs[k].at[p], c), (*chip, c))
                    for j, chip in enumerate(chips) for k in range(n)]

        def passed_on():
            return [remote(k, hops + j, _half_rows(outs[k].at[qs[j]], c), _half_rows(outs[k].at[qs[j]], c), sibling)
                    for j in range(hops) for k in range(n)]

        def cw_copies():
            return [cw_copy(j, c_outs[n].at[p], (*chip, c)) for j, chip in enumerate(chips)]

        def cw_own():
            return pltpu.make_async_copy(c_ins[n], c_outs[n].at[p], cw_local)

        def start():
            for cp in over_ici():
                cp.start()
            if with_cw:
                cw_own().start()
                for cp in cw_copies():
                    cp.start()

        def finish():
            for j in range(hops):
                for k in range(n):
                    landed = _half_rows(outs[k].at[qs[j]], c)
                    remote(k, j, landed, landed, sibling).wait_recv()
                    remote(k, hops + j, landed, landed, sibling).start()
            for j in range(hops):
                for k in range(n):
                    other = _half_rows(outs[k].at[qs[j]], 1 - c)
                    remote(k, hops + j, other, other, sibling).wait_recv()
            if with_cw:
                for j in range(hops):
                    cw_copy(j, c_outs[n].at[qs[j]], sibling).wait_recv()
                cw_own().wait()
                for cp in cw_copies():
                    cp.wait_send()
            for cp in over_ici() + passed_on():
                cp.wait_send()

        return start, finish

    out_shape = [jax.ShapeDtypeStruct(b.shape, b.dtype) for b in bufs]
    if with_cw:
        out_shape.append(jax.ShapeDtypeStruct((N_CHIPS,) + cw_shard.shape, cw_shard.dtype))
    return dict(operands=list(bufs) + ([cw_shard] if with_cw else []), out_shape=out_shape,
                aliases={k: k for k in range(n)}, make=make,
                scratch=[pltpu.SemaphoreType.DMA((n, 6)), pltpu.SemaphoreType.DMA((n, 6)),
                         pltpu.SemaphoreType.DMA((3,)), pltpu.SemaphoreType.DMA((3,)), pltpu.SemaphoreType.DMA])


def _pair_exchange_call(l, gs):
    n = len(gs)

    def body(*refs):
        g, got = refs[:n], refs[n:2 * n]
        send_sems, recv_sems = refs[2 * n:]
        x, y, c, _, _ = _place()
        sibling = (x, y, 1 - c)
        copies = [pltpu.make_async_remote_copy(src_ref=_half_rows(g[k], 1 - c), dst_ref=got[k],
                                               send_sem=send_sems.at[k], recv_sem=recv_sems.at[k],
                                               device_id=sibling, device_id_type=MESH) for k in range(n)]
        for cp in copies:
            cp.start()
        for cp in copies:
            cp.wait_send()
            cp.wait_recv()

    return pl.pallas_call(
        body, name=f"grad_pair_exchange_l{l}",
        in_specs=[ANY] * n, out_specs=[ANY] * n,
        out_shape=[jax.ShapeDtypeStruct((g.shape[0], g.shape[1] // 2, g.shape[2]), g.dtype) for g in gs],
        scratch_shapes=[pltpu.SemaphoreType.DMA((n,)), pltpu.SemaphoreType.DMA((n,))],
    )(*gs)


def _exchange_comm(sums):
    n = len(sums)

    def make(ins, outs, sems):
        send_sems, recv_sems, local_sems = sems
        x, y, c, p, chips = _place()
        qs = [2 * cx + cy for cx, cy in chips]

        def remote(k, j, src, dst, to):
            return pltpu.make_async_remote_copy(src_ref=src, dst_ref=dst, send_sem=send_sems.at[k, j],
                                                recv_sem=recv_sems.at[k, j], device_id=to, device_id_type=MESH)

        def own():
            return [pltpu.make_async_copy(ins[k].at[p], outs[k].at[p], local_sems.at[k]) for k in range(n)]

        def sent():
            return [remote(k, j, ins[k].at[qs[j]], outs[k].at[p], (*chip, c))
                    for j, chip in enumerate(chips) for k in range(n)]

        def start():
            for cp in own() + sent():
                cp.start()

        def finish():
            for j in range(len(chips)):
                for k in range(n):
                    remote(k, j, ins[k].at[qs[j]], outs[k].at[qs[j]], (x, y, c)).wait_recv()
            for cp in sent():
                cp.wait_send()
            for cp in own():
                cp.wait()

        return start, finish

    return dict(operands=list(sums), out_shape=[jax.ShapeDtypeStruct(s.shape, s.dtype) for s in sums], aliases={},
                make=make, scratch=[pltpu.SemaphoreType.DMA((n, 3)), pltpu.SemaphoreType.DMA((n, 3)),
                                    pltpu.SemaphoreType.DMA((n,))])


def _pair_gather_call(l, halves):
    n = len(halves)

    def body(*refs):
        ins, outs = refs[:n], refs[n:2 * n]
        send_sems, recv_sems = refs[2 * n:]
        x, y, c, _, _ = _place()
        sibling = (x, y, 1 - c)

        def remote(k, src, dst):
            return pltpu.make_async_remote_copy(src_ref=src, dst_ref=dst, send_sem=send_sems.at[k],
                                                recv_sem=recv_sems.at[k], device_id=sibling, device_id_type=MESH)

        sent = [remote(k, _half_rows(ins[k], c), _half_rows(outs[k], c)) for k in range(n)]
        for cp in sent:
            cp.start()
        for k in range(n):
            other = _half_rows(outs[k], 1 - c)
            remote(k, other, other).wait_recv()
        for cp in sent:
            cp.wait_send()

    return pl.pallas_call(
        body, name=f"grad_pair_gather_l{l}",
        in_specs=[ANY] * n, out_specs=[ANY] * n,
        out_shape=[jax.ShapeDtypeStruct(h.shape, h.dtype) for h in halves],
        input_output_aliases={k: k for k in range(n)},
        scratch_shapes=[pltpu.SemaphoreType.DMA((n,)), pltpu.SemaphoreType.DMA((n,))],
    )(*halves)


def _all_sum_small_call(block):
    m_per, n = block.shape

    def body(x_ref, sum_ref, all_ref, send_sems, recv_sems, local_sem):
        x, y, c, _, chip_list = _place()
        me, sibling = (x, y, c), (x, y, 1 - c)

        def rows(px, py, pc):
            return all_ref.at[pl.ds((4 * px + 2 * py + pc) * m_per, m_per), :]

        def copy(k, blk, to, src=None):
            return pltpu.make_async_remote_copy(src_ref=rows(*blk) if src is None else src, dst_ref=rows(*blk),
                                                send_sem=send_sems.at[k], recv_sem=recv_sems.at[k],
                                                device_id=to, device_id_type=MESH)

        mine = pltpu.make_async_copy(x_ref, rows(*me), local_sem)
        mine.start()
        first = [copy(0, me, sibling, src=x_ref)]
        first += [copy(1 + j, me, (*chip, c), src=x_ref) for j, chip in enumerate(chip_list)]
        for cp in first:
            cp.start()
        passed = [copy(4 + j, (*chip, c), sibling) for j, chip in enumerate(chip_list)]
        for j, chip in enumerate(chip_list):
            copy(1 + j, (*chip, c), me).wait_recv()
            passed[j].start()
        copy(0, sibling, me).wait_recv()
        for j, chip in enumerate(chip_list):
            copy(4 + j, (*chip, 1 - c), me).wait_recv()
        for cp in first + passed:
            cp.wait_send()
        mine.wait()
        total = all_ref[0:m_per, :]
        for d in range(1, N_DEV):
            total = total + all_ref[d * m_per:(d + 1) * m_per, :]
        sum_ref[...] = total

    vmem = pl.BlockSpec(memory_space=pltpu.VMEM)
    return pl.pallas_call(
        body, name="small_all_sum",
        in_specs=[vmem], out_specs=[vmem, vmem],
        out_shape=[jax.ShapeDtypeStruct((m_per, n), F32), jax.ShapeDtypeStruct((N_DEV * m_per, n), F32)],
        scratch_shapes=[pltpu.SemaphoreType.DMA((7,)), pltpu.SemaphoreType.DMA((7,)), pltpu.SemaphoreType.DMA],
        compiler_params=pltpu.CompilerParams(vmem_limit_bytes=VMEM_LIMIT),
    )(block)[0]


SMALL_NAMES = ["norm1_g", "sgu_ln_g", "sgu_ln_b", "sgu_w", "sgu_b", "conv_b", "conv_ln_g", "conv_ln_b", "norm2_g",
               "final_g"]
MIX_NAMES = ["w_in", "w_out"]
FFN_NAMES = ["w_ff1", "w_ff2"]
BIG_NAMES = MIX_NAMES + FFN_NAMES


def _rows128(a):
    return a.reshape(-1, LANES)


def kernel(x, norm1_g, w_in, sgu_ln_g, sgu_ln_b, sgu_w, sgu_b, conv_w, conv_b, conv_ln_g, conv_ln_b, w_out, norm2_g, w_ff1, w_ff2, final_g, loss_target, m_norm1_g, m_w_in, m_sgu_ln_g, m_sgu_ln_b, m_sgu_w, m_sgu_b, m_conv_w, m_conv_b, m_conv_ln_g, m_conv_ln_b, m_w_out, m_norm2_g, m_w_ff1, m_w_ff2, m_final_g, v_norm1_g, v_w_in, v_sgu_ln_g, v_sgu_ln_b, v_sgu_w, v_sgu_b, v_conv_w, v_conv_b, v_conv_ln_g, v_conv_ln_b, v_w_out, v_norm2_g, v_w_ff1, v_w_ff2, v_final_g):
    w = dict(norm1_g=norm1_g, w_in=w_in, sgu_ln_g=sgu_ln_g, sgu_ln_b=sgu_ln_b, sgu_w=sgu_w, sgu_b=sgu_b,
             conv_w=conv_w, conv_b=conv_b, conv_ln_g=conv_ln_g, conv_ln_b=conv_ln_b, w_out=w_out, norm2_g=norm2_g,
             w_ff1=w_ff1, w_ff2=w_ff2, final_g=final_g)
    m = dict(norm1_g=m_norm1_g, w_in=m_w_in, sgu_ln_g=m_sgu_ln_g, sgu_ln_b=m_sgu_ln_b, sgu_w=m_sgu_w, sgu_b=m_sgu_b,
             conv_w=m_conv_w, conv_b=m_conv_b, conv_ln_g=m_conv_ln_g, conv_ln_b=m_conv_ln_b, w_out=m_w_out,
             norm2_g=m_norm2_g, w_ff1=m_w_ff1, w_ff2=m_w_ff2, final_g=m_final_g)
    v = dict(norm1_g=v_norm1_g, w_in=v_w_in, sgu_ln_g=v_sgu_ln_g, sgu_ln_b=v_sgu_ln_b, sgu_w=v_sgu_w, sgu_b=v_sgu_b,
             conv_w=v_conv_w, conv_b=v_conv_b, conv_ln_g=v_conv_ln_g, conv_ln_b=v_conv_ln_b, w_out=v_w_out,
             norm2_g=v_norm2_g, w_ff1=v_w_ff1, w_ff2=v_w_ff2, final_g=v_final_g)
    depth = w_in.shape[0]
    assert depth == 2, "core c owns layer c of every gradient"
    T, D = x.shape[1], x.shape[2]
    heads = sgu_w.shape[1]
    da = heads * CHUNK
    core = lax.axis_index("c")
    chip = 2 * lax.axis_index("x") + lax.axis_index("y")

    core_arr = core.reshape(1).astype(jnp.int32)
    chip_arr = chip.reshape(1).astype(jnp.int32)
    cw_pad = jnp.pad(conv_w, ((0, 0), (0, HALO - CONV_TAPS), (0, 0)))
    own = [{name: _cast_own_call(f"cast_{name}_l{l}", chip_arr, w[name], l) for name in BIG_NAMES}
           for l in range(depth)]
    gather_of = lambda l, group: _gather_comm([own[l][name] for name in group])
    win_g, wout_g, cw_g = _alone("gather_mix_l0", _gather_comm([own[0][name] for name in MIX_NAMES], cw_pad))
    cw_full = jnp.transpose(cw_g, (1, 2, 0, 3)).reshape(depth, HALO, da)

    ws_t = jnp.swapaxes(sgu_w, -1, -2)
    b_full = jnp.broadcast_to(sgu_b[..., None], sgu_w.shape)
    row = lambda a, l: a[l:l + 1]

    xs, projs, x1s, zs = [], [], [], []
    h = x.reshape(T, D)
    gathered = []
    for l in range(depth):
        xs.append(h)
        (x1, proj), (w1_g, w2_g) = _mix_fwd_call(
            l, h, row(norm1_g, l), win_g, row(sgu_ln_g, l), row(sgu_ln_b, l), sgu_w[l], b_full[l], cw_full[l],
            row(conv_b, l), row(conv_ln_g, l), row(conv_ln_b, l), wout_g, comm=gather_of(l, FFN_NAMES))
        gathered.append((win_g, wout_g, w1_g, w2_g))
        (h, z), nxt = _ffn_fwd_call(l, x1, row(norm2_g, l), w1_g, w2_g,
                                    comm=gather_of(l + 1, MIX_NAMES) if l + 1 < depth else None)
        if nxt:
            win_g, wout_g = nxt
        projs.append(proj)
        x1s.append(x1)
        zs.append(z)
    dx, dxb, loss, d_final_g = _loss_call(h, final_g.reshape(1, D), loss_target.reshape(T, D))

    big = {name: [None] * depth for name in BIG_NAMES}
    reduced = {name: [None] * depth for name in BIG_NAMES}
    small = {name: [None] * depth for name in SMALL_NAMES[:-1] + ["conv_w"]}

    def pair_sums(l, group):
        got = _pair_exchange_call(f"{group[0]}_l{l}", [big[name][l] for name in group])
        return [_pair_sum_call(f"pair_sum_{name}_l{l}", core_arr, big[name][l], recv) for name, recv in zip(group, got)]

    def finish_reduce(l, group, parts):
        halves = [_chip_sum_call(f"chip_sum_{name}_l{l}", core_arr, part) for name, part in zip(group, parts)]
        for name, full in zip(group, _pair_gather_call(f"{group[0]}_l{l}", halves)):
            reduced[name][l] = full

    pending = None
    for l in reversed(range(depth)):
        win_g, wout_g, w1_g, w2_g = gathered[l]
        (dx1, dx1b, dz, f_t, h2_t, dg2), parts = _ffn_bwd_call(
            l, dx, dxb, x1s[l], zs[l], row(norm2_g, l), w1_g, w2_g,
            comm=_exchange_comm(pending[2]) if pending else None)
        if pending:
            finish_reduce(pending[0], pending[1], parts)
        big["w_ff2"][l] = _wgrad_call(f"wgrad_ff2_l{l}", f_t, dxb, False, 2)
        big["w_ff1"][l] = _wgrad_call(f"wgrad_ff1_l{l}", h2_t, dz, True, 2)
        ffn_sums = pair_sums(l, FFN_NAMES)
        ((dx, dxb, dproj, h1_t, mix_t, dg1, dlng, dlnb, dws, dbs, dcw, dcb, dcg, dcbeta), parts) = _mix_bwd_call(
            l, dx1, dx1b, xs[l], projs[l], row(norm1_g, l), win_g, row(sgu_ln_g, l), row(sgu_ln_b, l), sgu_w[l],
            ws_t[l], b_full[l], cw_full[l], row(conv_b, l), row(conv_ln_g, l), row(conv_ln_b, l), wout_g,
            comm=_exchange_comm(ffn_sums))
        finish_reduce(l, FFN_NAMES, parts)
        big["w_out"][l] = _wgrad_call(f"wgrad_out_l{l}", mix_t, dx1b, False, 1)
        big["w_in"][l] = _wgrad_call(f"wgrad_in_l{l}", h1_t, dproj, True, 1)
        pending = (l, MIX_NAMES, pair_sums(l, MIX_NAMES))
        small["norm1_g"][l] = dg1[0]
        small["sgu_ln_g"][l] = dlng[0]
        small["sgu_ln_b"][l] = dlnb[0]
        small["sgu_w"][l] = dws
        small["sgu_b"][l] = dbs[:, :, 0]
        small["conv_w"][l] = dcw[:CONV_TAPS]
        small["conv_b"][l] = dcb[0]
        small["conv_ln_g"][l] = dcg[0]
        small["conv_ln_b"][l] = dcbeta[0]
        small["norm2_g"][l] = dg2[0]
    finish_reduce(pending[0], pending[1], _alone("grad_chip_exchange_last", _exchange_comm(pending[2])))
    grad_x = dx.reshape(x.shape)

    small_local = {name: jnp.stack(small[name]) for name in small}
    small_local["final_g"] = d_final_g[0]
    pieces = [_rows128(small_local[name]) for name in SMALL_NAMES]
    pieces.append(_rows128(small_local["conv_w"]))
    pieces.append(jnp.broadcast_to(loss, (8, LANES)))
    offsets = [0]
    for piece in pieces:
        offsets.append(offsets[-1] + piece.shape[0])
    summed = _all_sum_small_call(jnp.concatenate(pieces, axis=0))
    n_small = offsets[len(SMALL_NAMES)]
    loss_out = summed[offsets[-2], 0]
    small_grads = {name: summed[offsets[k]:offsets[k + 1]].reshape(w[name].shape)
                   for k, name in enumerate(SMALL_NAMES)}
    conv_w_full = summed[offsets[-3]:offsets[-2]].reshape(depth, CONV_TAPS, da)
    conv_w_grad = lax.dynamic_slice_in_dim(conv_w_full, chip * conv_w.shape[-1], conv_w.shape[-1], axis=2)

    grads, delta, new_m, new_v = {}, {}, {}, {}
    for name in BIG_NAMES:
        grads[name], delta[name], new_m[name], new_v[name] = _adamw_layers_call(
            f"adamw_{name}", w[name], reduced[name][0], reduced[name][1], m[name], v[name])
    pack = lambda src: jnp.concatenate([_rows128(src[name]) for name in SMALL_NAMES], axis=0)
    d_, m_, v_ = _adamw_call("adamw_small", pack(w), summed[:n_small], pack(m), pack(v))
    for k, name in enumerate(SMALL_NAMES):
        sl = slice(offsets[k], offsets[k + 1])
        grads[name] = small_grads[name]
        delta[name] = d_[sl].reshape(w[name].shape)
        new_m[name] = m_[sl].reshape(w[name].shape)
        new_v[name] = v_[sl].reshape(w[name].shape)
    cshape = conv_w.shape
    flat = lambda a: a.reshape(-1, cshape[-1])
    d_, m_, v_ = _adamw_call("adamw_conv_w", flat(conv_w), flat(conv_w_grad), flat(m["conv_w"]), flat(v["conv_w"]))
    grads["conv_w"] = conv_w_grad
    delta["conv_w"], new_m["conv_w"], new_v["conv_w"] = d_.reshape(cshape), m_.reshape(cshape), v_.reshape(cshape)

    order = ["norm1_g", "w_in", "sgu_ln_g", "sgu_ln_b", "sgu_w", "sgu_b", "conv_w", "conv_b", "conv_ln_g",
             "conv_ln_b", "w_out", "norm2_g", "w_ff1", "w_ff2", "final_g"]
    return (loss_out, grad_x, *[grads[n] for n in order], *[delta[n] for n in order],
            *[new_m[n] for n in order], *[new_v[n] for n in order])
```

```python
import functools

import jax
import jax.numpy as jnp
from jax import lax
from jax.experimental import pallas as pl
from jax.experimental.pallas import tpu as pltpu

F32 = jnp.float32
BF16 = jnp.bfloat16
MESH = pl.DeviceIdType.MESH

EPS = 1e-6
CHUNK = 128
CONV_TAPS = 31
HALO = 32
N_CHIPS = 4
N_DEV = 8
LANES = 128
SUBLANES = 8

ADAM_LR = 0.001
ADAM_B1 = 0.9
ADAM_B2 = 0.999
ADAM_EPS = 1e-08
ADAM_WD = 0.01
ADAM_STEP = 10

TM_MIX = 256
TM_FFN = 512
V7X_VMEM_BYTES = 64 * 2 ** 20
VMEM_LIMIT = V7X_VMEM_BYTES - 8 * 2 ** 20


def _params(sem=None):
    return pltpu.CompilerParams(dimension_semantics=sem, vmem_limit_bytes=VMEM_LIMIT)


def _gelu_and_grad(x):
    gauss = jnp.exp(-0.5 * x * x)
    t = 1.0 / (1.0 + (0.3275911 * 0.7071067811865476) * jnp.abs(x))
    poly = t * (0.254829592 + t * (-0.284496736 + t * (1.421413741 + t * (-1.453152027 + t * 1.061405429))))
    erf_abs = 1.0 - poly * gauss
    cdf = 0.5 * (1.0 + jnp.where(x < 0, -erf_abs, erf_abs))
    return x * cdf, cdf + x * (gauss * 0.3989422804014327)


def _sigmoid(x):
    return 1.0 / (1.0 + jnp.exp(-x))


def _dot(a, b):
    return jnp.dot(a.astype(BF16), b.astype(BF16), preferred_element_type=F32)


def _dot_nt(a, b):
    return lax.dot_general(a.astype(BF16), b.astype(BF16), (((1,), (1,)), ((), ())), preferred_element_type=F32)


def _rms_fwd(x, g):
    r = lax.rsqrt(jnp.mean(x * x, axis=-1, keepdims=True) + EPS)
    xh = x * r
    return r, xh, xh * g


def _rms_bwd(dh, xh, r, g):
    dxh = dh * g
    return r * (dxh - xh * jnp.mean(dxh * xh, axis=-1, keepdims=True))


def _ln_stats(x):
    mu = jnp.mean(x, axis=-1, keepdims=True)
    xc = x - mu
    rs = lax.rsqrt(jnp.mean(xc * xc, axis=-1, keepdims=True) + EPS)
    return xc * rs, rs


def _ln_bwd(dxh, xh, rs):
    return rs * (dxh - jnp.mean(dxh, axis=-1, keepdims=True) - xh * jnp.mean(dxh * xh, axis=-1, keepdims=True))


def _group_ln_fwd(c):
    parts, rss = [], []
    for j in range(c.shape[1] // CHUNK):
        xh, rs = _ln_stats(c[:, j * CHUNK:(j + 1) * CHUNK])
        parts.append(xh)
        rss.append(rs)
    return jnp.concatenate(parts, axis=1), rss


def _group_ln_bwd(dxh, xh, rss):
    parts = []
    for j, rs in enumerate(rss):
        cols = slice(j * CHUNK, (j + 1) * CHUNK)
        parts.append(_ln_bwd(dxh[:, cols], xh[:, cols], rs))
    return jnp.concatenate(parts, axis=1)


def _sublane_shifts(ext):
    n = ext.shape[0]
    return [ext if b == 0 else pltpu.roll(ext, n - b, 0) for b in range(SUBLANES)]


def _rows_from(shifts, off, tm):
    a, b = divmod(off, SUBLANES)
    return shifts[b][a * SUBLANES:a * SUBLANES + tm]


def _tril_mask():
    t = lax.broadcasted_iota(jnp.int32, (CHUNK, CHUNK), 0)
    s = lax.broadcasted_iota(jnp.int32, (CHUNK, CHUNK), 1)
    return t >= s


def _mix_forward(ua, va, vb, gb, g_halo, ws_ref, bfull_ref, lng, lnb, cw_ref, cb, cg, cbeta, mixed_scr):
    tm, da = ua.shape
    heads = da // CHUNK
    u, du_fac = _gelu_and_grad(ua)
    vg, dvg_fac = _gelu_and_grad(va)
    vh, v_rs = _ln_stats(vg)
    v = vh * lng + lnb
    mask = _tril_mask()
    wm = [jnp.where(mask, ws_ref[h], 0.0).astype(BF16) for h in range(heads)]
    vb16 = v.astype(BF16)
    for ci in range(tm // CHUNK):
        rows = slice(ci * CHUNK, (ci + 1) * CHUNK)
        for h in range(heads):
            cols = slice(h * CHUNK, (h + 1) * CHUNK)
            mixed_scr[rows, cols] = jnp.dot(wm[h], vb16[rows, cols], preferred_element_type=F32) + bfull_ref[h]
    mixed = mixed_scr[...]
    a = u * mixed

    sg = _sigmoid(gb)
    g = vb * sg
    g_shifts = _sublane_shifts(jnp.concatenate([g_halo, g], axis=0))
    cpre = jnp.zeros_like(g) + cb
    for k in range(CONV_TAPS):
        cpre = cpre + _rows_from(g_shifts, HALO - (CONV_TAPS - 1) + k, tm) * cw_ref[k:k + 1, :]
    chh, c_rss = _group_ln_fwd(cpre)
    cn = chh * cg + cbeta
    sc = _sigmoid(cn)
    cout = cn * sc
    return dict(u=u, du_fac=du_fac, dvg_fac=dvg_fac, vh=vh, v_rs=v_rs, v16=vb16, mixed=mixed, a=a, sg=sg, g=g,
                g_shifts=g_shifts, chh=chh, c_rss=c_rss, cn=cn, sc=sc, cout=cout)


ANY = pl.BlockSpec(memory_space=pl.ANY)


def _call(body, name, grid, in_specs, out_specs, out_shape, scratch_shapes, sem, operands, comm=None):
    if comm is None:
        res = pl.pallas_call(body, name=name, grid=grid, in_specs=in_specs, out_specs=out_specs, out_shape=out_shape,
                             scratch_shapes=scratch_shapes, compiler_params=_params(sem))(*operands)
        return res, []
    n_in, n_out, n_scr = len(in_specs), len(out_specs), len(scratch_shapes)
    n_cin, n_cout = len(comm["operands"]), len(comm["out_shape"])

    def fused(*refs):
        ins, refs = refs[:n_in], refs[n_in:]
        c_ins, refs = refs[:n_cin], refs[n_cin:]
        outs, refs = refs[:n_out], refs[n_out:]
        c_outs, refs = refs[:n_cout], refs[n_cout:]
        scr, c_sems = refs[:n_scr], refs[n_scr:]
        start, finish = comm["make"](c_ins, c_outs, c_sems)
        if grid:
            first = functools.reduce(jnp.logical_and, [pl.program_id(a) == 0 for a in range(len(grid))])
            last = functools.reduce(jnp.logical_and, [pl.program_id(a) == grid[a] - 1 for a in range(len(grid))])
            pl.when(first)(start)
            body(*ins, *outs, *scr)
            pl.when(last)(finish)
        else:
            start()
            body(*ins, *outs, *scr)
            finish()

    res = pl.pallas_call(
        fused, name=name, grid=grid,
        in_specs=list(in_specs) + [ANY] * n_cin, out_specs=list(out_specs) + [ANY] * n_cout,
        out_shape=list(out_shape) + list(comm["out_shape"]),
        input_output_aliases={n_in + a: n_out + b for a, b in comm["aliases"].items()},
        scratch_shapes=list(scratch_shapes) + list(comm["scratch"]),
        compiler_params=_params(sem),
    )(*operands, *comm["operands"])
    return res[:n_out], res[n_out:]


def _alone(name, comm):
    n_cin, n_cout = len(comm["operands"]), len(comm["out_shape"])

    def body(*refs):
        start, finish = comm["make"](refs[:n_cin], refs[n_cin:n_cin + n_cout], refs[n_cin + n_cout:])
        start()
        finish()

    return pl.pallas_call(
        body, name=name, in_specs=[ANY] * n_cin, out_specs=[ANY] * n_cout, out_shape=list(comm["out_shape"]),
        input_output_aliases=dict(comm["aliases"]), scratch_shapes=list(comm["scratch"]),
    )(*comm["operands"])


def _mix_fwd_call(l, x, g1, win_g, lng, lnb, ws, bfull, cw, cb, cg, cbeta, wout_g, comm=None):
    T, D = x.shape
    tm = min(TM_MIX, T)
    n_t = T // tm
    da = win_g.shape[-1]
    wo_rows = wout_g.shape[1]

    def body(x_ref, g1_ref, win_ref, lng_ref, lnb_ref, ws_ref, bf_ref, cw_ref, cb_ref, cg_ref, cbeta_ref, wout_ref,
             x1_ref, proj_ref, halo_scr, mixed_scr):
        i = pl.program_id(0)

        @pl.when(i == 0)
        def _():
            halo_scr[...] = jnp.zeros_like(halo_scr)

        x_t = x_ref[...]
        _, _, h1 = _rms_fwd(x_t, g1_ref[...])
        h1b = h1.astype(BF16)
        ps = []
        for p in range(N_CHIPS):
            pp = jnp.dot(h1b, win_ref[p], preferred_element_type=F32)
            proj_ref[:, p * da:(p + 1) * da] = pp
            ps.append(pp)
        f = _mix_forward(ps[0], ps[1], ps[2], ps[3], halo_scr[...], ws_ref, bf_ref, lng_ref[...], lnb_ref[...],
                         cw_ref, cb_ref[...], cg_ref[...], cbeta_ref[...], mixed_scr)
        halo_scr[...] = f["g"][tm - HALO:, :]
        mix = jnp.concatenate([f["a"], f["cout"]], axis=1).astype(BF16)
        acc = x_t
        for q in range(N_CHIPS):
            acc = acc + jnp.dot(mix[:, q * wo_rows:(q + 1) * wo_rows], wout_ref[q], preferred_element_type=F32)
        x1_ref[...] = acc

    full = lambda a: pl.BlockSpec(a.shape, lambda i: (0,) * a.ndim)
    return _call(
        body, f"mix_fwd_l{l}", (n_t,),
        in_specs=[
            pl.BlockSpec((tm, D), lambda i: (i, 0)),
            full(g1),
            pl.BlockSpec((N_CHIPS, D, da), lambda i: (0, 0, 0)),
            full(lng), full(lnb), full(ws), full(bfull), full(cw), full(cb), full(cg), full(cbeta),
            pl.BlockSpec((N_CHIPS, wo_rows, D), lambda i: (0, 0, 0)),
        ],
        out_specs=[
            pl.BlockSpec((tm, D), lambda i: (i, 0)),
            pl.BlockSpec((tm, N_CHIPS * da), lambda i: (i, 0)),
        ],
        out_shape=[jax.ShapeDtypeStruct((T, D), F32), jax.ShapeDtypeStruct((T, N_CHIPS * da), F32)],
        scratch_shapes=[pltpu.VMEM((HALO, da), F32), pltpu.VMEM((tm, da), F32)],
        sem=("arbitrary",),
        operands=(x, g1, win_g, lng, lnb, ws, bfull, cw, cb, cg, cbeta, wout_g), comm=comm)


def _ffn_tile(T):
    return min(TM_FFN, max(T // 2, CHUNK))


def _ffn_fwd_call(l, x1, g2, w1_g, w2_g, comm=None):
    T, D = x1.shape
    tm = _ffn_tile(T)
    n_t = T // tm
    ffb = w1_g.shape[-1]

    def body(x1_ref, g2_ref, w1_ref, w2_ref, x2_ref, z_ref, h2_scr, acc_scr):
        p = pl.program_id(1)

        @pl.when(p == 0)
        def _():
            _, _, h2 = _rms_fwd(x1_ref[...], g2_ref[...])
            h2_scr[...] = h2.astype(BF16)
            acc_scr[...] = jnp.zeros_like(acc_scr)

        z = jnp.dot(h2_scr[...], w1_ref[...], preferred_element_type=F32)
        z_ref[...] = z.astype(BF16)
        rz = jnp.maximum(z, 0.0)
        acc_scr[...] += jnp.dot((rz * rz).astype(BF16), w2_ref[...], preferred_element_type=F32)

        @pl.when(p == N_CHIPS - 1)
        def _():
            x2_ref[...] = x1_ref[...] + acc_scr[...]

    return _call(
        body, f"ffn_fwd_l{l}", (n_t, N_CHIPS),
        in_specs=[
            pl.BlockSpec((tm, D), lambda i, p: (i, 0)),
            pl.BlockSpec(g2.shape, lambda i, p: (0, 0)),
            pl.BlockSpec((None, D, ffb), lambda i, p: (p, 0, 0)),
            pl.BlockSpec((None, ffb, D), lambda i, p: (p, 0, 0)),
        ],
        out_specs=[
            pl.BlockSpec((tm, D), lambda i, p: (i, 0)),
            pl.BlockSpec((tm, ffb), lambda i, p: (i, p)),
        ],
        out_shape=[jax.ShapeDtypeStruct((T, D), F32), jax.ShapeDtypeStruct((T, N_CHIPS * ffb), BF16)],
        scratch_shapes=[pltpu.VMEM((tm, D), BF16), pltpu.VMEM((tm, D), F32)],
        sem=("arbitrary", "arbitrary"), operands=(x1, g2, w1_g, w2_g), comm=comm)


def _loss_call(x, gf, target):
    T, D = x.shape
    tm = min(TM_FFN, T)
    n_t = T // tm

    def body(x_ref, gf_ref, t_ref, dx_ref, dxb_ref, loss_ref, dgf_ref):
        i = pl.program_id(0)

        @pl.when(i == 0)
        def _():
            loss_ref[...] = jnp.zeros_like(loss_ref)
            dgf_ref[...] = jnp.zeros_like(dgf_ref)

        g = gf_ref[...]
        r, xh, y = _rms_fwd(x_ref[...], g)
        e = y - t_ref[...]
        per_tok = jnp.sum(e * e, axis=-1, keepdims=True) * (1.0 / D)
        loss_ref[...] += 0.5 * jnp.sum(per_tok, axis=0, keepdims=True)
        dy = e * (1.0 / D)
        dgf_ref[...] += jnp.sum(dy * xh, axis=0, keepdims=True)
        dx = _rms_bwd(dy, xh, r, g)
        dx_ref[...] = dx
        dxb_ref[...] = dx.astype(BF16)

    return pl.pallas_call(
        body, name="loss_head",
        grid=(n_t,),
        in_specs=[
            pl.BlockSpec((tm, D), lambda i: (i, 0)),
            pl.BlockSpec(gf.shape, lambda i: (0, 0)),
            pl.BlockSpec((tm, D), lambda i: (i, 0)),
        ],
        out_specs=[
            pl.BlockSpec((tm, D), lambda i: (i, 0)),
            pl.BlockSpec((tm, D), lambda i: (i, 0)),
            pl.BlockSpec((1, 1), lambda i: (0, 0)),
            pl.BlockSpec((1, D), lambda i: (0, 0)),
        ],
        out_shape=[jax.ShapeDtypeStruct((T, D), F32), jax.ShapeDtypeStruct((T, D), BF16),
                   jax.ShapeDtypeStruct((1, 1), F32), jax.ShapeDtypeStruct((1, D), F32)],
        compiler_params=_params(("arbitrary",)),
    )(x, gf, target)


def _ffn_bwd_call(l, dx2, dx2b, x1, z, g2, w1_g, w2_g, comm=None):
    T, D = x1.shape
    tm = _ffn_tile(T)
    n_t = T // tm
    ffb = w1_g.shape[-1]

    def body(dx2_ref, dx2b_ref, x1_ref, z_ref, g2_ref, w1_ref, w2_ref,
             dx1_ref, dx1b_ref, dz_ref, ft_ref, h2t_ref, dg2_ref, acc_scr):
        i = pl.program_id(0)
        p = pl.program_id(1)

        @pl.when(jnp.logical_and(i == 0, p == 0))
        def _():
            dg2_ref[...] = jnp.zeros_like(dg2_ref)

        @pl.when(p == 0)
        def _():
            _, _, h2 = _rms_fwd(x1_ref[...], g2_ref[...])
            h2t_ref[...] = h2.T.astype(BF16)
            acc_scr[...] = jnp.zeros_like(acc_scr)

        rz = jnp.maximum(z_ref[...].astype(F32), 0.0)
        ft_ref[...] = (rz * rz).T.astype(BF16)
        df = _dot_nt(dx2b_ref[...], w2_ref[...])
        dz = (df * (2.0 * rz)).astype(BF16)
        dz_ref[...] = dz
        acc_scr[...] += _dot_nt(dz, w1_ref[...])

        @pl.when(p == N_CHIPS - 1)
        def _():
            g = g2_ref[...]
            r, xh, _ = _rms_fwd(x1_ref[...], g)
            dh2 = acc_scr[...]
            dg2_ref[...] += jnp.sum(dh2 * xh, axis=0, keepdims=True)
            dx1 = dx2_ref[...] + _rms_bwd(dh2, xh, r, g)
            dx1_ref[...] = dx1
            dx1b_ref[...] = dx1.astype(BF16)

    return _call(
        body, f"ffn_bwd_l{l}", (n_t, N_CHIPS),
        in_specs=[
            pl.BlockSpec((tm, D), lambda i, p: (i, 0)),
            pl.BlockSpec((tm, D), lambda i, p: (i, 0)),
            pl.BlockSpec((tm, D), lambda i, p: (i, 0)),
            pl.BlockSpec((tm, ffb), lambda i, p: (i, p)),
            pl.BlockSpec(g2.shape, lambda i, p: (0, 0)),
            pl.BlockSpec((None, D, ffb), lambda i, p: (p, 0, 0)),
            pl.BlockSpec((None, ffb, D), lambda i, p: (p, 0, 0)),
        ],
        out_specs=[
            pl.BlockSpec((tm, D), lambda i, p: (i, 0)),
            pl.BlockSpec((tm, D), lambda i, p: (i, 0)),
            pl.BlockSpec((tm, ffb), lambda i, p: (i, p)),
            pl.BlockSpec((ffb, tm), lambda i, p: (p, i)),
            pl.BlockSpec((D, tm), lambda i, p: (0, i)),
            pl.BlockSpec((1, D), lambda i, p: (0, 0)),
        ],
        out_shape=[jax.ShapeDtypeStruct((T, D), F32), jax.ShapeDtypeStruct((T, D), BF16),
                   jax.ShapeDtypeStruct((T, N_CHIPS * ffb), BF16), jax.ShapeDtypeStruct((N_CHIPS * ffb, T), BF16),
                   jax.ShapeDtypeStruct((D, T), BF16), jax.ShapeDtypeStruct((1, D), F32)],
        scratch_shapes=[pltpu.VMEM((tm, D), F32)],
        sem=("arbitrary", "arbitrary"), operands=(dx2, dx2b, x1, z, g2, w1_g, w2_g), comm=comm)


def _mix_bwd_call(l, dx1, dx1b, x, proj, g1, win_g, lng, lnb, ws, wst, bfull, cw, cb, cg, cbeta, wout_g, comm=None):
    T, D = x.shape
    tm = min(TM_MIX, T)
    n_t = T // tm
    da = win_g.shape[-1]
    heads = da // CHUNK
    wo_rows = wout_g.shape[1]
    halo_blocks = tm // HALO

    def body(dx1_ref, dx1b_ref, x_ref, proj_ref, vbh_ref, gbh_ref, g1_ref, win_ref, lng_ref, lnb_ref, ws_ref, wst_ref,
             bf_ref, cw_ref, cb_ref, cg_ref, cbeta_ref, wout_ref,
             dx_ref, dxb_ref, dproj_ref, h1t_ref, mixt_ref,
             dg1_ref, dlng_ref, dlnb_ref, dws_ref, dbs_ref, dcw_ref, dcb_ref, dcg_ref, dcbeta_ref,
             carry_scr, mixed_scr, dv_scr):
        i = pl.program_id(0)
        tile = n_t - 1 - i

        @pl.when(i == 0)
        def _():
            carry_scr[...] = jnp.zeros_like(carry_scr)
            for ref in (dg1_ref, dlng_ref, dlnb_ref, dws_ref, dbs_ref, dcw_ref, dcb_ref, dcg_ref, dcbeta_ref):
                ref[...] = jnp.zeros_like(ref)

        g1v = g1_ref[...]
        r, xh, h1 = _rms_fwd(x_ref[...], g1v)
        h1t_ref[...] = h1.T.astype(BF16)

        ua = proj_ref[:, 0 * da:1 * da]
        va = proj_ref[:, 1 * da:2 * da]
        vb = proj_ref[:, 2 * da:3 * da]
        gb = proj_ref[:, 3 * da:4 * da]
        g_halo = jnp.where(tile > 0, vbh_ref[...] * _sigmoid(gbh_ref[...]), 0.0)
        lng_v, cg_v = lng_ref[...], cg_ref[...]
        f = _mix_forward(ua, va, vb, gb, g_halo, ws_ref, bf_ref, lng_v, lnb_ref[...], cw_ref, cb_ref[...], cg_v,
                         cbeta_ref[...], mixed_scr)
        mixt_ref[0:da, :] = f["a"].T.astype(BF16)
        mixt_ref[da:2 * da, :] = f["cout"].T.astype(BF16)

        dxo = dx1b_ref[...]
        dmix = jnp.concatenate([_dot_nt(dxo, wout_ref[q]) for q in range(N_CHIPS)], axis=1)
        da_ = dmix[:, :da]
        dc_ = dmix[:, da:]

        dua = da_ * f["mixed"] * f["du_fac"]
        dmixed = (da_ * f["u"]).astype(BF16)
        mask_t = (lax.broadcasted_iota(jnp.int32, (CHUNK, CHUNK), 1)
                  >= lax.broadcasted_iota(jnp.int32, (CHUNK, CHUNK), 0))
        wmt =[jnp.where(mask_t, wst_ref[h], 0.0).astype(BF16) for h in range(heads)]
        mask = _tril_mask()
        v16 = f["v16"]
        for h in range(heads):
            cols = slice(h * CHUNK, (h + 1) * CHUNK)
            dws_h = jnp.zeros((CHUNK, CHUNK), F32)
            dbs_h = jnp.zeros((CHUNK, CHUNK), F32)
            for ci in range(tm // CHUNK):
                rows = slice(ci * CHUNK, (ci + 1) * CHUNK)
                dm = dmixed[rows, cols]
                dv_scr[rows, cols] = jnp.dot(wmt[h], dm, preferred_element_type=F32)
                dws_h = dws_h + _dot_nt(dm, v16[rows, cols])
                dbs_h = dbs_h + dm.astype(F32)
            dws_ref[h] += jnp.where(mask, dws_h, 0.0)
            dbs_ref[h] += jnp.broadcast_to(jnp.sum(dbs_h, axis=1, keepdims=True), (CHUNK, CHUNK))
        dv = dv_scr[...]
        dlng_ref[...] += jnp.sum(dv * f["vh"], axis=0, keepdims=True)
        dlnb_ref[...] += jnp.sum(dv, axis=0, keepdims=True)
        dva = _ln_bwd(dv * lng_v, f["vh"], f["v_rs"]) * f["dvg_fac"]

        cn, sc = f["cn"], f["sc"]
        dcn = dc_ * (sc * (1.0 + cn * (1.0 - sc)))
        dcg_ref[...] += jnp.sum(dcn * f["chh"], axis=0, keepdims=True)
        dcbeta_ref[...] += jnp.sum(dcn, axis=0, keepdims=True)
        dcpre = _group_ln_bwd(dcn * cg_v, f["chh"], f["c_rss"])
        dcb_ref[...] += jnp.sum(dcpre, axis=0, keepdims=True)
        d_shifts = _sublane_shifts(jnp.concatenate([dcpre, carry_scr[...]], axis=0))
        dg = jnp.zeros_like(dcpre)
        for k in range(CONV_TAPS):
            g_k = _rows_from(f["g_shifts"], HALO - (CONV_TAPS - 1) + k, tm)
            dcw_ref[k:k + 1, :] += jnp.sum(dcpre * g_k, axis=0, keepdims=True)
            dg = dg + _rows_from(d_shifts, CONV_TAPS - 1 - k, tm) * cw_ref[k:k + 1, :]
        carry_scr[...] = dcpre[:HALO, :]
        sg = f["sg"]
        dvb = dg * sg
        dgb = dg * vb * sg * (1.0 - sg)

        dps = [dua.astype(BF16), dva.astype(BF16), dvb.astype(BF16), dgb.astype(BF16)]
        dh1 = jnp.zeros((tm, D), F32)
        for p in range(N_CHIPS):
            dproj_ref[:, p * da:(p + 1) * da] = dps[p]
            dh1 = dh1 + _dot_nt(dps[p], win_ref[p])
        dg1_ref[...] += jnp.sum(dh1 * xh, axis=0, keepdims=True)
        dx = dx1_ref[...] + _rms_bwd(dh1, xh, r, g1v)
        dx_ref[...] = dx
        dxb_ref[...] = dx.astype(BF16)

    rev = lambda i: (n_t - 1 - i, 0)
    full = lambda a: pl.BlockSpec(a.shape, lambda i: (0,) * a.ndim)
    acc = lambda shape: pl.BlockSpec(shape, lambda i: (0,) * len(shape))
    halo_idx = lambda col: (lambda i: (jnp.maximum((n_t - 1 - i) * halo_blocks - 1, 0), col))
    small_shapes = [(1, D), (1, da), (1, da), (heads, CHUNK, CHUNK), (heads, CHUNK, CHUNK), (HALO, da),
                    (1, da), (1, da), (1, da)]
    return _call(
        body, f"mix_bwd_l{l}", (n_t,),
        in_specs=[
            pl.BlockSpec((tm, D), rev),
            pl.BlockSpec((tm, D), rev),
            pl.BlockSpec((tm, D), rev),
            pl.BlockSpec((tm, N_CHIPS * da), rev),
            pl.BlockSpec((HALO, da), halo_idx(2)),
            pl.BlockSpec((HALO, da), halo_idx(3)),
            full(g1),
            pl.BlockSpec((N_CHIPS, D, da), lambda i: (0, 0, 0)),
            full(lng), full(lnb), full(ws), full(wst), full(bfull), full(cw), full(cb), full(cg), full(cbeta),
            pl.BlockSpec((N_CHIPS, wo_rows, D), lambda i: (0, 0, 0)),
        ],
        out_specs=[
            pl.BlockSpec((tm, D), rev),
            pl.BlockSpec((tm, D), rev),
            pl.BlockSpec((tm, N_CHIPS * da), rev),
            pl.BlockSpec((D, tm), lambda i: (0, n_t - 1 - i)),
            pl.BlockSpec((2 * da, tm), lambda i: (0, n_t - 1 - i)),
        ] + [acc(s) for s in small_shapes],
        out_shape=[jax.ShapeDtypeStruct((T, D), F32), jax.ShapeDtypeStruct((T, D), BF16),
                   jax.ShapeDtypeStruct((T, N_CHIPS * da), BF16), jax.ShapeDtypeStruct((D, T), BF16),
                   jax.ShapeDtypeStruct((2 * da, T), BF16)] + [jax.ShapeDtypeStruct(s, F32) for s in small_shapes],
        scratch_shapes=[pltpu.VMEM((HALO, da), F32), pltpu.VMEM((tm, da), F32), pltpu.VMEM((tm, da), F32)],
        sem=("arbitrary",),
        operands=(dx1, dx1b, x, proj, proj, proj, g1, win_g, lng, lnb, ws, wst, bfull, cw, cb, cg, cbeta, wout_g),
        comm=comm)


def _wgrad_call(name, at, b, split_cols, steps):
    M, T = at.shape
    N = b.shape[1]
    if split_cols:
        bn = N // (N_CHIPS * steps)
        in_specs = [pl.BlockSpec((M, T), lambda j: (0, 0)), pl.BlockSpec((T, bn), lambda j: (0, j))]
        out_spec = pl.BlockSpec((None, M, bn), lambda j: (j // steps, 0, j % steps))
        out_shape = jax.ShapeDtypeStruct((N_CHIPS, M, N // N_CHIPS), BF16)
    else:
        bm = M // (N_CHIPS * steps)
        in_specs = [pl.BlockSpec((bm, T), lambda j: (j, 0)), pl.BlockSpec((T, N), lambda j: (0, 0))]
        out_spec = pl.BlockSpec((None, bm, N), lambda j: (j // steps, j % steps, 0))
        out_shape = jax.ShapeDtypeStruct((N_CHIPS, M // N_CHIPS, N), BF16)

    def body(at_ref, b_ref, o_ref):
        o_ref[...] = jnp.dot(at_ref[...], b_ref[...], preferred_element_type=F32).astype(BF16)

    return pl.pallas_call(
        body, name=name, grid=(N_CHIPS * steps,), in_specs=in_specs, out_specs=out_spec, out_shape=out_shape,
        compiler_params=_params(("arbitrary",)),
    )(at, b)


def _rows_block(rows, cols):
    br = rows
    while br * cols * 4 > 2 ** 20 and br % 16 == 0:
        br //= 2
    return br


def _cast_own_call(name, chip, w, l):
    _, rows, cols = w.shape
    br = _rows_block(rows, cols)
    n_b = rows // br

    def body(chip_ref, w_ref, o_ref):
        o_ref[...] = w_ref[...].astype(BF16)

    return pl.pallas_call(
        body, name=name,
        grid_spec=pltpu.PrefetchScalarGridSpec(
            num_scalar_prefetch=1, grid=(n_b,),
            in_specs=[pl.BlockSpec((None, br, cols), lambda i, chip_ref: (l, i, 0))],
            out_specs=pl.BlockSpec((None, br, cols), lambda i, chip_ref: (chip_ref[0], i, 0))),
        out_shape=jax.ShapeDtypeStruct((N_CHIPS, rows, cols), BF16),
        compiler_params=_params(("arbitrary",)),
    )(chip, w)


def _pair_sum_call(name, core, g, got):
    _, rh, cols = got.shape
    br = _rows_block(rh, cols)
    n_b = rh // br

    def body(core_ref, g_ref, got_ref, o_ref):
        o_ref[...] = (g_ref[...].astype(F32) + got_ref[...].astype(F32)).astype(BF16)

    half = pl.BlockSpec((None, br, cols), lambda q, i, core_ref: (q, i, 0))
    return pl.pallas_call(
        body, name=name,
        grid_spec=pltpu.PrefetchScalarGridSpec(
            num_scalar_prefetch=1, grid=(N_CHIPS, n_b),
            in_specs=[pl.BlockSpec((None, br, cols), lambda q, i, core_ref: (q, core_ref[0] * n_b + i, 0)), half],
            out_specs=half),
        out_shape=jax.ShapeDtypeStruct(got.shape, BF16),
        compiler_params=_params(("arbitrary", "arbitrary")),
    )(core, g, got)


def _chip_sum_call(name, core, parts):
    _, rh, cols = parts.shape
    br = _rows_block(rh, cols)
    n_b = rh // br

    def body(core_ref, *refs):
        o_ref = refs[N_CHIPS]
        total = refs[0][...].astype(F32)
        for q in range(1, N_CHIPS):
            total = total + refs[q][...].astype(F32)
        o_ref[...] = total

    return pl.pallas_call(
        body, name=name,
        grid_spec=pltpu.PrefetchScalarGridSpec(
            num_scalar_prefetch=1, grid=(n_b,),
            in_specs=[pl.BlockSpec((None, br, cols), functools.partial(lambda i, core_ref, q: (q, i, 0), q=q))
                      for q in range(N_CHIPS)],
            out_specs=pl.BlockSpec((br, cols), lambda i, core_ref: (core_ref[0] * n_b + i, 0))),
        out_shape=jax.ShapeDtypeStruct((2 * rh, cols), F32),
        compiler_params=_params(("arbitrary",)),
    )(core, *([parts] * N_CHIPS))


def _adamw_layers_call(name, w, g0, g1, m, v):
    _, rows, cols = w.shape
    br = _rows_block(rows, cols)

    def body(w_ref, g0_ref, g1_ref, m_ref, v_ref, g_ref, d_ref, nm_ref, nv_ref):
        gv = jnp.where(pl.program_id(0) == 0, g0_ref[...], g1_ref[...])
        g_ref[...] = gv
        m_new = ADAM_B1 * m_ref[...] + (1.0 - ADAM_B1) * gv
        v_new = ADAM_B2 * v_ref[...] + (1.0 - ADAM_B2) * (gv * gv)
        m_hat = m_new / (1.0 - ADAM_B1 ** ADAM_STEP)
        v_hat = v_new / (1.0 - ADAM_B2 ** ADAM_STEP)
        d_ref[...] = -ADAM_LR * (m_hat / (jnp.sqrt(v_hat) + ADAM_EPS) + ADAM_WD * w_ref[...])
        nm_ref[...] = m_new
        nv_ref[...] = v_new

    both = pl.BlockSpec((None, br, cols), lambda l, i: (l, i, 0))
    one = pl.BlockSpec((br, cols), lambda l, i: (i, 0))
    return pl.pallas_call(
        body, name=name, grid=(2, rows // br), in_specs=[both, one, one, both, both], out_specs=[both] * 4,
        out_shape=[jax.ShapeDtypeStruct(w.shape, F32)] * 4,
        compiler_params=_params(("arbitrary", "arbitrary")),
    )(w, g0, g1, m, v)


def _adamw_call(name, w, g, m, v):
    rows, cols = w.shape
    br = _rows_block(rows, cols)

    def body(w_ref, g_ref, m_ref, v_ref, d_ref, nm_ref, nv_ref):
        gv = g_ref[...]
        m_new = ADAM_B1 * m_ref[...] + (1.0 - ADAM_B1) * gv
        v_new = ADAM_B2 * v_ref[...] + (1.0 - ADAM_B2) * (gv * gv)
        m_hat = m_new / (1.0 - ADAM_B1 ** ADAM_STEP)
        v_hat = v_new / (1.0 - ADAM_B2 ** ADAM_STEP)
        d_ref[...] = -ADAM_LR * (m_hat / (jnp.sqrt(v_hat) + ADAM_EPS) + ADAM_WD * w_ref[...])
        nm_ref[...] = m_new
        nv_ref[...] = v_new

    spec = pl.BlockSpec((br, cols), lambda i: (i, 0))
    return pl.pallas_call(
        body, name=name, grid=(rows // br,), in_specs=[spec] * 4, out_specs=[spec] * 3,
        out_shape=[jax.ShapeDtypeStruct((rows, cols), F32)] * 3,
        compiler_params=_params(("arbitrary",)),
    )(w, g, m, v)


def _place():
    x, y, c = lax.axis_index("x"), lax.axis_index("y"), lax.axis_index("c")
    chips = [(1 - x, y), (x, 1 - y), (1 - x, 1 - y)]
    return x, y, c, 2 * x + y, chips


def _half_rows(ref, core):
    rh = ref.shape[-2] // 2
    rows = pl.ds(pl.multiple_of(core * rh, rh), rh)
    return ref.at[rows] if len(ref.shape) == 2 else ref.at[:, rows]


def _gather_comm(bufs, cw_shard=None):
    n = len(bufs)
    with_cw = cw_shard is not None

    def make(c_ins, c_outs, sems):
        ins, outs = c_ins[:n], c_outs[:n]
        send_sems, recv_sems, cw_send, cw_recv, cw_local = sems
        x, y, c, p, chips = _place()
        sibling = (x, y, 1 - c)
        qs = [2 * cx + cy for cx, cy in chips]
        hops = len(chips)

        def remote(k, j, src, dst, to):
            return pltpu.make_async_remote_copy(src_ref=src, dst_ref=dst, send_sem=send_sems.at[k, j],
                                                recv_sem=recv_sems.at[k, j], device_id=to, device_id_type=MESH)

        def cw_copy(j, dst, to):
            return pltpu.make_async_remote_copy(src_ref=c_ins[n], dst_ref=dst, send_sem=cw_send.at[j],
                                                recv_sem=cw_recv.at[j], device_id=to, device_id_type=MESH)

        def over_ici():
            return [remote(k, j, _half_rows(ins[k].at[p], c), _half_rows(outs[k].at[p], c), (*chip, c))
                    for j, chip in enumerate(chips) for k in range(n)]

        def passed_on():
            return [remote(k, hops + j, _half_rows(outs[k].at[qs[j]], c), _half_rows(outs[k].at[qs[j]], c), sibling)
                    for j in range(hops) for k in range(n)]

        def cw_copies():
            return [cw_copy(j, c_outs[n].at[p], (*chip, c)) for j, chip in enumerate(chips)]

        def cw_own():
            return pltpu.make_async_copy(c_ins[n], c_outs[n].at[p], cw_local)

        def start():
            for cp in over_ici():
                cp.start()
            if with_cw:
                cw_own().start()
                for cp in cw_copies():
                    cp.start()

        def finish():
            for j in range(hops):
                for k in range(n):
                    landed = _half_rows(outs[k].at[qs[j]], c)
                    remote(k, j, landed, landed, sibling).wait_recv()
                    remote(k, hops + j, landed, landed, sibling).start()
            for j in range(hops):
                for k in range(n):
                    other = _half_rows(outs[k].at[qs[j]], 1 - c)
                    remote(k, hops + j, other, other, sibling).wait_recv()
            if with_cw:
                for j in range(hops):
                    cw_copy(j, c_outs[n].at[qs[j]], sibling).wait_recv()
                cw_own().wait()
                for cp in cw_copies():
                    cp.wait_send()
            for cp in over_ici() + passed_on():
                cp.wait_send()

        return start, finish

    out_shape = [jax.ShapeDtypeStruct(b.shape, b.dtype) for b in bufs]
    if with_cw:
        out_shape.append(jax.ShapeDtypeStruct((N_CHIPS,) + cw_shard.shape, cw_shard.dtype))
    return dict(operands=list(bufs) + ([cw_shard] if with_cw else []), out_shape=out_shape,
                aliases={k: k for k in range(n)}, make=make,
                scratch=[pltpu.SemaphoreType.DMA((n, 6)), pltpu.SemaphoreType.DMA((n, 6)),
                         pltpu.SemaphoreType.DMA((3,)), pltpu.SemaphoreType.DMA((3,)), pltpu.SemaphoreType.DMA])


def _pair_exchange_call(l, gs):
    n = len(gs)

    def body(*refs):
        g, got = refs[:n], refs[n:2 * n]
        send_sems, recv_sems = refs[2 * n:]
        x, y, c, _, _ = _place()
        sibling = (x, y, 1 - c)
        copies = [pltpu.make_async_remote_copy(src_ref=_half_rows(g[k], 1 - c), dst_ref=got[k],
                                               send_sem=send_sems.at[k], recv_sem=recv_sems.at[k],
                                               device_id=sibling, device_id_type=MESH) for k in range(n)]
        for cp in copies:
            cp.start()
        for cp in copies:
            cp.wait_send()
            cp.wait_recv()

    return pl.pallas_call(
        body, name=f"grad_pair_exchange_l{l}",
        in_specs=[ANY] * n, out_specs=[ANY] * n,
        out_shape=[jax.ShapeDtypeStruct((g.shape[0], g.shape[1] // 2, g.shape[2]), g.dtype) for g in gs],
        scratch_shapes=[pltpu.SemaphoreType.DMA((n,)), pltpu.SemaphoreType.DMA((n,))],
    )(*gs)


def _exchange_comm(sums):
    n = len(sums)

    def make(ins, outs, sems):
        send_sems, recv_sems, local_sems = sems
        x, y, c, p, chips = _place()
        qs = [2 * cx + cy for cx, cy in chips]

        def remote(k, j, src, dst, to):
            return pltpu.make_async_remote_copy(src_ref=src, dst_ref=dst, send_sem=send_sems.at[k, j],
                                                recv_sem=recv_sems.at[k, j], device_id=to, device_id_type=MESH)

        def own():
            return [pltpu.make_async_copy(ins[k].at[p], outs[k].at[p], local_sems.at[k]) for k in range(n)]

        def sent():
            return [remote(k, j, ins[k].at[qs[j]], outs[k].at[p], (*chip, c))
                    for j, chip in enumerate(chips) for k in range(n)]

        def start():
            for cp in own() + sent():
                cp.start()

        def finish():
            for j in range(len(chips)):
                for k in range(n):
                    remote(k, j, ins[k].at[qs[j]], outs[k].at[qs[j]], (x, y, c)).wait_recv()
            for cp in sent():
                cp.wait_send()
            for cp in own():
                cp.wait()

        return start, finish

    return dict(operands=list(sums), out_shape=[jax.ShapeDtypeStruct(s.shape, s.dtype) for s in sums], aliases={},
                make=make, scratch=[pltpu.SemaphoreType.DMA((n, 3)), pltpu.SemaphoreType.DMA((n, 3)),
                                    pltpu.SemaphoreType.DMA((n,))])


def _pair_gather_call(l, halves):
    n = len(halves)

    def body(*refs):
        ins, outs = refs[:n], refs[n:2 * n]
        send_sems, recv_sems = refs[2 * n:]
        x, y, c, _, _ = _place()
        sibling = (x, y, 1 - c)

        def remote(k, src, dst):
            return pltpu.make_async_remote_copy(src_ref=src, dst_ref=dst, send_sem=send_sems.at[k],
                                                recv_sem=recv_sems.at[k], device_id=sibling, device_id_type=MESH)

        sent = [remote(k, _half_rows(ins[k], c), _half_rows(outs[k], c)) for k in range(n)]
        for cp in sent:
            cp.start()
        for k in range(n):
            other = _half_rows(outs[k], 1 - c)
            remote(k, other, other).wait_recv()
        for cp in sent:
            cp.wait_send()

    return pl.pallas_call(
        body, name=f"grad_pair_gather_l{l}",
        in_specs=[ANY] * n, out_specs=[ANY] * n,
        out_shape=[jax.ShapeDtypeStruct(h.shape, h.dtype) for h in halves],
        input_output_aliases={k: k for k in range(n)},
        scratch_shapes=[pltpu.SemaphoreType.DMA((n,)), pltpu.SemaphoreType.DMA((n,))],
    )(*halves)


def _all_sum_small_call(block, comm=None):
    m_per, n = block.shape

    def body(x_ref, sum_ref, all_ref, send_sems, recv_sems, local_sem):
        x, y, c, _, chip_list = _place()
        me, sibling = (x, y, c), (x, y, 1 - c)

        def rows(px, py, pc):
            return all_ref.at[pl.ds((4 * px + 2 * py + pc) * m_per, m_per), :]

        def copy(k, blk, to, src=None):
            return pltpu.make_async_remote_copy(src_ref=rows(*blk) if src is None else src, dst_ref=rows(*blk),
                                                send_sem=send_sems.at[k], recv_sem=recv_sems.at[k],
                                                device_id=to, device_id_type=MESH)

        mine = pltpu.make_async_copy(x_ref, rows(*me), local_sem)
        mine.start()
        first = [copy(0, me, sibling, src=x_ref)]
        first += [copy(1 + j, me, (*chip, c), src=x_ref) for j, chip in enumerate(chip_list)]
        for cp in first:
            cp.start()
        passed = [copy(4 + j, (*chip, c), sibling) for j, chip in enumerate(chip_list)]
        for j, chip in enumerate(chip_list):
            copy(1 + j, (*chip, c), me).wait_recv()
            passed[j].start()
        copy(0, sibling, me).wait_recv()
        for j, chip in enumerate(chip_list):
            copy(4 + j, (*chip, 1 - c), me).wait_recv()
        for cp in first + passed:
            cp.wait_send()
        mine.wait()
        total = all_ref[0:m_per, :]
        for d in range(1, N_DEV):
            total = total + all_ref[d * m_per:(d + 1) * m_per, :]
        sum_ref[...] = total

    vmem = pl.BlockSpec(memory_space=pltpu.VMEM)
    res, c_res = _call(
        body, "small_all_sum", (), in_specs=[vmem], out_specs=[vmem, vmem],
        out_shape=[jax.ShapeDtypeStruct((m_per, n), F32), jax.ShapeDtypeStruct((N_DEV * m_per, n), F32)],
        scratch_shapes=[pltpu.SemaphoreType.DMA((7,)), pltpu.SemaphoreType.DMA((7,)), pltpu.SemaphoreType.DMA],
        sem=None, operands=(block,), comm=comm)
    return res[0], c_res


SMALL_NAMES = ["norm1_g", "sgu_ln_g", "sgu_ln_b", "sgu_w", "sgu_b", "conv_b", "conv_ln_g", "conv_ln_b", "norm2_g",
               "final_g"]
MIX_NAMES = ["w_in", "w_out"]
FFN_NAMES = ["w_ff1", "w_ff2"]
BIG_NAMES = MIX_NAMES + FFN_NAMES


def _rows128(a):
    return a.reshape(-1, LANES)


def kernel(x, norm1_g, w_in, sgu_ln_g, sgu_ln_b, sgu_w, sgu_b, conv_w, conv_b, conv_ln_g, conv_ln_b, w_out, norm2_g, w_ff1, w_ff2, final_g, loss_target, m_norm1_g, m_w_in, m_sgu_ln_g, m_sgu_ln_b, m_sgu_w, m_sgu_b, m_conv_w, m_conv_b, m_conv_ln_g, m_conv_ln_b, m_w_out, m_norm2_g, m_w_ff1, m_w_ff2, m_final_g, v_norm1_g, v_w_in, v_sgu_ln_g, v_sgu_ln_b, v_sgu_w, v_sgu_b, v_conv_w, v_conv_b, v_conv_ln_g, v_conv_ln_b, v_w_out, v_norm2_g, v_w_ff1, v_w_ff2, v_final_g):
    w = dict(norm1_g=norm1_g, w_in=w_in, sgu_ln_g=sgu_ln_g, sgu_ln_b=sgu_ln_b, sgu_w=sgu_w, sgu_b=sgu_b,
             conv_w=conv_w, conv_b=conv_b, conv_ln_g=conv_ln_g, conv_ln_b=conv_ln_b, w_out=w_out, norm2_g=norm2_g,
             w_ff1=w_ff1, w_ff2=w_ff2, final_g=final_g)
    m = dict(norm1_g=m_norm1_g, w_in=m_w_in, sgu_ln_g=m_sgu_ln_g, sgu_ln_b=m_sgu_ln_b, sgu_w=m_sgu_w, sgu_b=m_sgu_b,
             conv_w=m_conv_w, conv_b=m_conv_b, conv_ln_g=m_conv_ln_g, conv_ln_b=m_conv_ln_b, w_out=m_w_out,
             norm2_g=m_norm2_g, w_ff1=m_w_ff1, w_ff2=m_w_ff2, final_g=m_final_g)
    v = dict(norm1_g=v_norm1_g, w_in=v_w_in, sgu_ln_g=v_sgu_ln_g, sgu_ln_b=v_sgu_ln_b, sgu_w=v_sgu_w, sgu_b=v_sgu_b,
             conv_w=v_conv_w, conv_b=v_conv_b, conv_ln_g=v_conv_ln_g, conv_ln_b=v_conv_ln_b, w_out=v_w_out,
             norm2_g=v_norm2_g, w_ff1=v_w_ff1, w_ff2=v_w_ff2, final_g=v_final_g)
    depth = w_in.shape[0]
    assert depth == 2, "core c owns layer c of every gradient"
    T, D = x.shape[1], x.shape[2]
    heads = sgu_w.shape[1]
    da = heads * CHUNK
    core = lax.axis_index("c")
    chip = 2 * lax.axis_index("x") + lax.axis_index("y")

    core_arr = core.reshape(1).astype(jnp.int32)
    chip_arr = chip.reshape(1).astype(jnp.int32)
    cw_pad = jnp.pad(conv_w, ((0, 0), (0, HALO - CONV_TAPS), (0, 0)))
    own = [{name: _cast_own_call(f"cast_{name}_l{l}", chip_arr, w[name], l) for name in BIG_NAMES}
           for l in range(depth)]
    gather_of = lambda l, group: _gather_comm([own[l][name] for name in group])
    win_g, wout_g, cw_g = _alone("gather_mix_l0", _gather_comm([own[0][name] for name in MIX_NAMES], cw_pad))
    cw_full = jnp.transpose(cw_g, (1, 2, 0, 3)).reshape(depth, HALO, da)

    ws_t = jnp.swapaxes(sgu_w, -1, -2)
    b_full = jnp.broadcast_to(sgu_b[..., None], sgu_w.shape)
    row = lambda a, l: a[l:l + 1]

    xs, projs, x1s, zs = [], [], [], []
    h = x.reshape(T, D)
    gathered = []
    for l in range(depth):
        xs.append(h)
        (x1, proj), (w1_g, w2_g) = _mix_fwd_call(
            l, h, row(norm1_g, l), win_g, row(sgu_ln_g, l), row(sgu_ln_b, l), sgu_w[l], b_full[l], cw_full[l],
            row(conv_b, l), row(conv_ln_g, l), row(conv_ln_b, l), wout_g, comm=gather_of(l, FFN_NAMES))
        gathered.append((win_g, wout_g, w1_g, w2_g))
        (h, z), nxt = _ffn_fwd_call(l, x1, row(norm2_g, l), w1_g, w2_g,
                                    comm=gather_of(l + 1, MIX_NAMES) if l + 1 < depth else None)
        if nxt:
            win_g, wout_g = nxt
        projs.append(proj)
        x1s.append(x1)
        zs.append(z)
    dx, dxb, loss, d_final_g = _loss_call(h, final_g.reshape(1, D), loss_target.reshape(T, D))

    big = {name: [None] * depth for name in BIG_NAMES}
    reduced = {name: [None] * depth for name in BIG_NAMES}
    small = {name: [None] * depth for name in SMALL_NAMES[:-1] + ["conv_w"]}

    def pair_sums(l, group):
        got = _pair_exchange_call(f"{group[0]}_l{l}", [big[name][l] for name in group])
        return [_pair_sum_call(f"pair_sum_{name}_l{l}", core_arr, big[name][l], recv) for name, recv in zip(group, got)]

    def finish_reduce(l, group, parts):
        halves = [_chip_sum_call(f"chip_sum_{name}_l{l}", core_arr, part) for name, part in zip(group, parts)]
        for name, full in zip(group, _pair_gather_call(f"{group[0]}_l{l}", halves)):
            reduced[name][l] = full

    pending = None
    for l in reversed(range(depth)):
        win_g, wout_g, w1_g, w2_g = gathered[l]
        (dx1, dx1b, dz, f_t, h2_t, dg2), parts = _ffn_bwd_call(
            l, dx, dxb, x1s[l], zs[l], row(norm2_g, l), w1_g, w2_g,
            comm=_exchange_comm(pending[2]) if pending else None)
        if pending:
            finish_reduce(pending[0], pending[1], parts)
        big["w_ff2"][l] = _wgrad_call(f"wgrad_ff2_l{l}", f_t, dxb, False, 2)
        big["w_ff1"][l] = _wgrad_call(f"wgrad_ff1_l{l}", h2_t, dz, True, 2)
        ffn_sums = pair_sums(l, FFN_NAMES)
        ((dx, dxb, dproj, h1_t, mix_t, dg1, dlng, dlnb, dws, dbs, dcw, dcb, dcg, dcbeta), parts) = _mix_bwd_call(
            l, dx1, dx1b, xs[l], projs[l], row(norm1_g, l), win_g, row(sgu_ln_g, l), row(sgu_ln_b, l), sgu_w[l],
            ws_t[l], b_full[l], cw_full[l], row(conv_b, l), row(conv_ln_g, l), row(conv_ln_b, l), wout_g,
            comm=_exchange_comm(ffn_sums))
        finish_reduce(l, FFN_NAMES, parts)
        big["w_out"][l] = _wgrad_call(f"wgrad_out_l{l}", mix_t, dx1b, False, 1)
        big["w_in"][l] = _wgrad_call(f"wgrad_in_l{l}", h1_t, dproj, True, 1)
        pending = (l, MIX_NAMES, pair_sums(l, MIX_NAMES))
        small["norm1_g"][l] = dg1[0]
        small["sgu_ln_g"][l] = dlng[0]
        small["sgu_ln_b"][l] = dlnb[0]
        small["sgu_w"][l] = dws
        small["sgu_b"][l] = dbs[:, :, 0]
        small["conv_w"][l] = dcw[:CONV_TAPS]
        small["conv_b"][l] = dcb[0]
        small["conv_ln_g"][l] = dcg[0]
        small["conv_ln_b"][l] = dcbeta[0]
        small["norm2_g"][l] = dg2[0]
    grad_x = dx.reshape(x.shape)

    small_local = {name: jnp.stack(small[name]) for name in small}
    small_local["final_g"] = d_final_g[0]
    pieces = [_rows128(small_local[name]) for name in SMALL_NAMES]
    pieces.append(_rows128(small_local["conv_w"]))
    pieces.append(jnp.broadcast_to(loss, (8, LANES)))
    offsets = [0]
    for piece in pieces:
        offsets.append(offsets[-1] + piece.shape[0])
    summed, parts = _all_sum_small_call(jnp.concatenate(pieces, axis=0), comm=_exchange_comm(pending[2]))
    finish_reduce(pending[0], pending[1], parts)
    n_small = offsets[len(SMALL_NAMES)]
    loss_out = summed[offsets[-2], 0]
    small_grads = {name: summed[offsets[k]:offsets[k + 1]].reshape(w[name].shape)
                   for k, name in enumerate(SMALL_NAMES)}
    conv_w_full = summed[offsets[-3]:offsets[-2]].reshape(depth, CONV_TAPS, da)
    conv_w_grad = lax.dynamic_slice_in_dim(conv_w_full, chip * conv_w.shape[-1], conv_w.shape[-1], axis=2)

    grads, delta, new_m, new_v = {}, {}, {}, {}
    for name in BIG_NAMES:
        grads[name], delta[name], new_m[name], new_v[name] = _adamw_layers_call(
            f"adamw_{name}", w[name], reduced[name][0], reduced[name][1], m[name], v[name])
    pack = lambda src: jnp.concatenate([_rows128(src[name]) for name in SMALL_NAMES], axis=0)
    d_, m_, v_ = _adamw_call("adamw_small", pack(w), summed[:n_small], pack(m), pack(v))
    for k, name in enumerate(SMALL_NAMES):
        sl = slice(offsets[k], offsets[k + 1])
        grads[name] = small_grads[name]
        delta[name] = d_[sl].reshape(w[name].shape)
        new_m[name] = m_[sl].reshape(w[name].shape)
        new_v[name] = v_[sl].reshape(w[name].shape)
    cshape = conv_w.shape
    flat = lambda a: a.reshape(-1, cshape[-1])
    d_, m_, v_ = _adamw_call("adamw_conv_w", flat(conv_w), flat(conv_w_grad), flat(m["conv_w"]), flat(v["conv_w"]))
    grads["conv_w"] = conv_w_grad
    delta["conv_w"], new_m["conv_w"], new_v["conv_w"] = d_.reshape(cshape), m_.reshape(cshape), v_.reshape(cshape)

    order = ["norm1_g", "w_in", "sgu_ln_g", "sgu_ln_b", "sgu_w", "sgu_b", "conv_w", "conv_b", "conv_ln_g",
             "conv_ln_b", "w_out", "norm2_g", "w_ff1", "w_ff2", "final_g"]
    return (loss_out, grad_x, *[grads[n] for n in order], *[delta[n] for n in order],
            *[new_m[n] for n in order], *[new_v[n] for n in order])
```

```python
import functools

import jax
import jax.numpy as jnp
from jax import lax
from jax.experimental import pallas as pl
from jax.experimental.pallas import tpu as pltpu

F32 = jnp.float32
BF16 = jnp.bfloat16
MESH = pl.DeviceIdType.MESH

EPS = 1e-6
CHUNK = 128
CONV_TAPS = 31
HALO = 32
N_CHIPS = 4
N_DEV = 8
LANES = 128
SUBLANES = 8

ADAM_LR = 0.001
ADAM_B1 = 0.9
ADAM_B2 = 0.999
ADAM_EPS = 1e-08
ADAM_WD = 0.01
ADAM_STEP = 10

TM_MIX = 256
TM_FFN = 512
V7X_VMEM_BYTES = 64 * 2 ** 20
VMEM_LIMIT = V7X_VMEM_BYTES - 8 * 2 ** 20


def _params(sem=None):
    return pltpu.CompilerParams(dimension_semantics=sem, vmem_limit_bytes=VMEM_LIMIT)


def _gelu_and_grad(x):
    gauss = jnp.exp(-0.5 * x * x)
    t = 1.0 / (1.0 + (0.3275911 * 0.7071067811865476) * jnp.abs(x))
    poly = t * (0.254829592 + t * (-0.284496736 + t * (1.421413741 + t * (-1.453152027 + t * 1.061405429))))
    erf_abs = 1.0 - poly * gauss
    cdf = 0.5 * (1.0 + jnp.where(x < 0, -erf_abs, erf_abs))
    return x * cdf, cdf + x * (gauss * 0.3989422804014327)


def _sigmoid(x):
    return 1.0 / (1.0 + jnp.exp(-x))


def _dot(a, b):
    return jnp.dot(a.astype(BF16), b.astype(BF16), preferred_element_type=F32)


def _dot_nt(a, b):
    return lax.dot_general(a.astype(BF16), b.astype(BF16), (((1,), (1,)), ((), ())), preferred_element_type=F32)


def _rms_fwd(x, g):
    r = lax.rsqrt(jnp.mean(x * x, axis=-1, keepdims=True) + EPS)
    xh = x * r
    return r, xh, xh * g


def _rms_bwd(dh, xh, r, g):
    dxh = dh * g
    return r * (dxh - xh * jnp.mean(dxh * xh, axis=-1, keepdims=True))


def _ln_stats(x):
    mu = jnp.mean(x, axis=-1, keepdims=True)
    xc = x - mu
    rs = lax.rsqrt(jnp.mean(xc * xc, axis=-1, keepdims=True) + EPS)
    return xc * rs, rs


def _ln_bwd(dxh, xh, rs):
    return rs * (dxh - jnp.mean(dxh, axis=-1, keepdims=True) - xh * jnp.mean(dxh * xh, axis=-1, keepdims=True))


def _group_ln_fwd(c):
    parts, rss = [], []
    for j in range(c.shape[1] // CHUNK):
        xh, rs = _ln_stats(c[:, j * CHUNK:(j + 1) * CHUNK])
        parts.append(xh)
        rss.append(rs)
    return jnp.concatenate(parts, axis=1), rss


def _group_ln_bwd(dxh, xh, rss):
    parts = []
    for j, rs in enumerate(rss):
        cols = slice(j * CHUNK, (j + 1) * CHUNK)
        parts.append(_ln_bwd(dxh[:, cols], xh[:, cols], rs))
    return jnp.concatenate(parts, axis=1)


def _sublane_shifts(ext):
    n = ext.shape[0]
    return [ext if b == 0 else pltpu.roll(ext, n - b, 0) for b in range(SUBLANES)]


def _rows_from(shifts, off, tm):
    a, b = divmod(off, SUBLANES)
    return shifts[b][a * SUBLANES:a * SUBLANES + tm]


def _tril_mask():
    t = lax.broadcasted_iota(jnp.int32, (CHUNK, CHUNK), 0)
    s = lax.broadcasted_iota(jnp.int32, (CHUNK, CHUNK), 1)
    return t >= s


def _mix_forward(ua, va, vb, gb, g_halo, ws_ref, bfull_ref, lng, lnb, cw_ref, cb, cg, cbeta, mixed_scr, cpre=None):
    tm, da = ua.shape
    heads = da // CHUNK
    u, du_fac = _gelu_and_grad(ua)
    vg, dvg_fac = _gelu_and_grad(va)
    vh, v_rs = _ln_stats(vg)
    v = vh * lng + lnb
    mask = _tril_mask()
    wm = [jnp.where(mask, ws_ref[h], 0.0).astype(BF16) for h in range(heads)]
    vb16 = v.astype(BF16)
    for ci in range(tm // CHUNK):
        rows = slice(ci * CHUNK, (ci + 1) * CHUNK)
        for h in range(heads):
            cols = slice(h * CHUNK, (h + 1) * CHUNK)
            mixed_scr[rows, cols] = jnp.dot(wm[h], vb16[rows, cols], preferred_element_type=F32) + bfull_ref[h]
    mixed = mixed_scr[...]
    a = u * mixed

    sg = _sigmoid(gb)
    g = vb * sg
    g_shifts = _sublane_shifts(jnp.concatenate([g_halo, g], axis=0))
    if cpre is None:
        cpre = jnp.zeros_like(g) + cb
        for k in range(CONV_TAPS):
            cpre = cpre + _rows_from(g_shifts, HALO - (CONV_TAPS - 1) + k, tm) * cw_ref[k:k + 1, :]
    chh, c_rss = _group_ln_fwd(cpre)
    cn = chh * cg + cbeta
    sc = _sigmoid(cn)
    cout = cn * sc
    return dict(u=u, du_fac=du_fac, dvg_fac=dvg_fac, vh=vh, v_rs=v_rs, v16=vb16, mixed=mixed, a=a, sg=sg, g=g,
                g_shifts=g_shifts, cpre=cpre, chh=chh, c_rss=c_rss, cn=cn, sc=sc, cout=cout)


ANY = pl.BlockSpec(memory_space=pl.ANY)


def _call(body, name, grid, in_specs, out_specs, out_shape, scratch_shapes, sem, operands, comm=None):
    if comm is None:
        res = pl.pallas_call(body, name=name, grid=grid, in_specs=in_specs, out_specs=out_specs, out_shape=out_shape,
                             scratch_shapes=scratch_shapes, compiler_params=_params(sem))(*operands)
        return res, []
    n_in, n_out, n_scr = len(in_specs), len(out_specs), len(scratch_shapes)
    n_cin, n_cout = len(comm["operands"]), len(comm["out_shape"])

    def fused(*refs):
        ins, refs = refs[:n_in], refs[n_in:]
        c_ins, refs = refs[:n_cin], refs[n_cin:]
        outs, refs = refs[:n_out], refs[n_out:]
        c_outs, refs = refs[:n_cout], refs[n_cout:]
        scr, c_sems = refs[:n_scr], refs[n_scr:]
        start, finish = comm["make"](c_ins, c_outs, c_sems)
        if grid:
            first = functools.reduce(jnp.logical_and, [pl.program_id(a) == 0 for a in range(len(grid))])
            last = functools.reduce(jnp.logical_and, [pl.program_id(a) == grid[a] - 1 for a in range(len(grid))])
            pl.when(first)(start)
            body(*ins, *outs, *scr)
            pl.when(last)(finish)
        else:
            start()
            body(*ins, *outs, *scr)
            finish()

    res = pl.pallas_call(
        fused, name=name, grid=grid,
        in_specs=list(in_specs) + [ANY] * n_cin, out_specs=list(out_specs) + [ANY] * n_cout,
        out_shape=list(out_shape) + list(comm["out_shape"]),
        input_output_aliases={n_in + a: n_out + b for a, b in comm["aliases"].items()},
        scratch_shapes=list(scratch_shapes) + list(comm["scratch"]),
        compiler_params=_params(sem),
    )(*operands, *comm["operands"])
    return res[:n_out], res[n_out:]


def _alone(name, comm):
    n_cin, n_cout = len(comm["operands"]), len(comm["out_shape"])

    def body(*refs):
        start, finish = comm["make"](refs[:n_cin], refs[n_cin:n_cin + n_cout], refs[n_cin + n_cout:])
        start()
        finish()

    return pl.pallas_call(
        body, name=name, in_specs=[ANY] * n_cin, out_specs=[ANY] * n_cout, out_shape=list(comm["out_shape"]),
        input_output_aliases=dict(comm["aliases"]), scratch_shapes=list(comm["scratch"]),
    )(*comm["operands"])


def _mix_fwd_call(l, x, g1, win_g, lng, lnb, ws, bfull, cw, cb, cg, cbeta, wout_g, comm=None):
    T, D = x.shape
    tm = min(TM_MIX, T)
    n_t = T // tm
    da = win_g.shape[-1]
    wo_rows = wout_g.shape[1]

    def body(x_ref, g1_ref, win_ref, lng_ref, lnb_ref, ws_ref, bf_ref, cw_ref, cb_ref, cg_ref, cbeta_ref, wout_ref,
             x1_ref, proj_ref, cpre_ref, halo_scr, mixed_scr):
        i = pl.program_id(0)

        @pl.when(i == 0)
        def _():
            halo_scr[...] = jnp.zeros_like(halo_scr)

        x_t = x_ref[...]
        _, _, h1 = _rms_fwd(x_t, g1_ref[...])
        h1b = h1.astype(BF16)
        ps = []
        for p in range(N_CHIPS):
            pp = jnp.dot(h1b, win_ref[p], preferred_element_type=F32)
            proj_ref[:, p * da:(p + 1) * da] = pp
            ps.append(pp)
        f = _mix_forward(ps[0], ps[1], ps[2], ps[3], halo_scr[...], ws_ref, bf_ref, lng_ref[...], lnb_ref[...],
                         cw_ref, cb_ref[...], cg_ref[...], cbeta_ref[...], mixed_scr)
        halo_scr[...] = f["g"][tm - HALO:, :]
        cpre_ref[...] = f["cpre"]
        mix =jnp.concatenate([f["a"], f["cout"]], axis=1).astype(BF16)
        acc = x_t
        for q in range(N_CHIPS):
            acc = acc + jnp.dot(mix[:, q * wo_rows:(q + 1) * wo_rows], wout_ref[q], preferred_element_type=F32)
        x1_ref[...] = acc

    full = lambda a: pl.BlockSpec(a.shape, lambda i: (0,) * a.ndim)
    return _call(
        body, f"mix_fwd_l{l}", (n_t,),
        in_specs=[
            pl.BlockSpec((tm, D), lambda i: (i, 0)),
            full(g1),
            pl.BlockSpec((N_CHIPS, D, da), lambda i: (0, 0, 0)),
            full(lng), full(lnb), full(ws), full(bfull), full(cw), full(cb), full(cg), full(cbeta),
            pl.BlockSpec((N_CHIPS, wo_rows, D), lambda i: (0, 0, 0)),
        ],
        out_specs=[
            pl.BlockSpec((tm, D), lambda i: (i, 0)),
            pl.BlockSpec((tm, N_CHIPS * da), lambda i: (i, 0)),
            pl.BlockSpec((tm, da), lambda i: (i, 0)),
        ],
        out_shape=[jax.ShapeDtypeStruct((T, D), F32), jax.ShapeDtypeStruct((T, N_CHIPS * da), F32),
                   jax.ShapeDtypeStruct((T, da), F32)],
        scratch_shapes=[pltpu.VMEM((HALO, da), F32), pltpu.VMEM((tm, da), F32)],
        sem=("arbitrary",),
        operands=(x, g1, win_g, lng, lnb, ws, bfull, cw, cb, cg, cbeta, wout_g), comm=comm)


def _ffn_tile(T):
    return min(TM_FFN, max(T // 2, CHUNK))


def _ffn_fwd_call(l, x1, g2, w1_g, w2_g, comm=None):
    T, D = x1.shape
    tm = _ffn_tile(T)
    n_t = T // tm
    ffb = w1_g.shape[-1]

    def body(x1_ref, g2_ref, w1_ref, w2_ref, x2_ref, z_ref, h2_scr, acc_scr):
        p = pl.program_id(1)

        @pl.when(p == 0)
        def _():
            _, _, h2 = _rms_fwd(x1_ref[...], g2_ref[...])
            h2_scr[...] = h2.astype(BF16)
            acc_scr[...] = jnp.zeros_like(acc_scr)

        z = jnp.dot(h2_scr[...], w1_ref[...], preferred_element_type=F32)
        z_ref[...] = z.astype(BF16)
        rz = jnp.maximum(z, 0.0)
        acc_scr[...] += jnp.dot((rz * rz).astype(BF16), w2_ref[...], preferred_element_type=F32)

        @pl.when(p == N_CHIPS - 1)
        def _():
            x2_ref[...] = x1_ref[...] + acc_scr[...]

    return _call(
        body, f"ffn_fwd_l{l}", (n_t, N_CHIPS),
        in_specs=[
            pl.BlockSpec((tm, D), lambda i, p: (i, 0)),
            pl.BlockSpec(g2.shape, lambda i, p: (0, 0)),
            pl.BlockSpec((None, D, ffb), lambda i, p: (p, 0, 0)),
            pl.BlockSpec((None, ffb, D), lambda i, p: (p, 0, 0)),
        ],
        out_specs=[
            pl.BlockSpec((tm, D), lambda i, p: (i, 0)),
            pl.BlockSpec((tm, ffb), lambda i, p: (i, p)),
        ],
        out_shape=[jax.ShapeDtypeStruct((T, D), F32), jax.ShapeDtypeStruct((T, N_CHIPS * ffb), BF16)],
        scratch_shapes=[pltpu.VMEM((tm, D), BF16), pltpu.VMEM((tm, D), F32)],
        sem=("arbitrary", "arbitrary"), operands=(x1, g2, w1_g, w2_g), comm=comm)


def _loss_call(x, gf, target):
    T, D = x.shape
    tm = min(TM_FFN, T)
    n_t = T // tm

    def body(x_ref, gf_ref, t_ref, dx_ref, dxb_ref, loss_ref, dgf_ref):
        i = pl.program_id(0)

        @pl.when(i == 0)
        def _():
            loss_ref[...] = jnp.zeros_like(loss_ref)
            dgf_ref[...] = jnp.zeros_like(dgf_ref)

        g = gf_ref[...]
        r, xh, y = _rms_fwd(x_ref[...], g)
        e = y - t_ref[...]
        per_tok = jnp.sum(e * e, axis=-1, keepdims=True) * (1.0 / D)
        loss_ref[...] += 0.5 * jnp.sum(per_tok, axis=0, keepdims=True)
        dy = e * (1.0 / D)
        dgf_ref[...] += jnp.sum(dy * xh, axis=0, keepdims=True)
        dx = _rms_bwd(dy, xh, r, g)
        dx_ref[...] = dx
        dxb_ref[...] = dx.astype(BF16)

    return pl.pallas_call(
        body, name="loss_head",
        grid=(n_t,),
        in_specs=[
            pl.BlockSpec((tm, D), lambda i: (i, 0)),
            pl.BlockSpec(gf.shape, lambda i: (0, 0)),
            pl.BlockSpec((tm, D), lambda i: (i, 0)),
        ],
        out_specs=[
            pl.BlockSpec((tm, D), lambda i: (i, 0)),
            pl.BlockSpec((tm, D), lambda i: (i, 0)),
            pl.BlockSpec((1, 1), lambda i: (0, 0)),
            pl.BlockSpec((1, D), lambda i: (0, 0)),
        ],
        out_shape=[jax.ShapeDtypeStruct((T, D), F32), jax.ShapeDtypeStruct((T, D), BF16),
                   jax.ShapeDtypeStruct((1, 1), F32), jax.ShapeDtypeStruct((1, D), F32)],
        compiler_params=_params(("arbitrary",)),
    )(x, gf, target)


def _ffn_bwd_call(l, dx2, dx2b, x1, z, g2, w1_g, w2_g, comm=None):
    T, D = x1.shape
    tm = _ffn_tile(T)
    n_t = T // tm
    ffb = w1_g.shape[-1]

    def body(dx2_ref, dx2b_ref, x1_ref, z_ref, g2_ref, w1_ref, w2_ref,
             dx1_ref, dx1b_ref, dz_ref, ft_ref, h2t_ref, dg2_ref, acc_scr):
        i = pl.program_id(0)
        p = pl.program_id(1)

        @pl.when(jnp.logical_and(i == 0, p == 0))
        def _():
            dg2_ref[...] = jnp.zeros_like(dg2_ref)

        @pl.when(p == 0)
        def _():
            _, _, h2 = _rms_fwd(x1_ref[...], g2_ref[...])
            h2t_ref[...] = h2.T.astype(BF16)
            acc_scr[...] = jnp.zeros_like(acc_scr)

        rz = jnp.maximum(z_ref[...].astype(F32), 0.0)
        ft_ref[...] = (rz * rz).T.astype(BF16)
        df = _dot_nt(dx2b_ref[...], w2_ref[...])
        dz = (df * (2.0 * rz)).astype(BF16)
        dz_ref[...] = dz
        acc_scr[...] += _dot_nt(dz, w1_ref[...])

        @pl.when(p == N_CHIPS - 1)
        def _():
            g = g2_ref[...]
            r, xh, _ = _rms_fwd(x1_ref[...], g)
            dh2 = acc_scr[...]
            dg2_ref[...] += jnp.sum(dh2 * xh, axis=0, keepdims=True)
            dx1 = dx2_ref[...] + _rms_bwd(dh2, xh, r, g)
            dx1_ref[...] = dx1
            dx1b_ref[...] = dx1.astype(BF16)

    return _call(
        body, f"ffn_bwd_l{l}", (n_t, N_CHIPS),
        in_specs=[
            pl.BlockSpec((tm, D), lambda i, p: (i, 0)),
            pl.BlockSpec((tm, D), lambda i, p: (i, 0)),
            pl.BlockSpec((tm, D), lambda i, p: (i, 0)),
            pl.BlockSpec((tm, ffb), lambda i, p: (i, p)),
            pl.BlockSpec(g2.shape, lambda i, p: (0, 0)),
            pl.BlockSpec((None, D, ffb), lambda i, p: (p, 0, 0)),
            pl.BlockSpec((None, ffb, D), lambda i, p: (p, 0, 0)),
        ],
        out_specs=[
            pl.BlockSpec((tm, D), lambda i, p: (i, 0)),
            pl.BlockSpec((tm, D), lambda i, p: (i, 0)),
            pl.BlockSpec((tm, ffb), lambda i, p: (i, p)),
            pl.BlockSpec((ffb, tm), lambda i, p: (p, i)),
            pl.BlockSpec((D, tm), lambda i, p: (0, i)),
            pl.BlockSpec((1, D), lambda i, p: (0, 0)),
        ],
        out_shape=[jax.ShapeDtypeStruct((T, D), F32), jax.ShapeDtypeStruct((T, D), BF16),
                   jax.ShapeDtypeStruct((T, N_CHIPS * ffb), BF16), jax.ShapeDtypeStruct((N_CHIPS * ffb, T), BF16),
                   jax.ShapeDtypeStruct((D, T), BF16), jax.ShapeDtypeStruct((1, D), F32)],
        scratch_shapes=[pltpu.VMEM((tm, D), F32)],
        sem=("arbitrary", "arbitrary"), operands=(dx2, dx2b, x1, z, g2, w1_g, w2_g), comm=comm)


def _mix_bwd_call(l, dx1, dx1b, x, proj, cpre, g1, win_g, lng, lnb, ws, wst, bfull, cw, cb, cg, cbeta, wout_g,
                  comm=None):
    T, D = x.shape
    tm = min(TM_MIX, T)
    n_t = T // tm
    da = win_g.shape[-1]
    heads = da // CHUNK
    wo_rows = wout_g.shape[1]
    halo_blocks = tm // HALO

    def body(dx1_ref, dx1b_ref, x_ref, proj_ref, vbh_ref, gbh_ref, cpre_ref, g1_ref, win_ref, lng_ref, lnb_ref, ws_ref, wst_ref,
             bf_ref, cw_ref, cb_ref, cg_ref, cbeta_ref, wout_ref,
             dx_ref, dxb_ref, dproj_ref, h1t_ref, mixt_ref,
             dg1_ref, dlng_ref, dlnb_ref, dws_ref, dbs_ref, dcw_ref, dcb_ref, dcg_ref, dcbeta_ref,
             carry_scr, mixed_scr, dv_scr):
        i = pl.program_id(0)
        tile = n_t - 1 - i

        @pl.when(i == 0)
        def _():
            carry_scr[...] = jnp.zeros_like(carry_scr)
            for ref in (dg1_ref, dlng_ref, dlnb_ref, dws_ref, dbs_ref, dcw_ref, dcb_ref, dcg_ref, dcbeta_ref):
                ref[...] = jnp.zeros_like(ref)

        g1v = g1_ref[...]
        r, xh, h1 = _rms_fwd(x_ref[...], g1v)
        h1t_ref[...] = h1.T.astype(BF16)

        ua = proj_ref[:, 0 * da:1 * da]
        va = proj_ref[:, 1 * da:2 * da]
        vb = proj_ref[:, 2 * da:3 * da]
        gb = proj_ref[:, 3 * da:4 * da]
        g_halo = jnp.where(tile > 0, vbh_ref[...] * _sigmoid(gbh_ref[...]), 0.0)
        lng_v, cg_v = lng_ref[...], cg_ref[...]
        f = _mix_forward(ua, va, vb, gb, g_halo, ws_ref, bf_ref, lng_v, lnb_ref[...], cw_ref, cb_ref[...], cg_v,
                         cbeta_ref[...], mixed_scr, cpre=cpre_ref[...])
        mixt_ref[0:da, :] = f["a"].T.astype(BF16)
        mixt_ref[da:2 * da, :] = f["cout"].T.astype(BF16)

        dxo = dx1b_ref[...]
        dmix = jnp.concatenate([_dot_nt(dxo, wout_ref[q]) for q in range(N_CHIPS)], axis=1)
        da_ = dmix[:, :da]
        dc_ = dmix[:, da:]

        dua = da_ * f["mixed"] * f["du_fac"]
        dmixed = (da_ * f["u"]).astype(BF16)
        mask_t = (lax.broadcasted_iota(jnp.int32, (CHUNK, CHUNK), 1)
                  >= lax.broadcasted_iota(jnp.int32, (CHUNK, CHUNK), 0))
        wmt =[jnp.where(mask_t, wst_ref[h], 0.0).astype(BF16) for h in range(heads)]
        mask = _tril_mask()
        v16 = f["v16"]
        for h in range(heads):
            cols = slice(h * CHUNK, (h + 1) * CHUNK)
            dws_h = jnp.zeros((CHUNK, CHUNK), F32)
            dbs_h = jnp.zeros((CHUNK, CHUNK), F32)
            for ci in range(tm // CHUNK):
                rows = slice(ci * CHUNK, (ci + 1) * CHUNK)
                dm = dmixed[rows, cols]
                dv_scr[rows, cols] = jnp.dot(wmt[h], dm, preferred_element_type=F32)
                dws_h = dws_h + _dot_nt(dm, v16[rows, cols])
                dbs_h = dbs_h + dm.astype(F32)
            dws_ref[h] += jnp.where(mask, dws_h, 0.0)
            dbs_ref[h] += jnp.broadcast_to(jnp.sum(dbs_h, axis=1, keepdims=True), (CHUNK, CHUNK))
        dv = dv_scr[...]
        dlng_ref[...] += jnp.sum(dv * f["vh"], axis=0, keepdims=True)
        dlnb_ref[...] += jnp.sum(dv, axis=0, keepdims=True)
        dva = _ln_bwd(dv * lng_v, f["vh"], f["v_rs"]) * f["dvg_fac"]

        cn, sc = f["cn"], f["sc"]
        dcn = dc_ * (sc * (1.0 + cn * (1.0 - sc)))
        dcg_ref[...] += jnp.sum(dcn * f["chh"], axis=0, keepdims=True)
        dcbeta_ref[...] += jnp.sum(dcn, axis=0, keepdims=True)
        dcpre = _group_ln_bwd(dcn * cg_v, f["chh"], f["c_rss"])
        dcb_ref[...] += jnp.sum(dcpre, axis=0, keepdims=True)
        d_shifts = _sublane_shifts(jnp.concatenate([dcpre, carry_scr[...]], axis=0))
        dg = jnp.zeros_like(dcpre)
        for k in range(CONV_TAPS):
            g_k = _rows_from(f["g_shifts"], HALO - (CONV_TAPS - 1) + k, tm)
            dcw_ref[k:k + 1, :] += jnp.sum(dcpre * g_k, axis=0, keepdims=True)
            dg = dg + _rows_from(d_shifts, CONV_TAPS - 1 - k, tm) * cw_ref[k:k + 1, :]
        carry_scr[...] = dcpre[:HALO, :]
        sg = f["sg"]
        dvb = dg * sg
        dgb = dg * vb * sg * (1.0 - sg)

        dps = [dua.astype(BF16), dva.astype(BF16), dvb.astype(BF16), dgb.astype(BF16)]
        dh1 = jnp.zeros((tm, D), F32)
        for p in range(N_CHIPS):
            dproj_ref[:, p * da:(p + 1) * da] = dps[p]
            dh1 = dh1 + _dot_nt(dps[p], win_ref[p])
        dg1_ref[...] += jnp.sum(dh1 * xh, axis=0, keepdims=True)
        dx = dx1_ref[...] + _rms_bwd(dh1, xh, r, g1v)
        dx_ref[...] = dx
        dxb_ref[...] = dx.astype(BF16)

    rev = lambda i: (n_t - 1 - i, 0)
    full = lambda a: pl.BlockSpec(a.shape, lambda i: (0,) * a.ndim)
    acc = lambda shape: pl.BlockSpec(shape, lambda i: (0,) * len(shape))
    halo_idx = lambda col: (lambda i: (jnp.maximum((n_t - 1 - i) * halo_blocks - 1, 0), col))
    small_shapes = [(1, D), (1, da), (1, da), (heads, CHUNK, CHUNK), (heads, CHUNK, CHUNK), (HALO, da),
                    (1, da), (1, da), (1, da)]
    return _call(
        body, f"mix_bwd_l{l}", (n_t,),
        in_specs=[
            pl.BlockSpec((tm, D), rev),
            pl.BlockSpec((tm, D), rev),
            pl.BlockSpec((tm, D), rev),
            pl.BlockSpec((tm, N_CHIPS * da), rev),
            pl.BlockSpec((HALO, da), halo_idx(2)),
            pl.BlockSpec((HALO, da), halo_idx(3)),
            pl.BlockSpec((tm, da), rev),
            full(g1),
            pl.BlockSpec((N_CHIPS, D, da), lambda i: (0, 0, 0)),
            full(lng), full(lnb), full(ws), full(wst), full(bfull), full(cw), full(cb), full(cg), full(cbeta),
            pl.BlockSpec((N_CHIPS, wo_rows, D), lambda i: (0, 0, 0)),
        ],
        out_specs=[
            pl.BlockSpec((tm, D), rev),
            pl.BlockSpec((tm, D), rev),
            pl.BlockSpec((tm, N_CHIPS * da), rev),
            pl.BlockSpec((D, tm), lambda i: (0, n_t - 1 - i)),
            pl.BlockSpec((2 * da, tm), lambda i: (0, n_t - 1 - i)),
        ] + [acc(s) for s in small_shapes],
        out_shape=[jax.ShapeDtypeStruct((T, D), F32), jax.ShapeDtypeStruct((T, D), BF16),
                   jax.ShapeDtypeStruct((T, N_CHIPS * da), BF16), jax.ShapeDtypeStruct((D, T), BF16),
                   jax.ShapeDtypeStruct((2 * da, T), BF16)] + [jax.ShapeDtypeStruct(s, F32) for s in small_shapes],
        scratch_shapes=[pltpu.VMEM((HALO, da), F32), pltpu.VMEM((tm, da), F32), pltpu.VMEM((tm, da), F32)],
        sem=("arbitrary",),
        operands=(dx1, dx1b, x, proj, proj, proj, cpre, g1, win_g, lng, lnb, ws, wst, bfull, cw, cb, cg, cbeta, wout_g),
        comm=comm)


def _wgrad_call(name, at, b, split_cols, steps):
    M, T = at.shape
    N = b.shape[1]
    if split_cols:
        bn = N // (N_CHIPS * steps)
        in_specs = [pl.BlockSpec((M, T), lambda j: (0, 0)), pl.BlockSpec((T, bn), lambda j: (0, j))]
        out_spec = pl.BlockSpec((None, M, bn), lambda j: (j // steps, 0, j % steps))
        out_shape = jax.ShapeDtypeStruct((N_CHIPS, M, N // N_CHIPS), BF16)
    else:
        bm = M // (N_CHIPS * steps)
        in_specs = [pl.BlockSpec((bm, T), lambda j: (j, 0)), pl.BlockSpec((T, N), lambda j: (0, 0))]
        out_spec = pl.BlockSpec((None, bm, N), lambda j: (j // steps, j % steps, 0))
        out_shape = jax.ShapeDtypeStruct((N_CHIPS, M // N_CHIPS, N), BF16)

    def body(at_ref, b_ref, o_ref):
        o_ref[...] = jnp.dot(at_ref[...], b_ref[...], preferred_element_type=F32).astype(BF16)

    return pl.pallas_call(
        body, name=name, grid=(N_CHIPS * steps,), in_specs=in_specs, out_specs=out_spec, out_shape=out_shape,
        compiler_params=_params(("arbitrary",)),
    )(at, b)


def _rows_block(rows, cols):
    br = rows
    while br * cols * 4 > 2 ** 20 and br % 16 == 0:
        br //= 2
    return br


def _cast_own_call(name, chip, w, l):
    _, rows, cols = w.shape
    br = _rows_block(rows, cols)
    n_b = rows // br

    def body(chip_ref, w_ref, o_ref):
        o_ref[...] = w_ref[...].astype(BF16)

    return pl.pallas_call(
        body, name=name,
        grid_spec=pltpu.PrefetchScalarGridSpec(
            num_scalar_prefetch=1, grid=(n_b,),
            in_specs=[pl.BlockSpec((None, br, cols), lambda i, chip_ref: (l, i, 0))],
            out_specs=pl.BlockSpec((None, br, cols), lambda i, chip_ref: (chip_ref[0], i, 0))),
        out_shape=jax.ShapeDtypeStruct((N_CHIPS, rows, cols), BF16),
        compiler_params=_params(("arbitrary",)),
    )(chip, w)


def _pair_sum_call(name, core, g, got):
    _, rh, cols = got.shape
    br = _rows_block(rh, cols)
    n_b = rh // br

    def body(core_ref, g_ref, got_ref, o_ref):
        o_ref[...] = (g_ref[...].astype(F32) + got_ref[...].astype(F32)).astype(BF16)

    half = pl.BlockSpec((None, br, cols), lambda q, i, core_ref: (q, i, 0))
    return pl.pallas_call(
        body, name=name,
        grid_spec=pltpu.PrefetchScalarGridSpec(
            num_scalar_prefetch=1, grid=(N_CHIPS, n_b),
            in_specs=[pl.BlockSpec((None, br, cols), lambda q, i, core_ref: (q, core_ref[0] * n_b + i, 0)), half],
            out_specs=half),
        out_shape=jax.ShapeDtypeStruct(got.shape, BF16),
        compiler_params=_params(("arbitrary", "arbitrary")),
    )(core, g, got)


def _chip_sum_call(name, core, parts):
    _, rh, cols = parts.shape
    br = _rows_block(rh, cols)
    n_b = rh // br

    def body(core_ref, *refs):
        o_ref = refs[N_CHIPS]
        total = refs[0][...].astype(F32)
        for q in range(1, N_CHIPS):
            total = total + refs[q][...].astype(F32)
        o_ref[...] = total

    return pl.pallas_call(
        body, name=name,
        grid_spec=pltpu.PrefetchScalarGridSpec(
            num_scalar_prefetch=1, grid=(n_b,),
            in_specs=[pl.BlockSpec((None, br, cols), functools.partial(lambda i, core_ref, q: (q, i, 0), q=q))
                      for q in range(N_CHIPS)],
            out_specs=pl.BlockSpec((br, cols), lambda i, core_ref: (core_ref[0] * n_b + i, 0))),
        out_shape=jax.ShapeDtypeStruct((2 * rh, cols), F32),
        compiler_params=_params(("arbitrary",)),
    )(core, *([parts] * N_CHIPS))


def _adamw_layers_call(name, w, g0, g1, m, v, comm=None):
    _, rows, cols = w.shape
    br = _rows_block(rows, cols)

    def body(w_ref, g0_ref, g1_ref, m_ref, v_ref, g_ref, d_ref, nm_ref, nv_ref):
        gv = jnp.where(pl.program_id(0) == 0, g0_ref[...], g1_ref[...])
        g_ref[...] = gv
        m_new = ADAM_B1 * m_ref[...] + (1.0 - ADAM_B1) * gv
        v_new = ADAM_B2 * v_ref[...] + (1.0 - ADAM_B2) * (gv * gv)
        m_hat = m_new / (1.0 - ADAM_B1 ** ADAM_STEP)
        v_hat = v_new / (1.0 - ADAM_B2 ** ADAM_STEP)
        d_ref[...] = -ADAM_LR * (m_hat / (jnp.sqrt(v_hat) + ADAM_EPS) + ADAM_WD * w_ref[...])
        nm_ref[...] = m_new
        nv_ref[...] = v_new

    both = pl.BlockSpec((None, br, cols), lambda l, i: (l, i, 0))
    one = pl.BlockSpec((br, cols), lambda l, i: (i, 0))
    return _call(body, name, (2, rows // br), in_specs=[both, one, one, both, both], out_specs=[both] * 4,
                 out_shape=[jax.ShapeDtypeStruct(w.shape, F32)] * 4, scratch_shapes=[],
                 sem=("arbitrary", "arbitrary"), operands=(w, g0, g1, m, v), comm=comm)


def _adamw_call(name, w, g, m, v):
    rows, cols = w.shape
    br = _rows_block(rows, cols)

    def body(w_ref, g_ref, m_ref, v_ref, d_ref, nm_ref, nv_ref):
        gv = g_ref[...]
        m_new = ADAM_B1 * m_ref[...] + (1.0 - ADAM_B1) * gv
        v_new = ADAM_B2 * v_ref[...] + (1.0 - ADAM_B2) * (gv * gv)
        m_hat = m_new / (1.0 - ADAM_B1 ** ADAM_STEP)
        v_hat = v_new / (1.0 - ADAM_B2 ** ADAM_STEP)
        d_ref[...] = -ADAM_LR * (m_hat / (jnp.sqrt(v_hat) + ADAM_EPS) + ADAM_WD * w_ref[...])
        nm_ref[...] = m_new
        nv_ref[...] = v_new

    spec = pl.BlockSpec((br, cols), lambda i: (i, 0))
    return pl.pallas_call(
        body, name=name, grid=(rows // br,), in_specs=[spec] * 4, out_specs=[spec] * 3,
        out_shape=[jax.ShapeDtypeStruct((rows, cols), F32)] * 3,
        compiler_params=_params(("arbitrary",)),
    )(w, g, m, v)


def _place():
    x, y, c = lax.axis_index("x"), lax.axis_index("y"), lax.axis_index("c")
    chips = [(1 - x, y), (x, 1 - y), (1 - x, 1 - y)]
    return x, y, c, 2 * x + y, chips


def _half_rows(ref, core):
    rh = ref.shape[-2] // 2
    rows = pl.ds(pl.multiple_of(core * rh, rh), rh)
    return ref.at[rows] if len(ref.shape) == 2 else ref.at[:, rows]


def _gather_comm(bufs, cw_shard=None):
    n = len(bufs)
    with_cw = cw_shard is not None

    def make(c_ins, c_outs, sems):
        ins, outs = c_ins[:n], c_outs[:n]
        send_sems, recv_sems, cw_send, cw_recv, cw_local = sems
        x, y, c, p, chips = _place()
        sibling = (x, y, 1 - c)
        qs = [2 * cx + cy for cx, cy in chips]
        hops = len(chips)

        def remote(k, j, src, dst, to):
            return pltpu.make_async_remote_copy(src_ref=src, dst_ref=dst, send_sem=send_sems.at[k, j],
                                                recv_sem=recv_sems.at[k, j], device_id=to, device_id_type=MESH)

        def cw_copy(j, dst, to):
            return pltpu.make_async_remote_copy(src_ref=c_ins[n], dst_ref=dst, send_sem=cw_send.at[j],
                                                recv_sem=cw_recv.at[j], device_id=to, device_id_type=MESH)

        def over_ici():
            return [remote(k, j, _half_rows(ins[k].at[p], c), _half_rows(outs[k].at[p], c), (*chip, c))
                    for j, chip in enumerate(chips) for k in range(n)]

        def passed_on():
            return [remote(k, hops + j, _half_rows(outs[k].at[qs[j]], c), _half_rows(outs[k].at[qs[j]], c), sibling)
                    for j in range(hops) for k in range(n)]

        def cw_copies():
            return [cw_copy(j, c_outs[n].at[p], (*chip, c)) for j, chip in enumerate(chips)]

        def cw_own():
            return pltpu.make_async_copy(c_ins[n], c_outs[n].at[p], cw_local)

        def start():
            for cp in over_ici():
                cp.start()
            if with_cw:
                cw_own().start()
                for cp in cw_copies():
                    cp.start()

        def finish():
            for j in range(hops):
                for k in range(n):
                    landed = _half_rows(outs[k].at[qs[j]], c)
                    remote(k, j, landed, landed, sibling).wait_recv()
                    remote(k, hops + j, landed, landed, sibling).start()
            for j in range(hops):
                for k in range(n):
                    other = _half_rows(outs[k].at[qs[j]], 1 - c)
                    remote(k, hops + j, other, other, sibling).wait_recv()
            if with_cw:
                for j in range(hops):
                    cw_copy(j, c_outs[n].at[qs[j]], sibling).wait_recv()
                cw_own().wait()
                for cp in cw_copies():
                    cp.wait_send()
            for cp in over_ici() + passed_on():
                cp.wait_send()

        return start, finish

    out_shape = [jax.ShapeDtypeStruct(b.shape, b.dtype) for b in bufs]
    if with_cw:
        out_shape.append(jax.ShapeDtypeStruct((N_CHIPS,) + cw_shard.shape, cw_shard.dtype))
    return dict(operands=list(bufs) + ([cw_shard] if with_cw else []), out_shape=out_shape,
                aliases={k: k for k in range(n)}, make=make,
                scratch=[pltpu.SemaphoreType.DMA((n, 6)), pltpu.SemaphoreType.DMA((n, 6)),
                         pltpu.SemaphoreType.DMA((3,)), pltpu.SemaphoreType.DMA((3,)), pltpu.SemaphoreType.DMA])


def _pair_exchange_call(l, gs):
    n = len(gs)

    def body(*refs):
        g, got = refs[:n], refs[n:2 * n]
        send_sems, recv_sems = refs[2 * n:]
        x, y, c, _, _ = _place()
        sibling = (x, y, 1 - c)
        copies = [pltpu.make_async_remote_copy(src_ref=_half_rows(g[k], 1 - c), dst_ref=got[k],
                                               send_sem=send_sems.at[k], recv_sem=recv_sems.at[k],
                                               device_id=sibling, device_id_type=MESH) for k in range(n)]
        for cp in copies:
            cp.start()
        for cp in copies:
            cp.wait_send()
            cp.wait_recv()

    return pl.pallas_call(
        body, name=f"grad_pair_exchange_l{l}",
        in_specs=[ANY] * n, out_specs=[ANY] * n,
        out_shape=[jax.ShapeDtypeStruct((g.shape[0], g.shape[1] // 2, g.shape[2]), g.dtype) for g in gs],
        scratch_shapes=[pltpu.SemaphoreType.DMA((n,)), pltpu.SemaphoreType.DMA((n,))],
    )(*gs)


def _exchange_comm(sums):
    n = len(sums)

    def make(ins, outs, sems):
        send_sems, recv_sems, local_sems = sems
        x, y, c, p, chips = _place()
        qs = [2 * cx + cy for cx, cy in chips]

        def remote(k, j, src, dst, to):
            return pltpu.make_async_remote_copy(src_ref=src, dst_ref=dst, send_sem=send_sems.at[k, j],
                                                recv_sem=recv_sems.at[k, j], device_id=to, device_id_type=MESH)

        def own():
            return [pltpu.make_async_copy(ins[k].at[p], outs[k].at[p], local_sems.at[k]) for k in range(n)]

        def sent():
            return [remote(k, j, ins[k].at[qs[j]], outs[k].at[p], (*chip, c))
                    for j, chip in enumerate(chips) for k in range(n)]

        def start():
            for cp in own() + sent():
                cp.start()

        def finish():
            for j in range(len(chips)):
                for k in range(n):
                    remote(k, j, ins[k].at[qs[j]], outs[k].at[qs[j]], (x, y, c)).wait_recv()
            for cp in sent():
                cp.wait_send()
            for cp in own():
                cp.wait()

        return start, finish

    return dict(operands=list(sums), out_shape=[jax.ShapeDtypeStruct(s.shape, s.dtype) for s in sums], aliases={},
                make=make, scratch=[pltpu.SemaphoreType.DMA((n, 3)), pltpu.SemaphoreType.DMA((n, 3)),
                                    pltpu.SemaphoreType.DMA((n,))])


def _pair_gather_call(l, halves):
    n = len(halves)

    def body(*refs):
        ins, outs = refs[:n], refs[n:2 * n]
        send_sems, recv_sems = refs[2 * n:]
        x, y, c, _, _ = _place()
        sibling = (x, y, 1 - c)

        def remote(k, src, dst):
            return pltpu.make_async_remote_copy(src_ref=src, dst_ref=dst, send_sem=send_sems.at[k],
                                                recv_sem=recv_sems.at[k], device_id=sibling, device_id_type=MESH)

        sent = [remote(k, _half_rows(ins[k], c), _half_rows(outs[k], c)) for k in range(n)]
        for cp in sent:
            cp.start()
        for k in range(n):
            other = _half_rows(outs[k], 1 - c)
            remote(k, other, other).wait_recv()
        for cp in sent:
            cp.wait_send()

    return pl.pallas_call(
        body, name=f"grad_pair_gather_l{l}",
        in_specs=[ANY] * n, out_specs=[ANY] * n,
        out_shape=[jax.ShapeDtypeStruct(h.shape, h.dtype) for h in halves],
        input_output_aliases={k: k for k in range(n)},
        scratch_shapes=[pltpu.SemaphoreType.DMA((n,)), pltpu.SemaphoreType.DMA((n,))],
    )(*halves)


def _all_sum_small_call(block, comm=None):
    m_per, n = block.shape

    def body(x_ref, sum_ref, all_ref, send_sems, recv_sems, local_sem):
        x, y, c, _, chip_list = _place()
        me, sibling = (x, y, c), (x, y, 1 - c)

        def rows(px, py, pc):
            return all_ref.at[pl.ds((4 * px + 2 * py + pc) * m_per, m_per), :]

        def copy(k, blk, to, src=None):
            return pltpu.make_async_remote_copy(src_ref=rows(*blk) if src is None else src, dst_ref=rows(*blk),
                                                send_sem=send_sems.at[k], recv_sem=recv_sems.at[k],
                                                device_id=to, device_id_type=MESH)

        mine = pltpu.make_async_copy(x_ref, rows(*me), local_sem)
        mine.start()
        first = [copy(0, me, sibling, src=x_ref)]
        first += [copy(1 + j, me, (*chip, c), src=x_ref) for j, chip in enumerate(chip_list)]
        for cp in first:
            cp.start()
        passed = [copy(4 + j, (*chip, c), sibling) for j, chip in enumerate(chip_list)]
        for j, chip in enumerate(chip_list):
            copy(1 + j, (*chip, c), me).wait_recv()
            passed[j].start()
        copy(0, sibling, me).wait_recv()
        for j, chip in enumerate(chip_list):
            copy(4 + j, (*chip, 1 - c), me).wait_recv()
        for cp in first + passed:
            cp.wait_send()
        mine.wait()
        total = all_ref[0:m_per, :]
        for d in range(1, N_DEV):
            total = total + all_ref[d * m_per:(d + 1) * m_per, :]
        sum_ref[...] = total

    vmem = pl.BlockSpec(memory_space=pltpu.VMEM)
    res, c_res = _call(
        body, "small_all_sum", (), in_specs=[vmem], out_specs=[vmem, vmem],
        out_shape=[jax.ShapeDtypeStruct((m_per, n), F32), jax.ShapeDtypeStruct((N_DEV * m_per, n), F32)],
        scratch_shapes=[pltpu.SemaphoreType.DMA((7,)), pltpu.SemaphoreType.DMA((7,)), pltpu.SemaphoreType.DMA],
        sem=None, operands=(block,), comm=comm)
    return res[0], c_res


SMALL_NAMES = ["norm1_g", "sgu_ln_g", "sgu_ln_b", "sgu_w", "sgu_b", "conv_b", "conv_ln_g", "conv_ln_b", "norm2_g",
               "final_g"]
MIX_NAMES = ["w_in", "w_out"]
FFN_NAMES = ["w_ff1", "w_ff2"]
BIG_NAMES = MIX_NAMES + FFN_NAMES


def _rows128(a):
    return a.reshape(-1, LANES)


def kernel(x, norm1_g, w_in, sgu_ln_g, sgu_ln_b, sgu_w, sgu_b, conv_w, conv_b, conv_ln_g, conv_ln_b, w_out, norm2_g, w_ff1, w_ff2, final_g, loss_target, m_norm1_g, m_w_in, m_sgu_ln_g, m_sgu_ln_b, m_sgu_w, m_sgu_b, m_conv_w, m_conv_b, m_conv_ln_g, m_conv_ln_b, m_w_out, m_norm2_g, m_w_ff1, m_w_ff2, m_final_g, v_norm1_g, v_w_in, v_sgu_ln_g, v_sgu_ln_b, v_sgu_w, v_sgu_b, v_conv_w, v_conv_b, v_conv_ln_g, v_conv_ln_b, v_w_out, v_norm2_g, v_w_ff1, v_w_ff2, v_final_g):
    w = dict(norm1_g=norm1_g, w_in=w_in, sgu_ln_g=sgu_ln_g, sgu_ln_b=sgu_ln_b, sgu_w=sgu_w, sgu_b=sgu_b,
             conv_w=conv_w, conv_b=conv_b, conv_ln_g=conv_ln_g, conv_ln_b=conv_ln_b, w_out=w_out, norm2_g=norm2_g,
             w_ff1=w_ff1, w_ff2=w_ff2, final_g=final_g)
    m = dict(norm1_g=m_norm1_g, w_in=m_w_in, sgu_ln_g=m_sgu_ln_g, sgu_ln_b=m_sgu_ln_b, sgu_w=m_sgu_w, sgu_b=m_sgu_b,
             conv_w=m_conv_w, conv_b=m_conv_b, conv_ln_g=m_conv_ln_g, conv_ln_b=m_conv_ln_b, w_out=m_w_out,
             norm2_g=m_norm2_g, w_ff1=m_w_ff1, w_ff2=m_w_ff2, final_g=m_final_g)
    v = dict(norm1_g=v_norm1_g, w_in=v_w_in, sgu_ln_g=v_sgu_ln_g, sgu_ln_b=v_sgu_ln_b, sgu_w=v_sgu_w, sgu_b=v_sgu_b,
             conv_w=v_conv_w, conv_b=v_conv_b, conv_ln_g=v_conv_ln_g, conv_ln_b=v_conv_ln_b, w_out=v_w_out,
             norm2_g=v_norm2_g, w_ff1=v_w_ff1, w_ff2=v_w_ff2, final_g=v_final_g)
    depth = w_in.shape[0]
    assert depth == 2, "core c owns layer c of every gradient"
    T, D = x.shape[1], x.shape[2]
    heads = sgu_w.shape[1]
    da = heads * CHUNK
    core = lax.axis_index("c")
    chip = 2 * lax.axis_index("x") + lax.axis_index("y")

    core_arr = core.reshape(1).astype(jnp.int32)
    chip_arr = chip.reshape(1).astype(jnp.int32)
    cw_pad = jnp.pad(conv_w, ((0, 0), (0, HALO - CONV_TAPS), (0, 0)))
    own = [{name: _cast_own_call(f"cast_{name}_l{l}", chip_arr, w[name], l) for name in BIG_NAMES}
           for l in range(depth)]
    gather_of = lambda l, group: _gather_comm([own[l][name] for name in group])
    win_g, wout_g, cw_g = _alone("gather_mix_l0", _gather_comm([own[0][name] for name in MIX_NAMES], cw_pad))
    cw_full = jnp.transpose(cw_g, (1, 2, 0, 3)).reshape(depth, HALO, da)

    ws_t = jnp.swapaxes(sgu_w, -1, -2)
    b_full = jnp.broadcast_to(sgu_b[..., None], sgu_w.shape)
    row = lambda a, l: a[l:l + 1]

    xs, projs, cpres, x1s, zs = [], [], [], [], []
    h = x.reshape(T, D)
    gathered = []
    w1_g = None
    for l in range(depth):
        xs.append(h)
        (x1, proj, cpre), late = _mix_fwd_call(
            l, h, row(norm1_g, l), win_g, row(sgu_ln_g, l), row(sgu_ln_b, l), sgu_w[l], b_full[l], cw_full[l],
            row(conv_b, l), row(conv_ln_g, l), row(conv_ln_b, l), wout_g,
            comm=gather_of(l, FFN_NAMES if w1_g is None else FFN_NAMES[1:]))
        w1_g, w2_g = late if len(late) == 2 else (w1_g, late[0])
        gathered.append((win_g, wout_g, w1_g, w2_g))
        (h, z), nxt = _ffn_fwd_call(l, x1, row(norm2_g, l), w1_g, w2_g,
                                    comm=gather_of(l + 1, MIX_NAMES + FFN_NAMES[:1]) if l + 1 < depth else None)
        if nxt:
            win_g, wout_g, w1_g = nxt
        projs.append(proj)
        cpres.append(cpre)
        x1s.append(x1)
        zs.append(z)
    dx, dxb, loss, d_final_g = _loss_call(h, final_g.reshape(1, D), loss_target.reshape(T, D))

    big = {name: [None] * depth for name in BIG_NAMES}
    reduced = {name: [None] * depth for name in BIG_NAMES}
    small = {name: [None] * depth for name in SMALL_NAMES[:-1] + ["conv_w"]}

    def pair_sums(l, group):
        got = _pair_exchange_call(f"{group[0]}_l{l}", [big[name][l] for name in group])
        return [_pair_sum_call(f"pair_sum_{name}_l{l}", core_arr, big[name][l], recv) for name, recv in zip(group, got)]

    def finish_reduce(l, group, parts):
        halves = [_chip_sum_call(f"chip_sum_{name}_l{l}", core_arr, part) for name, part in zip(group, parts)]
        for name, full in zip(group, _pair_gather_call(f"{group[0]}_l{l}", halves)):
            reduced[name][l] = full

    pending = None
    for l in reversed(range(depth)):
        win_g, wout_g, w1_g, w2_g = gathered[l]
        (dx1, dx1b, dz, f_t, h2_t, dg2), parts = _ffn_bwd_call(
            l, dx, dxb, x1s[l], zs[l], row(norm2_g, l), w1_g, w2_g,
            comm=_exchange_comm(pending[2]) if pending else None)
        if pending:
            finish_reduce(pending[0], pending[1], parts)
        big["w_ff2"][l] = _wgrad_call(f"wgrad_ff2_l{l}", f_t, dxb, False, 2)
        big["w_ff1"][l] = _wgrad_call(f"wgrad_ff1_l{l}", h2_t, dz, True, 2)
        ffn_sums = pair_sums(l, FFN_NAMES)
        ((dx, dxb, dproj, h1_t, mix_t, dg1, dlng, dlnb, dws, dbs, dcw, dcb, dcg, dcbeta), parts) = _mix_bwd_call(
            l, dx1, dx1b, xs[l], projs[l], cpres[l], row(norm1_g, l), win_g, row(sgu_ln_g, l), row(sgu_ln_b, l), sgu_w[l],
            ws_t[l], b_full[l], cw_full[l], row(conv_b, l), row(conv_ln_g, l), row(conv_ln_b, l), wout_g,
            comm=_exchange_comm(ffn_sums))
        finish_reduce(l, FFN_NAMES, parts)
        big["w_out"][l] = _wgrad_call(f"wgrad_out_l{l}", mix_t, dx1b, False, 1)
        big["w_in"][l] = _wgrad_call(f"wgrad_in_l{l}", h1_t, dproj, True, 1)
        pending = (l, MIX_NAMES, pair_sums(l, MIX_NAMES))
        small["norm1_g"][l] = dg1[0]
        small["sgu_ln_g"][l] = dlng[0]
        small["sgu_ln_b"][l] = dlnb[0]
        small["sgu_w"][l] = dws
        small["sgu_b"][l] = dbs[:, :, 0]
        small["conv_w"][l] = dcw[:CONV_TAPS]
        small["conv_b"][l] = dcb[0]
        small["conv_ln_g"][l] = dcg[0]
        small["conv_ln_b"][l] = dcbeta[0]
        small["norm2_g"][l] = dg2[0]
    grad_x = dx.reshape(x.shape)

    small_local = {name: jnp.stack(small[name]) for name in small}
    small_local["final_g"] = d_final_g[0]
    pieces = [_rows128(small_local[name]) for name in SMALL_NAMES]
    pieces.append(_rows128(small_local["conv_w"]))
    pieces.append(jnp.broadcast_to(loss, (8, LANES)))
    offsets = [0]
    for piece in pieces:
        offsets.append(offsets[-1] + piece.shape[0])
    summed, parts = _all_sum_small_call(jnp.concatenate(pieces, axis=0), comm=_exchange_comm(pending[2]))
    finish_reduce(pending[0], pending[1], parts)
    n_small = offsets[len(SMALL_NAMES)]
    loss_out = summed[offsets[-2], 0]
    small_grads = {name: summed[offsets[k]:offsets[k + 1]].reshape(w[name].shape)
                   for k, name in enumerate(SMALL_NAMES)}
    conv_w_full = summed[offsets[-3]:offsets[-2]].reshape(depth, CONV_TAPS, da)
    conv_w_grad = lax.dynamic_slice_in_dim(conv_w_full, chip * conv_w.shape[-1], conv_w.shape[-1], axis=2)

    grads, delta, new_m, new_v = {}, {}, {}, {}
    for name in BIG_NAMES:
        (grads[name], delta[name], new_m[name], new_v[name]), _ = _adamw_layers_call(
            f"adamw_{name}", w[name], reduced[name][0], reduced[name][1], m[name], v[name])
    pack = lambda src: jnp.concatenate([_rows128(src[name]) for name in SMALL_NAMES], axis=0)
    d_, m_, v_ = _adamw_call("adamw_small", pack(w), summed[:n_small], pack(m), pack(v))
    for k, name in enumerate(SMALL_NAMES):
        sl = slice(offsets[k], offsets[k + 1])
        grads[name] = small_grads[name]
        delta[name] = d_[sl].reshape(w[name].shape)
        new_m[name] = m_[sl].reshape(w[name].shape)
        new_v[name] = v_[sl].reshape(w[name].shape)
    cshape = conv_w.shape
    flat = lambda a: a.reshape(-1, cshape[-1])
    d_, m_, v_ = _adamw_call("adamw_conv_w", flat(conv_w), flat(conv_w_grad), flat(m["conv_w"]), flat(v["conv_w"]))
    grads["conv_w"] = conv_w_grad
    delta["conv_w"], new_m["conv_w"], new_v["conv_w"] = d_.reshape(cshape), m_.reshape(cshape), v_.reshape(cshape)

    order = ["norm1_g", "w_in", "sgu_ln_g", "sgu_ln_b", "sgu_w", "sgu_b", "conv_w", "conv_b", "conv_ln_g",
             "conv_ln_b", "w_out", "norm2_g", "w_ff1", "w_ff2", "final_g"]
    return (loss_out, grad_x, *[grads[n] for n in order], *[delta[n] for n in order],
            *[new_m[n] for n in order], *[new_v[n] for n in order])
```

```python
import functools

import jax
import jax.numpy as jnp
from jax import lax
from jax.experimental import pallas as pl
from jax.experimental.pallas import tpu as pltpu

F32 = jnp.float32
BF16 = jnp.bfloat16
MESH = pl.DeviceIdType.MESH

EPS = 1e-6
CHUNK = 128
CONV_TAPS = 31
HALO = 32
N_CHIPS = 4
N_DEV = 8
LANES = 128
SUBLANES = 8

ADAM_LR = 0.001
ADAM_B1 = 0.9
ADAM_B2 = 0.999
ADAM_EPS = 1e-08
ADAM_WD = 0.01
ADAM_STEP = 10

TM_MIX = 256
TM_FFN = 512
V7X_VMEM_BYTES = 64 * 2 ** 20
VMEM_LIMIT = V7X_VMEM_BYTES - 8 * 2 ** 20


def _params(sem=None):
    return pltpu.CompilerParams(dimension_semantics=sem, vmem_limit_bytes=VMEM_LIMIT)


def _gelu_and_grad(x):
    gauss = jnp.exp(-0.5 * x * x)
    t = 1.0 / (1.0 + (0.3275911 * 0.7071067811865476) * jnp.abs(x))
    poly = t * (0.254829592 + t * (-0.284496736 + t * (1.421413741 + t * (-1.453152027 + t * 1.061405429))))
    erf_abs = 1.0 - poly * gauss
    cdf = 0.5 * (1.0 + jnp.where(x < 0, -erf_abs, erf_abs))
    return x * cdf, cdf + x * (gauss * 0.3989422804014327)


def _sigmoid(x):
    return 1.0 / (1.0 + jnp.exp(-x))


def _dot(a, b):
    return jnp.dot(a.astype(BF16), b.astype(BF16), preferred_element_type=F32)


def _dot_nt(a, b):
    return lax.dot_general(a.astype(BF16), b.astype(BF16), (((1,), (1,)), ((), ())), preferred_element_type=F32)


def _rms_fwd(x, g):
    r = lax.rsqrt(jnp.mean(x * x, axis=-1, keepdims=True) + EPS)
    xh = x * r
    return r, xh, xh * g


def _rms_bwd(dh, xh, r, g):
    dxh = dh * g
    return r * (dxh - xh * jnp.mean(dxh * xh, axis=-1, keepdims=True))


def _ln_stats(x):
    mu = jnp.mean(x, axis=-1, keepdims=True)
    xc = x - mu
    rs = lax.rsqrt(jnp.mean(xc * xc, axis=-1, keepdims=True) + EPS)
    return xc * rs, rs


def _ln_bwd(dxh, xh, rs):
    return rs * (dxh - jnp.mean(dxh, axis=-1, keepdims=True) - xh * jnp.mean(dxh * xh, axis=-1, keepdims=True))


def _group_ln_fwd(c):
    parts, rss = [], []
    for j in range(c.shape[1] // CHUNK):
        xh, rs = _ln_stats(c[:, j * CHUNK:(j + 1) * CHUNK])
        parts.append(xh)
        rss.append(rs)
    return jnp.concatenate(parts, axis=1), rss


def _group_ln_bwd(dxh, xh, rss):
    parts = []
    for j, rs in enumerate(rss):
        cols = slice(j * CHUNK, (j + 1) * CHUNK)
        parts.append(_ln_bwd(dxh[:, cols], xh[:, cols], rs))
    return jnp.concatenate(parts, axis=1)


def _sublane_shifts(ext):
    n = ext.shape[0]
    return [ext if b == 0 else pltpu.roll(ext, n - b, 0) for b in range(SUBLANES)]


def _rows_from(shifts, off, tm):
    a, b = divmod(off, SUBLANES)
    return shifts[b][a * SUBLANES:a * SUBLANES + tm]


def _conv_taps(shifts, offsets, taps_ref, tm):
    outs = []
    for j in range(shifts[0].shape[1] // CHUNK):
        cols = slice(j * CHUNK, (j + 1) * CHUNK)
        slabs = jnp.concatenate([_rows_from(shifts, off, tm)[:, cols].astype(BF16) for off in offsets], axis=1)
        outs.append(jnp.dot(slabs, taps_ref[j], preferred_element_type=F32))
    return jnp.concatenate(outs, axis=1)


def _tril_mask():
    t = lax.broadcasted_iota(jnp.int32, (CHUNK, CHUNK), 0)
    s = lax.broadcasted_iota(jnp.int32, (CHUNK, CHUNK), 1)
    return t >= s


def _mix_forward(ua, va, vb, gb, g_halo, ws_ref, bfull_ref, lng, lnb, cw_ref, cb, cg, cbeta, mixed_scr, cpre=None):
    tm, da = ua.shape
    heads = da // CHUNK
    u, du_fac = _gelu_and_grad(ua)
    vg, dvg_fac = _gelu_and_grad(va)
    vh, v_rs = _ln_stats(vg)
    v = vh * lng + lnb
    mask = _tril_mask()
    wm = [jnp.where(mask, ws_ref[h], 0.0).astype(BF16) for h in range(heads)]
    vb16 = v.astype(BF16)
    for ci in range(tm // CHUNK):
        rows = slice(ci * CHUNK, (ci + 1) * CHUNK)
        for h in range(heads):
            cols = slice(h * CHUNK, (h + 1) * CHUNK)
            mixed_scr[rows, cols] = jnp.dot(wm[h], vb16[rows, cols], preferred_element_type=F32) + bfull_ref[h]
    mixed = mixed_scr[...]
    a = u * mixed

    sg = _sigmoid(gb)
    g = vb * sg
    g_shifts = _sublane_shifts(jnp.concatenate([g_halo, g], axis=0))
    if cpre is None:
        cpre = _conv_taps(g_shifts, [HALO - (CONV_TAPS - 1) + k for k in range(CONV_TAPS)], cw_ref, tm) + cb
    chh, c_rss = _group_ln_fwd(cpre)
    cn = chh * cg + cbeta
    sc = _sigmoid(cn)
    cout = cn * sc
    return dict(u=u, du_fac=du_fac, dvg_fac=dvg_fac, vh=vh, v_rs=v_rs, v16=vb16, mixed=mixed, a=a, sg=sg, g=g,
                g_shifts=g_shifts, cpre=cpre, chh=chh, c_rss=c_rss, cn=cn, sc=sc, cout=cout)


ANY = pl.BlockSpec(memory_space=pl.ANY)


def _call(body, name, grid, in_specs, out_specs, out_shape, scratch_shapes, sem, operands, comm=None):
    if comm is None:
        res = pl.pallas_call(body, name=name, grid=grid, in_specs=in_specs, out_specs=out_specs, out_shape=out_shape,
                             scratch_shapes=scratch_shapes, compiler_params=_params(sem))(*operands)
        return res, []
    n_in, n_out, n_scr = len(in_specs), len(out_specs), len(scratch_shapes)
    n_cin, n_cout = len(comm["operands"]), len(comm["out_shape"])

    def fused(*refs):
        ins, refs = refs[:n_in], refs[n_in:]
        c_ins, refs = refs[:n_cin], refs[n_cin:]
        outs, refs = refs[:n_out], refs[n_out:]
        c_outs, refs = refs[:n_cout], refs[n_cout:]
        scr, c_sems = refs[:n_scr], refs[n_scr:]
        start, finish = comm["make"](c_ins, c_outs, c_sems)
        if grid:
            first = functools.reduce(jnp.logical_and, [pl.program_id(a) == 0 for a in range(len(grid))])
            last = functools.reduce(jnp.logical_and, [pl.program_id(a) == grid[a] - 1 for a in range(len(grid))])
            pl.when(first)(start)
            body(*ins, *outs, *scr)
            pl.when(last)(finish)
        else:
            start()
            body(*ins, *outs, *scr)
            finish()

    res = pl.pallas_call(
        fused, name=name, grid=grid,
        in_specs=list(in_specs) + [ANY] * n_cin, out_specs=list(out_specs) + [ANY] * n_cout,
        out_shape=list(out_shape) + list(comm["out_shape"]),
        input_output_aliases={n_in + a: n_out + b for a, b in comm["aliases"].items()},
        scratch_shapes=list(scratch_shapes) + list(comm["scratch"]),
        compiler_params=_params(sem),
    )(*operands, *comm["operands"])
    return res[:n_out], res[n_out:]


def _alone(name, comm):
    n_cin, n_cout = len(comm["operands"]), len(comm["out_shape"])

    def body(*refs):
        start, finish = comm["make"](refs[:n_cin], refs[n_cin:n_cin + n_cout], refs[n_cin + n_cout:])
        start()
        finish()

    return pl.pallas_call(
        body, name=name, in_specs=[ANY] * n_cin, out_specs=[ANY] * n_cout, out_shape=list(comm["out_shape"]),
        input_output_aliases=dict(comm["aliases"]), scratch_shapes=list(comm["scratch"]),
    )(*comm["operands"])


def _mix_fwd_call(l, x, g1, win_g, lng, lnb, ws, bfull, cw, cb, cg, cbeta, wout_g, comm=None):
    T, D = x.shape
    tm = min(TM_MIX, T)
    n_t = T // tm
    da = win_g.shape[-1]
    wo_rows = wout_g.shape[1]

    def body(x_ref, g1_ref, win_ref, lng_ref, lnb_ref, ws_ref, bf_ref, cw_ref, cb_ref, cg_ref, cbeta_ref, wout_ref,
             x1_ref, proj_ref, cpre_ref, halo_scr, mixed_scr):
        i = pl.program_id(0)

        @pl.when(i == 0)
        def _():
            halo_scr[...] = jnp.zeros_like(halo_scr)

        x_t = x_ref[...]
        _, _, h1 = _rms_fwd(x_t, g1_ref[...])
        h1b = h1.astype(BF16)
        ps = []
        for p in range(N_CHIPS):
            pp = jnp.dot(h1b, win_ref[p], preferred_element_type=F32)
            proj_ref[:, p * da:(p + 1) * da] = pp
            ps.append(pp)
        f = _mix_forward(ps[0], ps[1], ps[2], ps[3], halo_scr[...], ws_ref, bf_ref, lng_ref[...], lnb_ref[...],
                         cw_ref, cb_ref[...], cg_ref[...], cbeta_ref[...], mixed_scr)
        halo_scr[...] = f["g"][tm - HALO:, :]
        cpre_ref[...] = f["cpre"]
        mix =jnp.concatenate([f["a"], f["cout"]], axis=1).astype(BF16)
        acc = x_t
        for q in range(N_CHIPS):
            acc = acc + jnp.dot(mix[:, q * wo_rows:(q + 1) * wo_rows], wout_ref[q], preferred_element_type=F32)
        x1_ref[...] = acc

    full = lambda a: pl.BlockSpec(a.shape, lambda i: (0,) * a.ndim)
    return _call(
        body, f"mix_fwd_l{l}", (n_t,),
        in_specs=[
            pl.BlockSpec((tm, D), lambda i: (i, 0)),
            full(g1),
            pl.BlockSpec((N_CHIPS, D, da), lambda i: (0, 0, 0)),
            full(lng), full(lnb), full(ws), full(bfull), full(cw), full(cb), full(cg), full(cbeta),
            pl.BlockSpec((N_CHIPS, wo_rows, D), lambda i: (0, 0, 0)),
        ],
        out_specs=[
            pl.BlockSpec((tm, D), lambda i: (i, 0)),
            pl.BlockSpec((tm, N_CHIPS * da), lambda i: (i, 0)),
            pl.BlockSpec((tm, da), lambda i: (i, 0)),
        ],
        out_shape=[jax.ShapeDtypeStruct((T, D), F32), jax.ShapeDtypeStruct((T, N_CHIPS * da), F32),
                   jax.ShapeDtypeStruct((T, da), F32)],
        scratch_shapes=[pltpu.VMEM((HALO, da), F32), pltpu.VMEM((tm, da), F32)],
        sem=("arbitrary",),
        operands=(x, g1, win_g, lng, lnb, ws, bfull, cw, cb, cg, cbeta, wout_g), comm=comm)


def _ffn_tile(T):
    return min(TM_FFN, max(T // 2, CHUNK))


def _ffn_fwd_call(l, x1, g2, w1_g, w2_g, comm=None):
    T, D = x1.shape
    tm = _ffn_tile(T)
    n_t = T // tm
    ffb = w1_g.shape[-1]

    def body(x1_ref, g2_ref, w1_ref, w2_ref, x2_ref, z_ref, h2_scr, acc_scr):
        p = pl.program_id(1)

        @pl.when(p == 0)
        def _():
            _, _, h2 = _rms_fwd(x1_ref[...], g2_ref[...])
            h2_scr[...] = h2.astype(BF16)
            acc_scr[...] = jnp.zeros_like(acc_scr)

        z = jnp.dot(h2_scr[...], w1_ref[...], preferred_element_type=F32)
        z_ref[...] = z.astype(BF16)
        rz = jnp.maximum(z, 0.0)
        acc_scr[...] += jnp.dot((rz * rz).astype(BF16), w2_ref[...], preferred_element_type=F32)

        @pl.when(p == N_CHIPS - 1)
        def _():
            x2_ref[...] = x1_ref[...] + acc_scr[...]

    return _call(
        body, f"ffn_fwd_l{l}", (n_t, N_CHIPS),
        in_specs=[
            pl.BlockSpec((tm, D), lambda i, p: (i, 0)),
            pl.BlockSpec(g2.shape, lambda i, p: (0, 0)),
            pl.BlockSpec((None, D, ffb), lambda i, p: (p, 0, 0)),
            pl.BlockSpec((None, ffb, D), lambda i, p: (p, 0, 0)),
        ],
        out_specs=[
            pl.BlockSpec((tm, D), lambda i, p: (i, 0)),
            pl.BlockSpec((tm, ffb), lambda i, p: (i, p)),
        ],
        out_shape=[jax.ShapeDtypeStruct((T, D), F32), jax.ShapeDtypeStruct((T, N_CHIPS * ffb), BF16)],
        scratch_shapes=[pltpu.VMEM((tm, D), BF16), pltpu.VMEM((tm, D), F32)],
        sem=("arbitrary", "arbitrary"), operands=(x1, g2, w1_g, w2_g), comm=comm)


def _loss_call(x, gf, target):
    T, D = x.shape
    tm = min(TM_FFN, T)
    n_t = T // tm

    def body(x_ref, gf_ref, t_ref, dx_ref, dxb_ref, loss_ref, dgf_ref):
        i = pl.program_id(0)

        @pl.when(i == 0)
        def _():
            loss_ref[...] = jnp.zeros_like(loss_ref)
            dgf_ref[...] = jnp.zeros_like(dgf_ref)

        g = gf_ref[...]
        r, xh, y = _rms_fwd(x_ref[...], g)
        e = y - t_ref[...]
        per_tok = jnp.sum(e * e, axis=-1, keepdims=True) * (1.0 / D)
        loss_ref[...] += 0.5 * jnp.sum(per_tok, axis=0, keepdims=True)
        dy = e * (1.0 / D)
        dgf_ref[...] += jnp.sum(dy * xh, axis=0, keepdims=True)
        dx = _rms_bwd(dy, xh, r, g)
        dx_ref[...] = dx
        dxb_ref[...] = dx.astype(BF16)

    return pl.pallas_call(
        body, name="loss_head",
        grid=(n_t,),
        in_specs=[
            pl.BlockSpec((tm, D), lambda i: (i, 0)),
            pl.BlockSpec(gf.shape, lambda i: (0, 0)),
            pl.BlockSpec((tm, D), lambda i: (i, 0)),
        ],
        out_specs=[
            pl.BlockSpec((tm, D), lambda i: (i, 0)),
            pl.BlockSpec((tm, D), lambda i: (i, 0)),
            pl.BlockSpec((1, 1), lambda i: (0, 0)),
            pl.BlockSpec((1, D), lambda i: (0, 0)),
        ],
        out_shape=[jax.ShapeDtypeStruct((T, D), F32), jax.ShapeDtypeStruct((T, D), BF16),
                   jax.ShapeDtypeStruct((1, 1), F32), jax.ShapeDtypeStruct((1, D), F32)],
        compiler_params=_params(("arbitrary",)),
    )(x, gf, target)


def _ffn_bwd_call(l, dx2, dx2b, x1, z, g2, w1_g, w2_g, comm=None):
    T, D = x1.shape
    tm = _ffn_tile(T)
    n_t = T // tm
    ffb = w1_g.shape[-1]

    def body(dx2_ref, dx2b_ref, x1_ref, z_ref, g2_ref, w1_ref, w2_ref,
             dx1_ref, dx1b_ref, dz_ref, ft_ref, h2t_ref, dg2_ref, acc_scr):
        i = pl.program_id(0)
        p = pl.program_id(1)

        @pl.when(jnp.logical_and(i == 0, p == 0))
        def _():
            dg2_ref[...] = jnp.zeros_like(dg2_ref)

        @pl.when(p == 0)
        def _():
            _, _, h2 = _rms_fwd(x1_ref[...], g2_ref[...])
            h2t_ref[...] = h2.T.astype(BF16)
            acc_scr[...] = jnp.zeros_like(acc_scr)

        rz = jnp.maximum(z_ref[...].astype(F32), 0.0)
        ft_ref[...] = (rz * rz).T.astype(BF16)
        df = _dot_nt(dx2b_ref[...], w2_ref[...])
        dz = (df * (2.0 * rz)).astype(BF16)
        dz_ref[...] = dz
        acc_scr[...] += _dot_nt(dz, w1_ref[...])

        @pl.when(p == N_CHIPS - 1)
        def _():
            g = g2_ref[...]
            r, xh, _ = _rms_fwd(x1_ref[...], g)
            dh2 = acc_scr[...]
            dg2_ref[...] += jnp.sum(dh2 * xh, axis=0, keepdims=True)
            dx1 = dx2_ref[...] + _rms_bwd(dh2, xh, r, g)
            dx1_ref[...] = dx1
            dx1b_ref[...] = dx1.astype(BF16)

    return _call(
        body, f"ffn_bwd_l{l}", (n_t, N_CHIPS),
        in_specs=[
            pl.BlockSpec((tm, D), lambda i, p: (i, 0)),
            pl.BlockSpec((tm, D), lambda i, p: (i, 0)),
            pl.BlockSpec((tm, D), lambda i, p: (i, 0)),
            pl.BlockSpec((tm, ffb), lambda i, p: (i, p)),
            pl.BlockSpec(g2.shape, lambda i, p: (0, 0)),
            pl.BlockSpec((None, D, ffb), lambda i, p: (p, 0, 0)),
            pl.BlockSpec((None, ffb, D), lambda i, p: (p, 0, 0)),
        ],
        out_specs=[
            pl.BlockSpec((tm, D), lambda i, p: (i, 0)),
            pl.BlockSpec((tm, D), lambda i, p: (i, 0)),
            pl.BlockSpec((tm, ffb), lambda i, p: (i, p)),
            pl.BlockSpec((ffb, tm), lambda i, p: (p, i)),
            pl.BlockSpec((D, tm), lambda i, p: (0, i)),
            pl.BlockSpec((1, D), lambda i, p: (0, 0)),
        ],
        out_shape=[jax.ShapeDtypeStruct((T, D), F32), jax.ShapeDtypeStruct((T, D), BF16),
                   jax.ShapeDtypeStruct((T, N_CHIPS * ffb), BF16), jax.ShapeDtypeStruct((N_CHIPS * ffb, T), BF16),
                   jax.ShapeDtypeStruct((D, T), BF16), jax.ShapeDtypeStruct((1, D), F32)],
        scratch_shapes=[pltpu.VMEM((tm, D), F32)],
        sem=("arbitrary", "arbitrary"), operands=(dx2, dx2b, x1, z, g2, w1_g, w2_g), comm=comm)


def _mix_bwd_call(l, dx1, dx1b, x, proj, cpre, g1, win_g, lng, lnb, ws, wst, bfull, cw, cb, cg, cbeta, wout_g,
                  comm=None):
    T, D = x.shape
    tm = min(TM_MIX, T)
    n_t = T // tm
    da = win_g.shape[-1]
    heads = da // CHUNK
    wo_rows = wout_g.shape[1]
    halo_blocks = tm // HALO

    def body(dx1_ref, dx1b_ref, x_ref, proj_ref, vbh_ref, gbh_ref, cpre_ref, g1_ref, win_ref, lng_ref, lnb_ref, ws_ref, wst_ref,
             bf_ref, cw_ref, cb_ref, cg_ref, cbeta_ref, wout_ref,
             dx_ref, dxb_ref, dproj_ref, h1t_ref, mixt_ref,
             dg1_ref, dlng_ref, dlnb_ref, dws_ref, dbs_ref, dcw_ref, dcb_ref, dcg_ref, dcbeta_ref,
             carry_scr, mixed_scr, dv_scr):
        i = pl.program_id(0)
        tile = n_t - 1 - i

        @pl.when(i == 0)
        def _():
            carry_scr[...] = jnp.zeros_like(carry_scr)
            for ref in (dg1_ref, dlng_ref, dlnb_ref, dws_ref, dbs_ref, dcw_ref, dcb_ref, dcg_ref, dcbeta_ref):
                ref[...] = jnp.zeros_like(ref)

        g1v = g1_ref[...]
        r, xh, h1 = _rms_fwd(x_ref[...], g1v)
        h1t_ref[...] = h1.T.astype(BF16)

        ua = proj_ref[:, 0 * da:1 * da]
        va = proj_ref[:, 1 * da:2 * da]
        vb = proj_ref[:, 2 * da:3 * da]
        gb = proj_ref[:, 3 * da:4 * da]
        g_halo = jnp.where(tile > 0, vbh_ref[...] * _sigmoid(gbh_ref[...]), 0.0)
        lng_v, cg_v = lng_ref[...], cg_ref[...]
        f = _mix_forward(ua, va, vb, gb, g_halo, ws_ref, bf_ref, lng_v, lnb_ref[...], cw_ref, cb_ref[...], cg_v,
                         cbeta_ref[...], mixed_scr, cpre=cpre_ref[...])
        mixt_ref[0:da, :] = f["a"].T.astype(BF16)
        mixt_ref[da:2 * da, :] = f["cout"].T.astype(BF16)

        dxo = dx1b_ref[...]
        dmix = jnp.concatenate([_dot_nt(dxo, wout_ref[q]) for q in range(N_CHIPS)], axis=1)
        da_ = dmix[:, :da]
        dc_ = dmix[:, da:]

        dua = da_ * f["mixed"] * f["du_fac"]
        dmixed = (da_ * f["u"]).astype(BF16)
        mask_t = (lax.broadcasted_iota(jnp.int32, (CHUNK, CHUNK), 1)
                  >= lax.broadcasted_iota(jnp.int32, (CHUNK, CHUNK), 0))
        wmt =[jnp.where(mask_t, wst_ref[h], 0.0).astype(BF16) for h in range(heads)]
        mask = _tril_mask()
        v16 = f["v16"]
        for h in range(heads):
            cols = slice(h * CHUNK, (h + 1) * CHUNK)
            dws_h = jnp.zeros((CHUNK, CHUNK), F32)
            dbs_h = jnp.zeros((CHUNK, CHUNK), F32)
            for ci in range(tm // CHUNK):
                rows = slice(ci * CHUNK, (ci + 1) * CHUNK)
                dm = dmixed[rows, cols]
                dv_scr[rows, cols] = jnp.dot(wmt[h], dm, preferred_element_type=F32)
                dws_h = dws_h + _dot_nt(dm, v16[rows, cols])
                dbs_h = dbs_h + dm.astype(F32)
            dws_ref[h] += jnp.where(mask, dws_h, 0.0)
            dbs_ref[h] += jnp.broadcast_to(jnp.sum(dbs_h, axis=1, keepdims=True), (CHUNK, CHUNK))
        dv = dv_scr[...]
        dlng_ref[...] += jnp.sum(dv * f["vh"], axis=0, keepdims=True)
        dlnb_ref[...] += jnp.sum(dv, axis=0, keepdims=True)
        dva = _ln_bwd(dv * lng_v, f["vh"], f["v_rs"]) * f["dvg_fac"]

        cn, sc = f["cn"], f["sc"]
        dcn = dc_ * (sc * (1.0 + cn * (1.0 - sc)))
        dcg_ref[...] += jnp.sum(dcn * f["chh"], axis=0, keepdims=True)
        dcbeta_ref[...] += jnp.sum(dcn, axis=0, keepdims=True)
        dcpre = _group_ln_bwd(dcn * cg_v, f["chh"], f["c_rss"])
        dcb_ref[...] += jnp.sum(dcpre, axis=0, keepdims=True)
        d_shifts = _sublane_shifts(jnp.concatenate([dcpre, carry_scr[...]], axis=0))
        for k in range(CONV_TAPS):
            g_k = _rows_from(f["g_shifts"], HALO - (CONV_TAPS - 1) + k, tm)
            dcw_ref[k:k + 1, :] += jnp.sum(dcpre * g_k, axis=0, keepdims=True)
        dg = _conv_taps(d_shifts, [CONV_TAPS - 1 - k for k in range(CONV_TAPS)], cw_ref, tm)
        carry_scr[...] = dcpre[:HALO, :]
        sg = f["sg"]
        dvb = dg * sg
        dgb = dg * vb * sg * (1.0 - sg)

        dps = [dua.astype(BF16), dva.astype(BF16), dvb.astype(BF16), dgb.astype(BF16)]
        dh1 = jnp.zeros((tm, D), F32)
        for p in range(N_CHIPS):
            dproj_ref[:, p * da:(p + 1) * da] = dps[p]
            dh1 = dh1 + _dot_nt(dps[p], win_ref[p])
        dg1_ref[...] += jnp.sum(dh1 * xh, axis=0, keepdims=True)
        dx = dx1_ref[...] + _rms_bwd(dh1, xh, r, g1v)
        dx_ref[...] = dx
        dxb_ref[...] = dx.astype(BF16)

    rev = lambda i: (n_t - 1 - i, 0)
    full = lambda a: pl.BlockSpec(a.shape, lambda i: (0,) * a.ndim)
    acc = lambda shape: pl.BlockSpec(shape, lambda i: (0,) * len(shape))
    halo_idx = lambda col: (lambda i: (jnp.maximum((n_t - 1 - i) * halo_blocks - 1, 0), col))
    small_shapes = [(1, D), (1, da), (1, da), (heads, CHUNK, CHUNK), (heads, CHUNK, CHUNK), (HALO, da),
                    (1, da), (1, da), (1, da)]
    return _call(
        body, f"mix_bwd_l{l}", (n_t,),
        in_specs=[
            pl.BlockSpec((tm, D), rev),
            pl.BlockSpec((tm, D), rev),
            pl.BlockSpec((tm, D), rev),
            pl.BlockSpec((tm, N_CHIPS * da), rev),
            pl.BlockSpec((HALO, da), halo_idx(2)),
            pl.BlockSpec((HALO, da), halo_idx(3)),
            pl.BlockSpec((tm, da), rev),
            full(g1),
            pl.BlockSpec((N_CHIPS, D, da), lambda i: (0, 0, 0)),
            full(lng), full(lnb), full(ws), full(wst), full(bfull), full(cw), full(cb), full(cg), full(cbeta),
            pl.BlockSpec((N_CHIPS, wo_rows, D), lambda i: (0, 0, 0)),
        ],
        out_specs=[
            pl.BlockSpec((tm, D), rev),
            pl.BlockSpec((tm, D), rev),
            pl.BlockSpec((tm, N_CHIPS * da), rev),
            pl.BlockSpec((D, tm), lambda i: (0, n_t - 1 - i)),
            pl.BlockSpec((2 * da, tm), lambda i: (0, n_t - 1 - i)),
        ] + [acc(s) for s in small_shapes],
        out_shape=[jax.ShapeDtypeStruct((T, D), F32), jax.ShapeDtypeStruct((T, D), BF16),
                   jax.ShapeDtypeStruct((T, N_CHIPS * da), BF16), jax.ShapeDtypeStruct((D, T), BF16),
                   jax.ShapeDtypeStruct((2 * da, T), BF16)] + [jax.ShapeDtypeStruct(s, F32) for s in small_shapes],
        scratch_shapes=[pltpu.VMEM((HALO, da), F32), pltpu.VMEM((tm, da), F32), pltpu.VMEM((tm, da), F32)],
        sem=("arbitrary",),
        operands=(dx1, dx1b, x, proj, proj, proj, cpre, g1, win_g, lng, lnb, ws, wst, bfull, cw, cb, cg, cbeta, wout_g),
        comm=comm)


def _wgrad_call(name, at, b, split_cols, steps):
    M, T = at.shape
    N = b.shape[1]
    if split_cols:
        bn = N // (N_CHIPS * steps)
        in_specs = [pl.BlockSpec((M, T), lambda j: (0, 0)), pl.BlockSpec((T, bn), lambda j: (0, j))]
        out_spec = pl.BlockSpec((None, M, bn), lambda j: (j // steps, 0, j % steps))
        out_shape = jax.ShapeDtypeStruct((N_CHIPS, M, N // N_CHIPS), BF16)
    else:
        bm = M // (N_CHIPS * steps)
        in_specs = [pl.BlockSpec((bm, T), lambda j: (j, 0)), pl.BlockSpec((T, N), lambda j: (0, 0))]
        out_spec = pl.BlockSpec((None, bm, N), lambda j: (j // steps, j % steps, 0))
        out_shape = jax.ShapeDtypeStruct((N_CHIPS, M // N_CHIPS, N), BF16)

    def body(at_ref, b_ref, o_ref):
        o_ref[...] = jnp.dot(at_ref[...], b_ref[...], preferred_element_type=F32).astype(BF16)

    return pl.pallas_call(
        body, name=name, grid=(N_CHIPS * steps,), in_specs=in_specs, out_specs=out_spec, out_shape=out_shape,
        compiler_params=_params(("arbitrary",)),
    )(at, b)


def _rows_block(rows, cols):
    br = rows
    while br * cols * 4 > 2 ** 20 and br % 16 == 0:
        br //= 2
    return br


def _cast_own_call(name, chip, w, l):
    _, rows, cols = w.shape
    br = _rows_block(rows, cols)
    n_b = rows // br

    def body(chip_ref, w_ref, o_ref):
        o_ref[...] = w_ref[...].astype(BF16)

    return pl.pallas_call(
        body, name=name,
        grid_spec=pltpu.PrefetchScalarGridSpec(
            num_scalar_prefetch=1, grid=(n_b,),
            in_specs=[pl.BlockSpec((None, br, cols), lambda i, chip_ref: (l, i, 0))],
            out_specs=pl.BlockSpec((None, br, cols), lambda i, chip_ref: (chip_ref[0], i, 0))),
        out_shape=jax.ShapeDtypeStruct((N_CHIPS, rows, cols), BF16),
        compiler_params=_params(("arbitrary",)),
    )(chip, w)


def _pair_sum_call(name, core, g, got):
    _, rh, cols = got.shape
    br = _rows_block(rh, cols)
    n_b = rh // br

    def body(core_ref, g_ref, got_ref, o_ref):
        o_ref[...] = (g_ref[...].astype(F32) + got_ref[...].astype(F32)).astype(BF16)

    half = pl.BlockSpec((None, br, cols), lambda q, i, core_ref: (q, i, 0))
    return pl.pallas_call(
        body, name=name,
        grid_spec=pltpu.PrefetchScalarGridSpec(
            num_scalar_prefetch=1, grid=(N_CHIPS, n_b),
            in_specs=[pl.BlockSpec((None, br, cols), lambda q, i, core_ref: (q, core_ref[0] * n_b + i, 0)), half],
            out_specs=half),
        out_shape=jax.ShapeDtypeStruct(got.shape, BF16),
        compiler_params=_params(("arbitrary", "arbitrary")),
    )(core, g, got)


def _chip_sum_call(name, core, parts):
    _, rh, cols = parts.shape
    br = _rows_block(rh, cols)
    n_b = rh // br

    def body(core_ref, *refs):
        o_ref = refs[N_CHIPS]
        total = refs[0][...].astype(F32)
        for q in range(1, N_CHIPS):
            total = total + refs[q][...].astype(F32)
        o_ref[...] = total

    return pl.pallas_call(
        body, name=name,
        grid_spec=pltpu.PrefetchScalarGridSpec(
            num_scalar_prefetch=1, grid=(n_b,),
            in_specs=[pl.BlockSpec((None, br, cols), functools.partial(lambda i, core_ref, q: (q, i, 0), q=q))
                      for q in range(N_CHIPS)],
            out_specs=pl.BlockSpec((br, cols), lambda i, core_ref: (core_ref[0] * n_b + i, 0))),
        out_shape=jax.ShapeDtypeStruct((2 * rh, cols), F32),
        compiler_params=_params(("arbitrary",)),
    )(core, *([parts] * N_CHIPS))


def _adamw_layers_call(name, w, g0, g1, m, v, comm=None):
    _, rows, cols = w.shape
    br = _rows_block(rows, cols)

    def body(w_ref, g0_ref, g1_ref, m_ref, v_ref, g_ref, d_ref, nm_ref, nv_ref):
        gv = jnp.where(pl.program_id(0) == 0, g0_ref[...], g1_ref[...])
        g_ref[...] = gv
        m_new = ADAM_B1 * m_ref[...] + (1.0 - ADAM_B1) * gv
        v_new = ADAM_B2 * v_ref[...] + (1.0 - ADAM_B2) * (gv * gv)
        m_hat = m_new / (1.0 - ADAM_B1 ** ADAM_STEP)
        v_hat = v_new / (1.0 - ADAM_B2 ** ADAM_STEP)
        d_ref[...] = -ADAM_LR * (m_hat / (jnp.sqrt(v_hat) + ADAM_EPS) + ADAM_WD * w_ref[...])
        nm_ref[...] = m_new
        nv_ref[...] = v_new

    both = pl.BlockSpec((None, br, cols), lambda l, i: (l, i, 0))
    one = pl.BlockSpec((br, cols), lambda l, i: (i, 0))
    return _call(body, name, (2, rows // br), in_specs=[both, one, one, both, both], out_specs=[both] * 4,
                 out_shape=[jax.ShapeDtypeStruct(w.shape, F32)] * 4, scratch_shapes=[],
                 sem=("arbitrary", "arbitrary"), operands=(w, g0, g1, m, v), comm=comm)


def _adamw_call(name, w, g, m, v):
    rows, cols = w.shape
    br = _rows_block(rows, cols)

    def body(w_ref, g_ref, m_ref, v_ref, d_ref, nm_ref, nv_ref):
        gv = g_ref[...]
        m_new = ADAM_B1 * m_ref[...] + (1.0 - ADAM_B1) * gv
        v_new = ADAM_B2 * v_ref[...] + (1.0 - ADAM_B2) * (gv * gv)
        m_hat = m_new / (1.0 - ADAM_B1 ** ADAM_STEP)
        v_hat = v_new / (1.0 - ADAM_B2 ** ADAM_STEP)
        d_ref[...] = -ADAM_LR * (m_hat / (jnp.sqrt(v_hat) + ADAM_EPS) + ADAM_WD * w_ref[...])
        nm_ref[...] = m_new
        nv_ref[...] = v_new

    spec = pl.BlockSpec((br, cols), lambda i: (i, 0))
    return pl.pallas_call(
        body, name=name, grid=(rows // br,), in_specs=[spec] * 4, out_specs=[spec] * 3,
        out_shape=[jax.ShapeDtypeStruct((rows, cols), F32)] * 3,
        compiler_params=_params(("arbitrary",)),
    )(w, g, m, v)


def _place():
    x, y, c = lax.axis_index("x"), lax.axis_index("y"), lax.axis_index("c")
    chips = [(1 - x, y), (x, 1 - y), (1 - x, 1 - y)]
    return x, y, c, 2 * x + y, chips


def _half_rows(ref, core):
    rh = ref.shape[-2] // 2
    rows = pl.ds(pl.multiple_of(core * rh, rh), rh)
    return ref.at[rows] if len(ref.shape) == 2 else ref.at[:, rows]


def _gather_comm(bufs, cw_shard=None):
    n = len(bufs)
    with_cw = cw_shard is not None

    def make(c_ins, c_outs, sems):
        ins, outs = c_ins[:n], c_outs[:n]
        send_sems, recv_sems, cw_send, cw_recv, cw_local = sems
        x, y, c, p, chips = _place()
        sibling = (x, y, 1 - c)
        qs = [2 * cx + cy for cx, cy in chips]
        hops = len(chips)

        def remote(k, j, src, dst, to):
            return pltpu.make_async_remote_copy(src_ref=src, dst_ref=dst, send_sem=send_sems.at[k, j],
                                                recv_sem=recv_sems.at[k, j], device_id=to, device_id_type=MESH)

        def cw_copy(j, dst, to):
            return pltpu.make_async_remote_copy(src_ref=c_ins[n], dst_ref=dst, send_sem=cw_send.at[j],
                                                recv_sem=cw_recv.at[j], device_id=to, device_id_type=MESH)

        def over_ici():
            return [remote(k, j, _half_rows(ins[k].at[p], c), _half_rows(outs[k].at[p], c), (*chip, c))
                    for j, chip in enumerate(chips) for k in range(n)]

        def passed_on():
            return [remote(k, hops + j, _half_rows(outs[k].at[qs[j]], c), _half_rows(outs[k].at[qs[j]], c), sibling)
                    for j in range(hops) for k in range(n)]

        def cw_copies():
            return [cw_copy(j, c_outs[n].at[p], (*chip, c)) for j, chip in enumerate(chips)]

        def cw_own():
            return pltpu.make_async_copy(c_ins[n], c_outs[n].at[p], cw_local)

        def start():
            for cp in over_ici():
                cp.start()
            if with_cw:
                cw_own().start()
                for cp in cw_copies():
                    cp.start()

        def finish():
            for j in range(hops):
                for k in range(n):
                    landed = _half_rows(outs[k].at[qs[j]], c)
                    remote(k, j, landed, landed, sibling).wait_recv()
                    remote(k, hops + j, landed, landed, sibling).start()
            for j in range(hops):
                for k in range(n):
                    other = _half_rows(outs[k].at[qs[j]], 1 - c)
                    remote(k, hops + j, other, other, sibling).wait_recv()
            if with_cw:
                for j in range(hops):
                    cw_copy(j, c_outs[n].at[qs[j]], sibling).wait_recv()
                cw_own().wait()
                for cp in cw_copies():
                    cp.wait_send()
            for cp in over_ici() + passed_on():
                cp.wait_send()

        return start, finish

    out_shape = [jax.ShapeDtypeStruct(b.shape, b.dtype) for b in bufs]
    if with_cw:
        out_shape.append(jax.ShapeDtypeStruct((N_CHIPS,) + cw_shard.shape, cw_shard.dtype))
    return dict(operands=list(bufs) + ([cw_shard] if with_cw else []), out_shape=out_shape,
                aliases={k: k for k in range(n)}, make=make,
                scratch=[pltpu.SemaphoreType.DMA((n, 6)), pltpu.SemaphoreType.DMA((n, 6)),
                         pltpu.SemaphoreType.DMA((3,)), pltpu.SemaphoreType.DMA((3,)), pltpu.SemaphoreType.DMA])


def _pair_exchange_call(l, gs):
    n = len(gs)

    def body(*refs):
        g, got = refs[:n], refs[n:2 * n]
        send_sems, recv_sems = refs[2 * n:]
        x, y, c, _, _ = _place()
        sibling = (x, y, 1 - c)
        copies = [pltpu.make_async_remote_copy(src_ref=_half_rows(g[k], 1 - c), dst_ref=got[k],
                                               send_sem=send_sems.at[k], recv_sem=recv_sems.at[k],
                                               device_id=sibling, device_id_type=MESH) for k in range(n)]
        for cp in copies:
            cp.start()
        for cp in copies:
            cp.wait_send()
            cp.wait_recv()

    return pl.pallas_call(
        body, name=f"grad_pair_exchange_l{l}",
        in_specs=[ANY] * n, out_specs=[ANY] * n,
        out_shape=[jax.ShapeDtypeStruct((g.shape[0], g.shape[1] // 2, g.shape[2]), g.dtype) for g in gs],
        scratch_shapes=[pltpu.SemaphoreType.DMA((n,)), pltpu.SemaphoreType.DMA((n,))],
    )(*gs)


def _exchange_comm(sums):
    n = len(sums)

    def make(ins, outs, sems):
        send_sems, recv_sems, local_sems = sems
        x, y, c, p, chips = _place()
        qs = [2 * cx + cy for cx, cy in chips]

        def remote(k, j, src, dst, to):
            return pltpu.make_async_remote_copy(src_ref=src, dst_ref=dst, send_sem=send_sems.at[k, j],
                                                recv_sem=recv_sems.at[k, j], device_id=to, device_id_type=MESH)

        def own():
            return [pltpu.make_async_copy(ins[k].at[p], outs[k].at[p], local_sems.at[k]) for k in range(n)]

        def sent():
            return [remote(k, j, ins[k].at[qs[j]], outs[k].at[p], (*chip, c))
                    for j, chip in enumerate(chips) for k in range(n)]

        def start():
            for cp in own() + sent():
                cp.start()

        def finish():
            for j in range(len(chips)):
                for k in range(n):
                    remote(k, j, ins[k].at[qs[j]], outs[k].at[qs[j]], (x, y, c)).wait_recv()
            for cp in sent():
                cp.wait_send()
            for cp in own():
                cp.wait()

        return start, finish

    return dict(operands=list(sums), out_shape=[jax.ShapeDtypeStruct(s.shape, s.dtype) for s in sums], aliases={},
                make=make, scratch=[pltpu.SemaphoreType.DMA((n, 3)), pltpu.SemaphoreType.DMA((n, 3)),
                                    pltpu.SemaphoreType.DMA((n,))])


def _pair_gather_call(l, halves):
    n = len(halves)

    def body(*refs):
        ins, outs = refs[:n], refs[n:2 * n]
        send_sems, recv_sems = refs[2 * n:]
        x, y, c, _, _ = _place()
        sibling = (x, y, 1 - c)

        def remote(k, src, dst):
            return pltpu.make_async_remote_copy(src_ref=src, dst_ref=dst, send_sem=send_sems.at[k],
                                                recv_sem=recv_sems.at[k], device_id=sibling, device_id_type=MESH)

        sent = [remote(k, _half_rows(ins[k], c), _half_rows(outs[k], c)) for k in range(n)]
        for cp in sent:
            cp.start()
        for k in range(n):
            other = _half_rows(outs[k], 1 - c)
            remote(k, other, other).wait_recv()
        for cp in sent:
            cp.wait_send()

    return pl.pallas_call(
        body, name=f"grad_pair_gather_l{l}",
        in_specs=[ANY] * n, out_specs=[ANY] * n,
        out_shape=[jax.ShapeDtypeStruct(h.shape, h.dtype) for h in halves],
        input_output_aliases={k: k for k in range(n)},
        scratch_shapes=[pltpu.SemaphoreType.DMA((n,)), pltpu.SemaphoreType.DMA((n,))],
    )(*halves)


def _all_sum_small_call(block, comm=None):
    m_per, n = block.shape

    def body(x_ref, sum_ref, all_ref, send_sems, recv_sems, local_sem):
        x, y, c, _, chip_list = _place()
        me, sibling = (x, y, c), (x, y, 1 - c)

        def rows(px, py, pc):
            return all_ref.at[pl.ds((4 * px + 2 * py + pc) * m_per, m_per), :]

        def copy(k, blk, to, src=None):
            return pltpu.make_async_remote_copy(src_ref=rows(*blk) if src is None else src, dst_ref=rows(*blk),
                                                send_sem=send_sems.at[k], recv_sem=recv_sems.at[k],
                                                device_id=to, device_id_type=MESH)

        mine = pltpu.make_async_copy(x_ref, rows(*me), local_sem)
        mine.start()
        first = [copy(0, me, sibling, src=x_ref)]
        first += [copy(1 + j, me, (*chip, c), src=x_ref) for j, chip in enumerate(chip_list)]
        for cp in first:
            cp.start()
        passed = [copy(4 + j, (*chip, c), sibling) for j, chip in enumerate(chip_list)]
        for j, chip in enumerate(chip_list):
            copy(1 + j, (*chip, c), me).wait_recv()
            passed[j].start()
        copy(0, sibling, me).wait_recv()
        for j, chip in enumerate(chip_list):
            copy(4 + j, (*chip, 1 - c), me).wait_recv()
        for cp in first + passed:
            cp.wait_send()
        mine.wait()
        total = all_ref[0:m_per, :]
        for d in range(1, N_DEV):
            total = total + all_ref[d * m_per:(d + 1) * m_per, :]
        sum_ref[...] = total

    vmem = pl.BlockSpec(memory_space=pltpu.VMEM)
    res, c_res = _call(
        body, "small_all_sum", (), in_specs=[vmem], out_specs=[vmem, vmem],
        out_shape=[jax.ShapeDtypeStruct((m_per, n), F32), jax.ShapeDtypeStruct((N_DEV * m_per, n), F32)],
        scratch_shapes=[pltpu.SemaphoreType.DMA((7,)), pltpu.SemaphoreType.DMA((7,)), pltpu.SemaphoreType.DMA],
        sem=None, operands=(block,), comm=comm)
    return res[0], c_res


SMALL_NAMES = ["norm1_g", "sgu_ln_g", "sgu_ln_b", "sgu_w", "sgu_b", "conv_b", "conv_ln_g", "conv_ln_b", "norm2_g",
               "final_g"]
MIX_NAMES = ["w_in", "w_out"]
FFN_NAMES = ["w_ff1", "w_ff2"]
BIG_NAMES = MIX_NAMES + FFN_NAMES


def _rows128(a):
    return a.reshape(-1, LANES)


def kernel(x, norm1_g, w_in, sgu_ln_g, sgu_ln_b, sgu_w, sgu_b, conv_w, conv_b, conv_ln_g, conv_ln_b, w_out, norm2_g, w_ff1, w_ff2, final_g, loss_target, m_norm1_g, m_w_in, m_sgu_ln_g, m_sgu_ln_b, m_sgu_w, m_sgu_b, m_conv_w, m_conv_b, m_conv_ln_g, m_conv_ln_b, m_w_out, m_norm2_g, m_w_ff1, m_w_ff2, m_final_g, v_norm1_g, v_w_in, v_sgu_ln_g, v_sgu_ln_b, v_sgu_w, v_sgu_b, v_conv_w, v_conv_b, v_conv_ln_g, v_conv_ln_b, v_w_out, v_norm2_g, v_w_ff1, v_w_ff2, v_final_g):
    w = dict(norm1_g=norm1_g, w_in=w_in, sgu_ln_g=sgu_ln_g, sgu_ln_b=sgu_ln_b, sgu_w=sgu_w, sgu_b=sgu_b,
             conv_w=conv_w, conv_b=conv_b, conv_ln_g=conv_ln_g, conv_ln_b=conv_ln_b, w_out=w_out, norm2_g=norm2_g,
             w_ff1=w_ff1, w_ff2=w_ff2, final_g=final_g)
    m = dict(norm1_g=m_norm1_g, w_in=m_w_in, sgu_ln_g=m_sgu_ln_g, sgu_ln_b=m_sgu_ln_b, sgu_w=m_sgu_w, sgu_b=m_sgu_b,
             conv_w=m_conv_w, conv_b=m_conv_b, conv_ln_g=m_conv_ln_g, conv_ln_b=m_conv_ln_b, w_out=m_w_out,
             norm2_g=m_norm2_g, w_ff1=m_w_ff1, w_ff2=m_w_ff2, final_g=m_final_g)
    v = dict(norm1_g=v_norm1_g, w_in=v_w_in, sgu_ln_g=v_sgu_ln_g, sgu_ln_b=v_sgu_ln_b, sgu_w=v_sgu_w, sgu_b=v_sgu_b,
             conv_w=v_conv_w, conv_b=v_conv_b, conv_ln_g=v_conv_ln_g, conv_ln_b=v_conv_ln_b, w_out=v_w_out,
             norm2_g=v_norm2_g, w_ff1=v_w_ff1, w_ff2=v_w_ff2, final_g=v_final_g)
    depth = w_in.shape[0]
    assert depth == 2, "core c owns layer c of every gradient"
    T, D = x.shape[1], x.shape[2]
    heads = sgu_w.shape[1]
    da = heads * CHUNK
    core = lax.axis_index("c")
    chip = 2 * lax.axis_index("x") + lax.axis_index("y")

    core_arr = core.reshape(1).astype(jnp.int32)
    chip_arr = chip.reshape(1).astype(jnp.int32)
    cw_pad = jnp.pad(conv_w, ((0, 0), (0, HALO - CONV_TAPS), (0, 0)))
    own = [{name: _cast_own_call(f"cast_{name}_l{l}", chip_arr, w[name], l) for name in BIG_NAMES}
           for l in range(depth)]
    gather_of = lambda l, group: _gather_comm([own[l][name] for name in group])
    win_g, wout_g, cw_g = _alone("gather_mix_l0", _gather_comm([own[0][name] for name in MIX_NAMES], cw_pad))
    cw_full = jnp.transpose(cw_g, (1, 2, 0, 3)).reshape(depth, HALO, da)
    taps = cw_full[:, :CONV_TAPS].reshape(depth, CONV_TAPS, heads, CHUNK)
    cw_diag = (taps[..., None] * jnp.eye(CHUNK, dtype=F32)).transpose(0, 2, 1, 3, 4)
    cw_diag = cw_diag.reshape(depth, heads, CONV_TAPS * CHUNK, CHUNK).astype(BF16)

    ws_t = jnp.swapaxes(sgu_w, -1, -2)
    b_full = jnp.broadcast_to(sgu_b[..., None], sgu_w.shape)
    row = lambda a, l: a[l:l + 1]

    xs, projs, cpres, x1s, zs = [], [], [], [], []
    h = x.reshape(T, D)
    gathered = []
    w1_g = None
    for l in range(depth):
        xs.append(h)
        (x1, proj, cpre), late = _mix_fwd_call(
            l, h, row(norm1_g, l), win_g, row(sgu_ln_g, l), row(sgu_ln_b, l), sgu_w[l], b_full[l], cw_diag[l],
            row(conv_b, l), row(conv_ln_g, l), row(conv_ln_b, l), wout_g,
            comm=gather_of(l, FFN_NAMES if w1_g is None else FFN_NAMES[1:]))
        w1_g, w2_g = late if len(late) == 2 else (w1_g, late[0])
        gathered.append((win_g, wout_g, w1_g, w2_g))
        (h, z), nxt = _ffn_fwd_call(l, x1, row(norm2_g, l), w1_g, w2_g,
                                    comm=gather_of(l + 1, MIX_NAMES + FFN_NAMES[:1]) if l + 1 < depth else None)
        if nxt:
            win_g, wout_g, w1_g = nxt
        projs.append(proj)
        cpres.append(cpre)
        x1s.append(x1)
        zs.append(z)
    dx, dxb, loss, d_final_g = _loss_call(h, final_g.reshape(1, D), loss_target.reshape(T, D))

    big = {name: [None] * depth for name in BIG_NAMES}
    reduced = {name: [None] * depth for name in BIG_NAMES}
    small = {name: [None] * depth for name in SMALL_NAMES[:-1] + ["conv_w"]}

    def pair_sums(l, group):
        got = _pair_exchange_call(f"{group[0]}_l{l}", [big[name][l] for name in group])
        return [_pair_sum_call(f"pair_sum_{name}_l{l}", core_arr, big[name][l], recv) for name, recv in zip(group, got)]

    def finish_reduce(l, group, parts):
        halves = [_chip_sum_call(f"chip_sum_{name}_l{l}", core_arr, part) for name, part in zip(group, parts)]
        for name, full in zip(group, _pair_gather_call(f"{group[0]}_l{l}", halves)):
            reduced[name][l] = full

    pending = None
    for l in reversed(range(depth)):
        win_g, wout_g, w1_g, w2_g = gathered[l]
        (dx1, dx1b, dz, f_t, h2_t, dg2), parts = _ffn_bwd_call(
            l, dx, dxb, x1s[l], zs[l], row(norm2_g, l), w1_g, w2_g,
            comm=_exchange_comm(pending[2]) if pending else None)
        if pending:
            finish_reduce(pending[0], pending[1], parts)
        big["w_ff2"][l] = _wgrad_call(f"wgrad_ff2_l{l}", f_t, dxb, False, 2)
        big["w_ff1"][l] = _wgrad_call(f"wgrad_ff1_l{l}", h2_t, dz, True, 2)
        ffn_sums = pair_sums(l, FFN_NAMES)
        ((dx, dxb, dproj, h1_t, mix_t, dg1, dlng, dlnb, dws, dbs, dcw, dcb, dcg, dcbeta), parts) = _mix_bwd_call(
            l, dx1, dx1b, xs[l], projs[l], cpres[l], row(norm1_g, l), win_g, row(sgu_ln_g, l), row(sgu_ln_b, l), sgu_w[l],
            ws_t[l], b_full[l], cw_diag[l], row(conv_b, l), row(conv_ln_g, l), row(conv_ln_b, l), wout_g,
            comm=_exchange_comm(ffn_sums))
        finish_reduce(l, FFN_NAMES, parts)
        big["w_out"][l] = _wgrad_call(f"wgrad_out_l{l}", mix_t, dx1b, False, 1)
        big["w_in"][l] = _wgrad_call(f"wgrad_in_l{l}", h1_t, dproj, True, 1)
        pending = (l, MIX_NAMES, pair_sums(l, MIX_NAMES))
        small["norm1_g"][l] = dg1[0]
        small["sgu_ln_g"][l] = dlng[0]
        small["sgu_ln_b"][l] = dlnb[0]
        small["sgu_w"][l] = dws
        small["sgu_b"][l] = dbs[:, :, 0]
        small["conv_w"][l] = dcw[:CONV_TAPS]
        small["conv_b"][l] = dcb[0]
        small["conv_ln_g"][l] = dcg[0]
        small["conv_ln_b"][l] = dcbeta[0]
        small["norm2_g"][l] = dg2[0]
    grad_x = dx.reshape(x.shape)

    small_local = {name: jnp.stack(small[name]) for name in small}
    small_local["final_g"] = d_final_g[0]
    pieces = [_rows128(small_local[name]) for name in SMALL_NAMES]
    pieces.append(_rows128(small_local["conv_w"]))
    pieces.append(jnp.broadcast_to(loss, (8, LANES)))
    offsets = [0]
    for piece in pieces:
        offsets.append(offsets[-1] + piece.shape[0])
    summed, parts = _all_sum_small_call(jnp.concatenate(pieces, axis=0), comm=_exchange_comm(pending[2]))
    finish_reduce(pending[0], pending[1], parts)
    n_small = offsets[len(SMALL_NAMES)]
    loss_out = summed[offsets[-2], 0]
    small_grads = {name: summed[offsets[k]:offsets[k + 1]].reshape(w[name].shape)
                   for k, name in enumerate(SMALL_NAMES)}
    conv_w_full = summed[offsets[-3]:offsets[-2]].reshape(depth, CONV_TAPS, da)
    conv_w_grad = lax.dynamic_slice_in_dim(conv_w_full, chip * conv_w.shape[-1], conv_w.shape[-1], axis=2)

    grads, delta, new_m, new_v = {}, {}, {}, {}
    for name in BIG_NAMES:
        (grads[name], delta[name], new_m[name], new_v[name]), _ = _adamw_layers_call(
            f"adamw_{name}", w[name], reduced[name][0], reduced[name][1], m[name], v[name])
    pack = lambda src: jnp.concatenate([_rows128(src[name]) for name in SMALL_NAMES], axis=0)
    d_, m_, v_ = _adamw_call("adamw_small", pack(w), summed[:n_small], pack(m), pack(v))
    for k, name in enumerate(SMALL_NAMES):
        sl = slice(offsets[k], offsets[k + 1])
        grads[name] = small_grads[name]
        delta[name] = d_[sl].reshape(w[name].shape)
        new_m[name] = m_[sl].reshape(w[name].shape)
        new_v[name] = v_[sl].reshape(w[name].shape)
    cshape = conv_w.shape
    flat = lambda a: a.reshape(-1, cshape[-1])
    d_, m_, v_ = _adamw_call("adamw_conv_w", flat(conv_w), flat(conv_w_grad), flat(m["conv_w"]), flat(v["conv_w"]))
    grads["conv_w"] = conv_w_grad
    delta["conv_w"], new_m["conv_w"], new_v["conv_w"] = d_.reshape(cshape), m_.reshape(cshape), v_.reshape(cshape)

    order = ["norm1_g", "w_in", "sgu_ln_g", "sgu_ln_b", "sgu_w", "sgu_b", "conv_w", "conv_b", "conv_ln_g",
             "conv_ln_b", "w_out", "norm2_g", "w_ff1", "w_ff2", "final_g"]
    return (loss_out, grad_x, *[grads[n] for n in order], *[delta[n] for n in order],
            *[new_m[n] for n in order], *[new_v[n] for n in order])
```

```python
import functools

import jax
import jax.numpy as jnp
from jax import lax
from jax.experimental import pallas as pl
from jax.experimental.pallas import tpu as pltpu

F32 = jnp.float32
BF16 = jnp.bfloat16
MESH = pl.DeviceIdType.MESH

EPS = 1e-6
CHUNK = 128
CONV_TAPS = 31
HALO = 32
N_CHIPS = 4
N_DEV = 8
LANES = 128
SUBLANES = 8

ADAM_LR = 0.001
ADAM_B1 = 0.9
ADAM_B2 = 0.999
ADAM_EPS = 1e-08
ADAM_WD = 0.01
ADAM_STEP = 10

TM_MIX = 256
TM_FFN = 512
TM_FFN_FWD = 1024
V7X_VMEM_BYTES = 64 * 2 ** 20
VMEM_LIMIT = V7X_VMEM_BYTES - 8 * 2 ** 20


def _params(sem=None):
    return pltpu.CompilerParams(dimension_semantics=sem, vmem_limit_bytes=VMEM_LIMIT)


def _gelu_and_grad(x):
    gauss = jnp.exp(-0.5 * x * x)
    t = 1.0 / (1.0 + (0.3275911 * 0.7071067811865476) * jnp.abs(x))
    poly = t * (0.254829592 + t * (-0.284496736 + t * (1.421413741 + t * (-1.453152027 + t * 1.061405429))))
    erf_abs = 1.0 - poly * gauss
    cdf = 0.5 * (1.0 + jnp.where(x < 0, -erf_abs, erf_abs))
    return x * cdf, cdf + x * (gauss * 0.3989422804014327)


def _sigmoid(x):
    return 1.0 / (1.0 + jnp.exp(-x))


def _dot(a, b):
    return jnp.dot(a.astype(BF16), b.astype(BF16), preferred_element_type=F32)


def _dot_nt(a, b):
    return lax.dot_general(a.astype(BF16), b.astype(BF16), (((1,), (1,)), ((), ())), preferred_element_type=F32)


def _rms_fwd(x, g):
    r = lax.rsqrt(jnp.mean(x * x, axis=-1, keepdims=True) + EPS)
    xh = x * r
    return r, xh, xh * g


def _rms_bwd(dh, xh, r, g):
    dxh = dh * g
    return r * (dxh - xh * jnp.mean(dxh * xh, axis=-1, keepdims=True))


def _ln_stats(x):
    mu = jnp.mean(x, axis=-1, keepdims=True)
    xc = x - mu
    rs = lax.rsqrt(jnp.mean(xc * xc, axis=-1, keepdims=True) + EPS)
    return xc * rs, rs


def _ln_bwd(dxh, xh, rs):
    return rs * (dxh - jnp.mean(dxh, axis=-1, keepdims=True) - xh * jnp.mean(dxh * xh, axis=-1, keepdims=True))


def _group_ln_fwd(c):
    parts, rss = [], []
    for j in range(c.shape[1] // CHUNK):
        xh, rs = _ln_stats(c[:, j * CHUNK:(j + 1) * CHUNK])
        parts.append(xh)
        rss.append(rs)
    return jnp.concatenate(parts, axis=1), rss


def _group_ln_bwd(dxh, xh, rss):
    parts = []
    for j, rs in enumerate(rss):
        cols = slice(j * CHUNK, (j + 1) * CHUNK)
        parts.append(_ln_bwd(dxh[:, cols], xh[:, cols], rs))
    return jnp.concatenate(parts, axis=1)


def _sublane_shifts(ext):
    n = ext.shape[0]
    return [ext if b == 0 else pltpu.roll(ext, n - b, 0) for b in range(SUBLANES)]


def _rows_from(shifts, off, tm):
    a, b = divmod(off, SUBLANES)
    return shifts[b][a * SUBLANES:a * SUBLANES + tm]


def _tril_mask():
    t = lax.broadcasted_iota(jnp.int32, (CHUNK, CHUNK), 0)
    s = lax.broadcasted_iota(jnp.int32, (CHUNK, CHUNK), 1)
    return t >= s


def _mix_forward(ua, va, vb, gb, g_halo, ws_ref, bfull_ref, lng, lnb, cw_ref, cb, cg, cbeta, mixed_scr, cpre=None):
    tm, da = ua.shape
    heads = da // CHUNK
    u, du_fac = _gelu_and_grad(ua)
    vg, dvg_fac = _gelu_and_grad(va)
    vh, v_rs = _ln_stats(vg)
    v = vh * lng + lnb
    mask = _tril_mask()
    wm = [jnp.where(mask, ws_ref[h], 0.0).astype(BF16) for h in range(heads)]
    vb16 = v.astype(BF16)
    for ci in range(tm // CHUNK):
        rows = slice(ci * CHUNK, (ci + 1) * CHUNK)
        for h in range(heads):
            cols = slice(h * CHUNK, (h + 1) * CHUNK)
            mixed_scr[rows, cols] = jnp.dot(wm[h], vb16[rows, cols], preferred_element_type=F32) + bfull_ref[h]
    mixed = mixed_scr[...]
    a = u * mixed

    sg = _sigmoid(gb)
    g = vb * sg
    g_shifts = _sublane_shifts(jnp.concatenate([g_halo, g], axis=0))
    if cpre is None:
        cpre = jnp.zeros_like(g) + cb
        for k in range(CONV_TAPS):
            cpre = cpre + _rows_from(g_shifts, HALO - (CONV_TAPS - 1) + k, tm) * cw_ref[k:k + 1, :]
    chh, c_rss = _group_ln_fwd(cpre)
    cn = chh * cg + cbeta
    sc = _sigmoid(cn)
    cout = cn * sc
    return dict(u=u, du_fac=du_fac, dvg_fac=dvg_fac, vh=vh, v_rs=v_rs, v16=vb16, mixed=mixed, a=a, sg=sg, g=g,
                g_shifts=g_shifts, cpre=cpre, chh=chh, c_rss=c_rss, cn=cn, sc=sc, cout=cout)


ANY = pl.BlockSpec(memory_space=pl.ANY)


def _call(body, name, grid, in_specs, out_specs, out_shape, scratch_shapes, sem, operands, comm=None):
    if comm is None:
        res = pl.pallas_call(body, name=name, grid=grid, in_specs=in_specs, out_specs=out_specs, out_shape=out_shape,
                             scratch_shapes=scratch_shapes, compiler_params=_params(sem))(*operands)
        return res, []
    n_in, n_out, n_scr = len(in_specs), len(out_specs), len(scratch_shapes)
    n_cin, n_cout = len(comm["operands"]), len(comm["out_shape"])

    def fused(*refs):
        ins, refs = refs[:n_in], refs[n_in:]
        c_ins, refs = refs[:n_cin], refs[n_cin:]
        outs, refs = refs[:n_out], refs[n_out:]
        c_outs, refs = refs[:n_cout], refs[n_cout:]
        scr, c_sems = refs[:n_scr], refs[n_scr:]
        start, finish = comm["make"](c_ins, c_outs, c_sems)
        if grid:
            first = functools.reduce(jnp.logical_and, [pl.program_id(a) == 0 for a in range(len(grid))])
            last = functools.reduce(jnp.logical_and, [pl.program_id(a) == grid[a] - 1 for a in range(len(grid))])
            pl.when(first)(start)
            body(*ins, *outs, *scr)
            pl.when(last)(finish)
        else:
            start()
            body(*ins, *outs, *scr)
            finish()

    res = pl.pallas_call(
        fused, name=name, grid=grid,
        in_specs=list(in_specs) + [ANY] * n_cin, out_specs=list(out_specs) + [ANY] * n_cout,
        out_shape=list(out_shape) + list(comm["out_shape"]),
        input_output_aliases={n_in + a: n_out + b for a, b in comm["aliases"].items()},
        scratch_shapes=list(scratch_shapes) + list(comm["scratch"]),
        compiler_params=_params(sem),
    )(*operands, *comm["operands"])
    return res[:n_out], res[n_out:]


def _alone(name, comm):
    n_cin, n_cout = len(comm["operands"]), len(comm["out_shape"])

    def body(*refs):
        start, finish = comm["make"](refs[:n_cin], refs[n_cin:n_cin + n_cout], refs[n_cin + n_cout:])
        start()
        finish()

    return pl.pallas_call(
        body, name=name, in_specs=[ANY] * n_cin, out_specs=[ANY] * n_cout, out_shape=list(comm["out_shape"]),
        input_output_aliases=dict(comm["aliases"]), scratch_shapes=list(comm["scratch"]),
    )(*comm["operands"])


def _mix_fwd_call(l, x, g1, win_g, lng, lnb, ws, bfull, cw, cb, cg, cbeta, wout_g, comm=None):
    T, D = x.shape
    tm = min(TM_MIX, T)
    n_t = T // tm
    da = win_g.shape[-1]
    wo_rows = wout_g.shape[1]

    def body(x_ref, g1_ref, win_ref, lng_ref, lnb_ref, ws_ref, bf_ref, cw_ref, cb_ref, cg_ref, cbeta_ref, wout_ref,
             x1_ref, proj_ref, cpre_ref, halo_scr, mixed_scr):
        i = pl.program_id(0)

        @pl.when(i == 0)
        def _():
            halo_scr[...] = jnp.zeros_like(halo_scr)

        x_t = x_ref[...]
        _, _, h1 = _rms_fwd(x_t, g1_ref[...])
        h1b = h1.astype(BF16)
        ps = []
        for p in range(N_CHIPS):
            pp = jnp.dot(h1b, win_ref[p], preferred_element_type=F32)
            proj_ref[:, p * da:(p + 1) * da] = pp
            ps.append(pp)
        f = _mix_forward(ps[0], ps[1], ps[2], ps[3], halo_scr[...], ws_ref, bf_ref, lng_ref[...], lnb_ref[...],
                         cw_ref, cb_ref[...], cg_ref[...], cbeta_ref[...], mixed_scr)
        halo_scr[...] = f["g"][tm - HALO:, :]
        cpre_ref[...] = f["cpre"]
        mix =jnp.concatenate([f["a"], f["cout"]], axis=1).astype(BF16)
        acc = x_t
        for q in range(N_CHIPS):
            acc = acc + jnp.dot(mix[:, q * wo_rows:(q + 1) * wo_rows], wout_ref[q], preferred_element_type=F32)
        x1_ref[...] = acc

    full = lambda a: pl.BlockSpec(a.shape, lambda i: (0,) * a.ndim)
    return _call(
        body, f"mix_fwd_l{l}", (n_t,),
        in_specs=[
            pl.BlockSpec((tm, D), lambda i: (i, 0)),
            full(g1),
            pl.BlockSpec((N_CHIPS, D, da), lambda i: (0, 0, 0)),
            full(lng), full(lnb), full(ws), full(bfull), full(cw), full(cb), full(cg), full(cbeta),
            pl.BlockSpec((N_CHIPS, wo_rows, D), lambda i: (0, 0, 0)),
        ],
        out_specs=[
            pl.BlockSpec((tm, D), lambda i: (i, 0)),
            pl.BlockSpec((tm, N_CHIPS * da), lambda i: (i, 0)),
            pl.BlockSpec((tm, da), lambda i: (i, 0)),
        ],
        out_shape=[jax.ShapeDtypeStruct((T, D), F32), jax.ShapeDtypeStruct((T, N_CHIPS * da), F32),
                   jax.ShapeDtypeStruct((T, da), F32)],
        scratch_shapes=[pltpu.VMEM((HALO, da), F32), pltpu.VMEM((tm, da), F32)],
        sem=("arbitrary",),
        operands=(x, g1, win_g, lng, lnb, ws, bfull, cw, cb, cg, cbeta, wout_g), comm=comm)


def _ffn_tile(T, tile=TM_FFN):
    return min(tile, max(T // 2, CHUNK))


def _ffn_fwd_call(l, x1, g2, w1_g, w2_g, comm=None):
    T, D = x1.shape
    tm = _ffn_tile(T, TM_FFN_FWD)
    n_t = T // tm
    ffb = w1_g.shape[-1]

    def body(x1_ref, g2_ref, w1_ref, w2_ref, x2_ref, z_ref, h2_scr, acc_scr):
        p = pl.program_id(1)

        @pl.when(p == 0)
        def _():
            _, _, h2 = _rms_fwd(x1_ref[...], g2_ref[...])
            h2_scr[...] = h2.astype(BF16)
            acc_scr[...] = jnp.zeros_like(acc_scr)

        z = jnp.dot(h2_scr[...], w1_ref[...], preferred_element_type=F32)
        z_ref[...] = z.astype(BF16)
        rz = jnp.maximum(z, 0.0)
        acc_scr[...] += jnp.dot((rz * rz).astype(BF16), w2_ref[...], preferred_element_type=F32)

        @pl.when(p == N_CHIPS - 1)
        def _():
            x2_ref[...] = x1_ref[...] + acc_scr[...]

    return _call(
        body, f"ffn_fwd_l{l}", (n_t, N_CHIPS),
        in_specs=[
            pl.BlockSpec((tm, D), lambda i, p: (i, 0)),
            pl.BlockSpec(g2.shape, lambda i, p: (0, 0)),
            pl.BlockSpec((None, D, ffb), lambda i, p: (p, 0, 0)),
            pl.BlockSpec((None, ffb, D), lambda i, p: (p, 0, 0)),
        ],
        out_specs=[
            pl.BlockSpec((tm, D), lambda i, p: (i, 0)),
            pl.BlockSpec((tm, ffb), lambda i, p: (i, p)),
        ],
        out_shape=[jax.ShapeDtypeStruct((T, D), F32), jax.ShapeDtypeStruct((T, N_CHIPS * ffb), BF16)],
        scratch_shapes=[pltpu.VMEM((tm, D), BF16), pltpu.VMEM((tm, D), F32)],
        sem=("arbitrary", "arbitrary"), operands=(x1, g2, w1_g, w2_g), comm=comm)


def _loss_call(x, gf, target):
    T, D = x.shape
    tm = min(TM_FFN, T)
    n_t = T // tm

    def body(x_ref, gf_ref, t_ref, dx_ref, dxb_ref, loss_ref, dgf_ref):
        i = pl.program_id(0)

        @pl.when(i == 0)
        def _():
            loss_ref[...] = jnp.zeros_like(loss_ref)
            dgf_ref[...] = jnp.zeros_like(dgf_ref)

        g = gf_ref[...]
        r, xh, y = _rms_fwd(x_ref[...], g)
        e = y - t_ref[...]
        per_tok = jnp.sum(e * e, axis=-1, keepdims=True) * (1.0 / D)
        loss_ref[...] += 0.5 * jnp.sum(per_tok, axis=0, keepdims=True)
        dy = e * (1.0 / D)
        dgf_ref[...] += jnp.sum(dy * xh, axis=0, keepdims=True)
        dx = _rms_bwd(dy, xh, r, g)
        dx_ref[...] = dx
        dxb_ref[...] = dx.astype(BF16)

    return pl.pallas_call(
        body, name="loss_head",
        grid=(n_t,),
        in_specs=[
            pl.BlockSpec((tm, D), lambda i: (i, 0)),
            pl.BlockSpec(gf.shape, lambda i: (0, 0)),
            pl.BlockSpec((tm, D), lambda i: (i, 0)),
        ],
        out_specs=[
            pl.BlockSpec((tm, D), lambda i: (i, 0)),
            pl.BlockSpec((tm, D), lambda i: (i, 0)),
            pl.BlockSpec((1, 1), lambda i: (0, 0)),
            pl.BlockSpec((1, D), lambda i: (0, 0)),
        ],
        out_shape=[jax.ShapeDtypeStruct((T, D), F32), jax.ShapeDtypeStruct((T, D), BF16),
                   jax.ShapeDtypeStruct((1, 1), F32), jax.ShapeDtypeStruct((1, D), F32)],
        compiler_params=_params(("arbitrary",)),
    )(x, gf, target)


def _ffn_bwd_call(l, dx2, dx2b, x1, z, g2, w1_g, w2_g, comm=None):
    T, D = x1.shape
    tm = _ffn_tile(T)
    n_t = T // tm
    ffb = w1_g.shape[-1]

    def body(dx2_ref, dx2b_ref, x1_ref, z_ref, g2_ref, w1_ref, w2_ref,
             dx1_ref, dx1b_ref, dz_ref, ft_ref, h2t_ref, dg2_ref, acc_scr):
        i = pl.program_id(0)
        p = pl.program_id(1)

        @pl.when(jnp.logical_and(i == 0, p == 0))
        def _():
            dg2_ref[...] = jnp.zeros_like(dg2_ref)

        @pl.when(p == 0)
        def _():
            _, _, h2 = _rms_fwd(x1_ref[...], g2_ref[...])
            h2t_ref[...] = h2.T.astype(BF16)
            acc_scr[...] = jnp.zeros_like(acc_scr)

        rz = jnp.maximum(z_ref[...].astype(F32), 0.0)
        ft_ref[...] = (rz * rz).T.astype(BF16)
        df = _dot_nt(dx2b_ref[...], w2_ref[...])
        dz = (df * (2.0 * rz)).astype(BF16)
        dz_ref[...] = dz
        acc_scr[...] += _dot_nt(dz, w1_ref[...])

        @pl.when(p == N_CHIPS - 1)
        def _():
            g = g2_ref[...]
            r, xh, _ = _rms_fwd(x1_ref[...], g)
            dh2 = acc_scr[...]
            dg2_ref[...] += jnp.sum(dh2 * xh, axis=0, keepdims=True)
            dx1 = dx2_ref[...] + _rms_bwd(dh2, xh, r, g)
            dx1_ref[...] = dx1
            dx1b_ref[...] = dx1.astype(BF16)

    return _call(
        body, f"ffn_bwd_l{l}", (n_t, N_CHIPS),
        in_specs=[
            pl.BlockSpec((tm, D), lambda i, p: (i, 0)),
            pl.BlockSpec((tm, D), lambda i, p: (i, 0)),
            pl.BlockSpec((tm, D), lambda i, p: (i, 0)),
            pl.BlockSpec((tm, ffb), lambda i, p: (i, p)),
            pl.BlockSpec(g2.shape, lambda i, p: (0, 0)),
            pl.BlockSpec((None, D, ffb), lambda i, p: (p, 0, 0)),
            pl.BlockSpec((None, ffb, D), lambda i, p: (p, 0, 0)),
        ],
        out_specs=[
            pl.BlockSpec((tm, D), lambda i, p: (i, 0)),
            pl.BlockSpec((tm, D), lambda i, p: (i, 0)),
            pl.BlockSpec((tm, ffb), lambda i, p: (i, p)),
            pl.BlockSpec((ffb, tm), lambda i, p: (p, i)),
            pl.BlockSpec((D, tm), lambda i, p: (0, i)),
            pl.BlockSpec((1, D), lambda i, p: (0, 0)),
        ],
        out_shape=[jax.ShapeDtypeStruct((T, D), F32), jax.ShapeDtypeStruct((T, D), BF16),
                   jax.ShapeDtypeStruct((T, N_CHIPS * ffb), BF16), jax.ShapeDtypeStruct((N_CHIPS * ffb, T), BF16),
                   jax.ShapeDtypeStruct((D, T), BF16), jax.ShapeDtypeStruct((1, D), F32)],
        scratch_shapes=[pltpu.VMEM((tm, D), F32)],
        sem=("arbitrary", "arbitrary"), operands=(dx2, dx2b, x1, z, g2, w1_g, w2_g), comm=comm)


def _mix_bwd_call(l, dx1, dx1b, x, proj, cpre, g1, win_g, lng, lnb, ws, wst, bfull, cw, cb, cg, cbeta, wout_g,
                  comm=None):
    T, D = x.shape
    tm = min(TM_MIX, T)
    n_t = T // tm
    da = win_g.shape[-1]
    heads = da // CHUNK
    wo_rows = wout_g.shape[1]
    halo_blocks = tm // HALO

    def body(dx1_ref, dx1b_ref, x_ref, proj_ref, vbh_ref, gbh_ref, cpre_ref, g1_ref, win_ref, lng_ref, lnb_ref, ws_ref, wst_ref,
             bf_ref, cw_ref, cb_ref, cg_ref, cbeta_ref, wout_ref,
             dx_ref, dxb_ref, dproj_ref, h1t_ref, mixt_ref,
             dg1_ref, dlng_ref, dlnb_ref, dws_ref, dbs_ref, dcw_ref, dcb_ref, dcg_ref, dcbeta_ref,
             carry_scr, mixed_scr, dv_scr):
        i = pl.program_id(0)
        tile = n_t - 1 - i

        @pl.when(i == 0)
        def _():
            carry_scr[...] = jnp.zeros_like(carry_scr)
            for ref in (dg1_ref, dlng_ref, dlnb_ref, dws_ref, dbs_ref, dcw_ref, dcb_ref, dcg_ref, dcbeta_ref):
                ref[...] = jnp.zeros_like(ref)

        g1v = g1_ref[...]
        r, xh, h1 = _rms_fwd(x_ref[...], g1v)
        h1t_ref[...] = h1.T.astype(BF16)

        ua = proj_ref[:, 0 * da:1 * da]
        va = proj_ref[:, 1 * da:2 * da]
        vb = proj_ref[:, 2 * da:3 * da]
        gb = proj_ref[:, 3 * da:4 * da]
        g_halo = jnp.where(tile > 0, vbh_ref[...] * _sigmoid(gbh_ref[...]), 0.0)
        lng_v, cg_v = lng_ref[...], cg_ref[...]
        f = _mix_forward(ua, va, vb, gb, g_halo, ws_ref, bf_ref, lng_v, lnb_ref[...], cw_ref, cb_ref[...], cg_v,
                         cbeta_ref[...], mixed_scr, cpre=cpre_ref[...])
        mixt_ref[0:da, :] = f["a"].T.astype(BF16)
        mixt_ref[da:2 * da, :] = f["cout"].T.astype(BF16)

        dxo = dx1b_ref[...]
        dmix = jnp.concatenate([_dot_nt(dxo, wout_ref[q]) for q in range(N_CHIPS)], axis=1)
        da_ = dmix[:, :da]
        dc_ = dmix[:, da:]

        dua = da_ * f["mixed"] * f["du_fac"]
        dmixed = (da_ * f["u"]).astype(BF16)
        mask_t = (lax.broadcasted_iota(jnp.int32, (CHUNK, CHUNK), 1)
                  >= lax.broadcasted_iota(jnp.int32, (CHUNK, CHUNK), 0))
        wmt =[jnp.where(mask_t, wst_ref[h], 0.0).astype(BF16) for h in range(heads)]
        mask = _tril_mask()
        v16 = f["v16"]
        for h in range(heads):
            cols = slice(h * CHUNK, (h + 1) * CHUNK)
            dws_h = jnp.zeros((CHUNK, CHUNK), F32)
            dbs_h = jnp.zeros((CHUNK, CHUNK), F32)
            for ci in range(tm // CHUNK):
                rows = slice(ci * CHUNK, (ci + 1) * CHUNK)
                dm = dmixed[rows, cols]
                dv_scr[rows, cols] = jnp.dot(wmt[h], dm, preferred_element_type=F32)
                dws_h = dws_h + _dot_nt(dm, v16[rows, cols])
                dbs_h = dbs_h + dm.astype(F32)
            dws_ref[h] += jnp.where(mask, dws_h, 0.0)
            dbs_ref[h] += jnp.broadcast_to(jnp.sum(dbs_h, axis=1, keepdims=True), (CHUNK, CHUNK))
        dv = dv_scr[...]
        dlng_ref[...] += jnp.sum(dv * f["vh"], axis=0, keepdims=True)
        dlnb_ref[...] += jnp.sum(dv, axis=0, keepdims=True)
        dva = _ln_bwd(dv * lng_v, f["vh"], f["v_rs"]) * f["dvg_fac"]

        cn, sc = f["cn"], f["sc"]
        dcn = dc_ * (sc * (1.0 + cn * (1.0 - sc)))
        dcg_ref[...] += jnp.sum(dcn * f["chh"], axis=0, keepdims=True)
        dcbeta_ref[...] += jnp.sum(dcn, axis=0, keepdims=True)
        dcpre = _group_ln_bwd(dcn * cg_v, f["chh"], f["c_rss"])
        dcb_ref[...] += jnp.sum(dcpre, axis=0, keepdims=True)
        d_shifts = _sublane_shifts(jnp.concatenate([dcpre, carry_scr[...]], axis=0))
        dg = jnp.zeros_like(dcpre)
        for k in range(CONV_TAPS):
            g_k = _rows_from(f["g_shifts"], HALO - (CONV_TAPS - 1) + k, tm)
            dcw_ref[k:k + 1, :] += jnp.sum(dcpre * g_k, axis=0, keepdims=True)
            dg = dg + _rows_from(d_shifts, CONV_TAPS - 1 - k, tm) * cw_ref[k:k + 1, :]
        carry_scr[...] = dcpre[:HALO, :]
        sg = f["sg"]
        dvb = dg * sg
        dgb = dg * vb * sg * (1.0 - sg)

        dps = [dua.astype(BF16), dva.astype(BF16), dvb.astype(BF16), dgb.astype(BF16)]
        dh1 = jnp.zeros((tm, D), F32)
        for p in range(N_CHIPS):
            dproj_ref[:, p * da:(p + 1) * da] = dps[p]
            dh1 = dh1 + _dot_nt(dps[p], win_ref[p])
        dg1_ref[...] += jnp.sum(dh1 * xh, axis=0, keepdims=True)
        dx = dx1_ref[...] + _rms_bwd(dh1, xh, r, g1v)
        dx_ref[...] = dx
        dxb_ref[...] = dx.astype(BF16)

    rev = lambda i: (n_t - 1 - i, 0)
    full = lambda a: pl.BlockSpec(a.shape, lambda i: (0,) * a.ndim)
    acc = lambda shape: pl.BlockSpec(shape, lambda i: (0,) * len(shape))
    halo_idx = lambda col: (lambda i: (jnp.maximum((n_t - 1 - i) * halo_blocks - 1, 0), col))
    small_shapes = [(1, D), (1, da), (1, da), (heads, CHUNK, CHUNK), (heads, CHUNK, CHUNK), (HALO, da),
                    (1, da), (1, da), (1, da)]
    return _call(
        body, f"mix_bwd_l{l}", (n_t,),
        in_specs=[
            pl.BlockSpec((tm, D), rev),
            pl.BlockSpec((tm, D), rev),
            pl.BlockSpec((tm, D), rev),
            pl.BlockSpec((tm, N_CHIPS * da), rev),
            pl.BlockSpec((HALO, da), halo_idx(2)),
            pl.BlockSpec((HALO, da), halo_idx(3)),
            pl.BlockSpec((tm, da), rev),
            full(g1),
            pl.BlockSpec((N_CHIPS, D, da), lambda i: (0, 0, 0)),
            full(lng), full(lnb), full(ws), full(wst), full(bfull), full(cw), full(cb), full(cg), full(cbeta),
            pl.BlockSpec((N_CHIPS, wo_rows, D), lambda i: (0, 0, 0)),
        ],
        out_specs=[
            pl.BlockSpec((tm, D), rev),
            pl.BlockSpec((tm, D), rev),
            pl.BlockSpec((tm, N_CHIPS * da), rev),
            pl.BlockSpec((D, tm), lambda i: (0, n_t - 1 - i)),
            pl.BlockSpec((2 * da, tm), lambda i: (0, n_t - 1 - i)),
        ] + [acc(s) for s in small_shapes],
        out_shape=[jax.ShapeDtypeStruct((T, D), F32), jax.ShapeDtypeStruct((T, D), BF16),
                   jax.ShapeDtypeStruct((T, N_CHIPS * da), BF16), jax.ShapeDtypeStruct((D, T), BF16),
                   jax.ShapeDtypeStruct((2 * da, T), BF16)] + [jax.ShapeDtypeStruct(s, F32) for s in small_shapes],
        scratch_shapes=[pltpu.VMEM((HALO, da), F32), pltpu.VMEM((tm, da), F32), pltpu.VMEM((tm, da), F32)],
        sem=("arbitrary",),
        operands=(dx1, dx1b, x, proj, proj, proj, cpre, g1, win_g, lng, lnb, ws, wst, bfull, cw, cb, cg, cbeta, wout_g),
        comm=comm)


def _wgrad_call(name, at, b, split_cols, steps):
    M, T = at.shape
    N = b.shape[1]
    if split_cols:
        bn = N // (N_CHIPS * steps)
        in_specs = [pl.BlockSpec((M, T), lambda j: (0, 0)), pl.BlockSpec((T, bn), lambda j: (0, j))]
        out_spec = pl.BlockSpec((None, M, bn), lambda j: (j // steps, 0, j % steps))
        out_shape = jax.ShapeDtypeStruct((N_CHIPS, M, N // N_CHIPS), BF16)
    else:
        bm = M // (N_CHIPS * steps)
        in_specs = [pl.BlockSpec((bm, T), lambda j: (j, 0)), pl.BlockSpec((T, N), lambda j: (0, 0))]
        out_spec = pl.BlockSpec((None, bm, N), lambda j: (j // steps, j % steps, 0))
        out_shape = jax.ShapeDtypeStruct((N_CHIPS, M // N_CHIPS, N), BF16)

    def body(at_ref, b_ref, o_ref):
        o_ref[...] = jnp.dot(at_ref[...], b_ref[...], preferred_element_type=F32).astype(BF16)

    return pl.pallas_call(
        body, name=name, grid=(N_CHIPS * steps,), in_specs=in_specs, out_specs=out_spec, out_shape=out_shape,
        compiler_params=_params(("arbitrary",)),
    )(at, b)


def _rows_block(rows, cols):
    br = rows
    while br * cols * 4 > 2 ** 20 and br % 16 == 0:
        br //= 2
    return br


def _cast_own_call(name, chip, w, l):
    _, rows, cols = w.shape
    br = _rows_block(rows, cols)
    n_b = rows // br

    def body(chip_ref, w_ref, o_ref):
        o_ref[...] = w_ref[...].astype(BF16)

    return pl.pallas_call(
        body, name=name,
        grid_spec=pltpu.PrefetchScalarGridSpec(
            num_scalar_prefetch=1, grid=(n_b,),
            in_specs=[pl.BlockSpec((None, br, cols), lambda i, chip_ref: (l, i, 0))],
            out_specs=pl.BlockSpec((None, br, cols), lambda i, chip_ref: (chip_ref[0], i, 0))),
        out_shape=jax.ShapeDtypeStruct((N_CHIPS, rows, cols), BF16),
        compiler_params=_params(("arbitrary",)),
    )(chip, w)


def _pair_sum_call(name, core, g, got):
    _, rh, cols = got.shape
    br = _rows_block(rh, cols)
    n_b = rh // br

    def body(core_ref, g_ref, got_ref, o_ref):
        o_ref[...] = (g_ref[...].astype(F32) + got_ref[...].astype(F32)).astype(BF16)

    half = pl.BlockSpec((None, br, cols), lambda q, i, core_ref: (q, i, 0))
    return pl.pallas_call(
        body, name=name,
        grid_spec=pltpu.PrefetchScalarGridSpec(
            num_scalar_prefetch=1, grid=(N_CHIPS, n_b),
            in_specs=[pl.BlockSpec((None, br, cols), lambda q, i, core_ref: (q, core_ref[0] * n_b + i, 0)), half],
            out_specs=half),
        out_shape=jax.ShapeDtypeStruct(got.shape, BF16),
        compiler_params=_params(("arbitrary", "arbitrary")),
    )(core, g, got)


def _chip_sum_call(name, core, parts):
    _, rh, cols = parts.shape
    br = _rows_block(rh, cols)
    n_b = rh // br

    def body(core_ref, *refs):
        o_ref = refs[N_CHIPS]
        total = refs[0][...].astype(F32)
        for q in range(1, N_CHIPS):
            total = total + refs[q][...].astype(F32)
        o_ref[...] = total

    return pl.pallas_call(
        body, name=name,
        grid_spec=pltpu.PrefetchScalarGridSpec(
            num_scalar_prefetch=1, grid=(n_b,),
            in_specs=[pl.BlockSpec((None, br, cols), functools.partial(lambda i, core_ref, q: (q, i, 0), q=q))
                      for q in range(N_CHIPS)],
            out_specs=pl.BlockSpec((br, cols), lambda i, core_ref: (core_ref[0] * n_b + i, 0))),
        out_shape=jax.ShapeDtypeStruct((2 * rh, cols), F32),
        compiler_params=_params(("arbitrary",)),
    )(core, *([parts] * N_CHIPS))


def _adamw_layers_call(name, w, g0, g1, m, v, comm=None):
    _, rows, cols = w.shape
    br = _rows_block(rows, cols)

    def body(w_ref, g0_ref, g1_ref, m_ref, v_ref, g_ref, d_ref, nm_ref, nv_ref):
        gv = jnp.where(pl.program_id(0) == 0, g0_ref[...], g1_ref[...])
        g_ref[...] = gv
        m_new = ADAM_B1 * m_ref[...] + (1.0 - ADAM_B1) * gv
        v_new = ADAM_B2 * v_ref[...] + (1.0 - ADAM_B2) * (gv * gv)
        m_hat = m_new / (1.0 - ADAM_B1 ** ADAM_STEP)
        v_hat = v_new / (1.0 - ADAM_B2 ** ADAM_STEP)
        d_ref[...] = -ADAM_LR * (m_hat / (jnp.sqrt(v_hat) + ADAM_EPS) + ADAM_WD * w_ref[...])
        nm_ref[...] = m_new
        nv_ref[...] = v_new

    both = pl.BlockSpec((None, br, cols), lambda l, i: (l, i, 0))
    one = pl.BlockSpec((br, cols), lambda l, i: (i, 0))
    return _call(body, name, (2, rows // br), in_specs=[both, one, one, both, both], out_specs=[both] * 4,
                 out_shape=[jax.ShapeDtypeStruct(w.shape, F32)] * 4, scratch_shapes=[],
                 sem=("arbitrary", "arbitrary"), operands=(w, g0, g1, m, v), comm=comm)


def _adamw_call(name, w, g, m, v):
    rows, cols = w.shape
    br = _rows_block(rows, cols)

    def body(w_ref, g_ref, m_ref, v_ref, d_ref, nm_ref, nv_ref):
        gv = g_ref[...]
        m_new = ADAM_B1 * m_ref[...] + (1.0 - ADAM_B1) * gv
        v_new = ADAM_B2 * v_ref[...] + (1.0 - ADAM_B2) * (gv * gv)
        m_hat = m_new / (1.0 - ADAM_B1 ** ADAM_STEP)
        v_hat = v_new / (1.0 - ADAM_B2 ** ADAM_STEP)
        d_ref[...] = -ADAM_LR * (m_hat / (jnp.sqrt(v_hat) + ADAM_EPS) + ADAM_WD * w_ref[...])
        nm_ref[...] = m_new
        nv_ref[...] = v_new

    spec = pl.BlockSpec((br, cols), lambda i: (i, 0))
    return pl.pallas_call(
        body, name=name, grid=(rows // br,), in_specs=[spec] * 4, out_specs=[spec] * 3,
        out_shape=[jax.ShapeDtypeStruct((rows, cols), F32)] * 3,
        compiler_params=_params(("arbitrary",)),
    )(w, g, m, v)


def _place():
    x, y, c = lax.axis_index("x"), lax.axis_index("y"), lax.axis_index("c")
    chips = [(1 - x, y), (x, 1 - y), (1 - x, 1 - y)]
    return x, y, c, 2 * x + y, chips


def _half_rows(ref, core):
    rh = ref.shape[-2] // 2
    rows = pl.ds(pl.multiple_of(core * rh, rh), rh)
    return ref.at[rows] if len(ref.shape) == 2 else ref.at[:, rows]


def _gather_comm(bufs, cw_shard=None):
    n = len(bufs)
    with_cw = cw_shard is not None

    def make(c_ins, c_outs, sems):
        ins, outs = c_ins[:n], c_outs[:n]
        send_sems, recv_sems, cw_send, cw_recv, cw_local = sems
        x, y, c, p, chips = _place()
        sibling = (x, y, 1 - c)
        qs = [2 * cx + cy for cx, cy in chips]
        hops = len(chips)

        def remote(k, j, src, dst, to):
            return pltpu.make_async_remote_copy(src_ref=src, dst_ref=dst, send_sem=send_sems.at[k, j],
                                                recv_sem=recv_sems.at[k, j], device_id=to, device_id_type=MESH)

        def cw_copy(j, dst, to):
            return pltpu.make_async_remote_copy(src_ref=c_ins[n], dst_ref=dst, send_sem=cw_send.at[j],
                                                recv_sem=cw_recv.at[j], device_id=to, device_id_type=MESH)

        def over_ici():
            return [remote(k, j, _half_rows(ins[k].at[p], c), _half_rows(outs[k].at[p], c), (*chip, c))
                    for j, chip in enumerate(chips) for k in range(n)]

        def passed_on():
            return [remote(k, hops + j, _half_rows(outs[k].at[qs[j]], c), _half_rows(outs[k].at[qs[j]], c), sibling)
                    for j in range(hops) for k in range(n)]

        def cw_copies():
            return [cw_copy(j, c_outs[n].at[p], (*chip, c)) for j, chip in enumerate(chips)]

        def cw_own():
            return pltpu.make_async_copy(c_ins[n], c_outs[n].at[p], cw_local)

        def start():
            for cp in over_ici():
                cp.start()
            if with_cw:
                cw_own().start()
                for cp in cw_copies():
                    cp.start()

        def finish():
            for j in range(hops):
                for k in range(n):
                    landed = _half_rows(outs[k].at[qs[j]], c)
                    remote(k, j, landed, landed, sibling).wait_recv()
                    remote(k, hops + j, landed, landed, sibling).start()
            for j in range(hops):
                for k in range(n):
                    other = _half_rows(outs[k].at[qs[j]], 1 - c)
                    remote(k, hops + j, other, other, sibling).wait_recv()
            if with_cw:
                for j in range(hops):
                    cw_copy(j, c_outs[n].at[qs[j]], sibling).wait_recv()
                cw_own().wait()
                for cp in cw_copies():
                    cp.wait_send()
            for cp in over_ici() + passed_on():
                cp.wait_send()

        return start, finish

    out_shape = [jax.ShapeDtypeStruct(b.shape, b.dtype) for b in bufs]
    if with_cw:
        out_shape.append(jax.ShapeDtypeStruct((N_CHIPS,) + cw_shard.shape, cw_shard.dtype))
    return dict(operands=list(bufs) + ([cw_shard] if with_cw else []), out_shape=out_shape,
                aliases={k: k for k in range(n)}, make=make,
                scratch=[pltpu.SemaphoreType.DMA((n, 6)), pltpu.SemaphoreType.DMA((n, 6)),
                         pltpu.SemaphoreType.DMA((3,)), pltpu.SemaphoreType.DMA((3,)), pltpu.SemaphoreType.DMA])


def _pair_exchange_call(l, gs):
    n = len(gs)

    def body(*refs):
        g, got = refs[:n], refs[n:2 * n]
        send_sems, recv_sems = refs[2 * n:]
        x, y, c, _, _ = _place()
        sibling = (x, y, 1 - c)
        copies = [pltpu.make_async_remote_copy(src_ref=_half_rows(g[k], 1 - c), dst_ref=got[k],
                                               send_sem=send_sems.at[k], recv_sem=recv_sems.at[k],
                                               device_id=sibling, device_id_type=MESH) for k in range(n)]
        for cp in copies:
            cp.start()
        for cp in copies:
            cp.wait_send()
            cp.wait_recv()

    return pl.pallas_call(
        body, name=f"grad_pair_exchange_l{l}",
        in_specs=[ANY] * n, out_specs=[ANY] * n,
        out_shape=[jax.ShapeDtypeStruct((g.shape[0], g.shape[1] // 2, g.shape[2]), g.dtype) for g in gs],
        scratch_shapes=[pltpu.SemaphoreType.DMA((n,)), pltpu.SemaphoreType.DMA((n,))],
    )(*gs)


def _exchange_comm(sums):
    n = len(sums)

    def make(ins, outs, sems):
        send_sems, recv_sems, local_sems = sems
        x, y, c, p, chips = _place()
        qs = [2 * cx + cy for cx, cy in chips]

        def remote(k, j, src, dst, to):
            return pltpu.make_async_remote_copy(src_ref=src, dst_ref=dst, send_sem=send_sems.at[k, j],
                                                recv_sem=recv_sems.at[k, j], device_id=to, device_id_type=MESH)

        def own():
            return [pltpu.make_async_copy(ins[k].at[p], outs[k].at[p], local_sems.at[k]) for k in range(n)]

        def sent():
            return [remote(k, j, ins[k].at[qs[j]], outs[k].at[p], (*chip, c))
                    for j, chip in enumerate(chips) for k in range(n)]

        def start():
            for cp in own() + sent():
                cp.start()

        def finish():
            for j in range(len(chips)):
                for k in range(n):
                    remote(k, j, ins[k].at[qs[j]], outs[k].at[qs[j]], (x, y, c)).wait_recv()
            for cp in sent():
                cp.wait_send()
            for cp in own():
                cp.wait()

        return start, finish

    return dict(operands=list(sums), out_shape=[jax.ShapeDtypeStruct(s.shape, s.dtype) for s in sums], aliases={},
                make=make, scratch=[pltpu.SemaphoreType.DMA((n, 3)), pltpu.SemaphoreType.DMA((n, 3)),
                                    pltpu.SemaphoreType.DMA((n,))])


def _pair_gather_call(l, halves):
    n = len(halves)

    def body(*refs):
        ins, outs = refs[:n], refs[n:2 * n]
        send_sems, recv_sems = refs[2 * n:]
        x, y, c, _, _ = _place()
        sibling = (x, y, 1 - c)

        def remote(k, src, dst):
            return pltpu.make_async_remote_copy(src_ref=src, dst_ref=dst, send_sem=send_sems.at[k],
                                                recv_sem=recv_sems.at[k], device_id=sibling, device_id_type=MESH)

        sent = [remote(k, _half_rows(ins[k], c), _half_rows(outs[k], c)) for k in range(n)]
        for cp in sent:
            cp.start()
        for k in range(n):
            other = _half_rows(outs[k], 1 - c)
            remote(k, other, other).wait_recv()
        for cp in sent:
            cp.wait_send()

    return pl.pallas_call(
        body, name=f"grad_pair_gather_l{l}",
        in_specs=[ANY] * n, out_specs=[ANY] * n,
        out_shape=[jax.ShapeDtypeStruct(h.shape, h.dtype) for h in halves],
        input_output_aliases={k: k for k in range(n)},
        scratch_shapes=[pltpu.SemaphoreType.DMA((n,)), pltpu.SemaphoreType.DMA((n,))],
    )(*halves)


def _all_sum_small_call(block, comm=None):
    m_per, n = block.shape

    def body(x_ref, sum_ref, all_ref, send_sems, recv_sems, local_sem):
        x, y, c, _, chip_list = _place()
        me, sibling = (x, y, c), (x, y, 1 - c)

        def rows(px, py, pc):
            return all_ref.at[pl.ds((4 * px + 2 * py + pc) * m_per, m_per), :]

        def copy(k, blk, to, src=None):
            return pltpu.make_async_remote_copy(src_ref=rows(*blk) if src is None else src, dst_ref=rows(*blk),
                                                send_sem=send_sems.at[k], recv_sem=recv_sems.at[k],
                                                device_id=to, device_id_type=MESH)

        mine = pltpu.make_async_copy(x_ref, rows(*me), local_sem)
        mine.start()
        first = [copy(0, me, sibling, src=x_ref)]
        first += [copy(1 + j, me, (*chip, c), src=x_ref) for j, chip in enumerate(chip_list)]
        for cp in first:
            cp.start()
        passed = [copy(4 + j, (*chip, c), sibling) for j, chip in enumerate(chip_list)]
        for j, chip in enumerate(chip_list):
            copy(1 + j, (*chip, c), me).wait_recv()
            passed[j].start()
        copy(0, sibling, me).wait_recv()
        for j, chip in enumerate(chip_list):
            copy(4 + j, (*chip, 1 - c), me).wait_recv()
        for cp in first + passed:
            cp.wait_send()
        mine.wait()
        total = all_ref[0:m_per, :]
        for d in range(1, N_DEV):
            total = total + all_ref[d * m_per:(d + 1) * m_per, :]
        sum_ref[...] = total

    vmem = pl.BlockSpec(memory_space=pltpu.VMEM)
    res, c_res = _call(
        body, "small_all_sum", (), in_specs=[vmem], out_specs=[vmem, vmem],
        out_shape=[jax.ShapeDtypeStruct((m_per, n), F32), jax.ShapeDtypeStruct((N_DEV * m_per, n), F32)],
        scratch_shapes=[pltpu.SemaphoreType.DMA((7,)), pltpu.SemaphoreType.DMA((7,)), pltpu.SemaphoreType.DMA],
        sem=None, operands=(block,), comm=comm)
    return res[0], c_res


SMALL_NAMES = ["norm1_g", "sgu_ln_g", "sgu_ln_b", "sgu_w", "sgu_b", "conv_b", "conv_ln_g", "conv_ln_b", "norm2_g",
               "final_g"]
MIX_NAMES = ["w_in", "w_out"]
FFN_NAMES = ["w_ff1", "w_ff2"]
BIG_NAMES = MIX_NAMES + FFN_NAMES


def _rows128(a):
    return a.reshape(-1, LANES)


def kernel(x, norm1_g, w_in, sgu_ln_g, sgu_ln_b, sgu_w, sgu_b, conv_w, conv_b, conv_ln_g, conv_ln_b, w_out, norm2_g, w_ff1, w_ff2, final_g, loss_target, m_norm1_g, m_w_in, m_sgu_ln_g, m_sgu_ln_b, m_sgu_w, m_sgu_b, m_conv_w, m_conv_b, m_conv_ln_g, m_conv_ln_b, m_w_out, m_norm2_g, m_w_ff1, m_w_ff2, m_final_g, v_norm1_g, v_w_in, v_sgu_ln_g, v_sgu_ln_b, v_sgu_w, v_sgu_b, v_conv_w, v_conv_b, v_conv_ln_g, v_conv_ln_b, v_w_out, v_norm2_g, v_w_ff1, v_w_ff2, v_final_g):
    w = dict(norm1_g=norm1_g, w_in=w_in, sgu_ln_g=sgu_ln_g, sgu_ln_b=sgu_ln_b, sgu_w=sgu_w, sgu_b=sgu_b,
             conv_w=conv_w, conv_b=conv_b, conv_ln_g=conv_ln_g, conv_ln_b=conv_ln_b, w_out=w_out, norm2_g=norm2_g,
             w_ff1=w_ff1, w_ff2=w_ff2, final_g=final_g)
    m = dict(norm1_g=m_norm1_g, w_in=m_w_in, sgu_ln_g=m_sgu_ln_g, sgu_ln_b=m_sgu_ln_b, sgu_w=m_sgu_w, sgu_b=m_sgu_b,
             conv_w=m_conv_w, conv_b=m_conv_b, conv_ln_g=m_conv_ln_g, conv_ln_b=m_conv_ln_b, w_out=m_w_out,
             norm2_g=m_norm2_g, w_ff1=m_w_ff1, w_ff2=m_w_ff2, final_g=m_final_g)
    v = dict(norm1_g=v_norm1_g, w_in=v_w_in, sgu_ln_g=v_sgu_ln_g, sgu_ln_b=v_sgu_ln_b, sgu_w=v_sgu_w, sgu_b=v_sgu_b,
             conv_w=v_conv_w, conv_b=v_conv_b, conv_ln_g=v_conv_ln_g, conv_ln_b=v_conv_ln_b, w_out=v_w_out,
             norm2_g=v_norm2_g, w_ff1=v_w_ff1, w_ff2=v_w_ff2, final_g=v_final_g)
    depth = w_in.shape[0]
    assert depth == 2, "core c owns layer c of every gradient"
    T, D = x.shape[1], x.shape[2]
    heads = sgu_w.shape[1]
    da = heads * CHUNK
    core = lax.axis_index("c")
    chip = 2 * lax.axis_index("x") + lax.axis_index("y")

    core_arr = core.reshape(1).astype(jnp.int32)
    chip_arr = chip.reshape(1).astype(jnp.int32)
    cw_pad = jnp.pad(conv_w, ((0, 0), (0, HALO - CONV_TAPS), (0, 0)))
    own = [{name: _cast_own_call(f"cast_{name}_l{l}", chip_arr, w[name], l) for name in BIG_NAMES}
           for l in range(depth)]
    gather_of = lambda l, group: _gather_comm([own[l][name] for name in group])
    win_g, wout_g, cw_g = _alone("gather_mix_l0", _gather_comm([own[0][name] for name in MIX_NAMES], cw_pad))
    cw_full = jnp.transpose(cw_g, (1, 2, 0, 3)).reshape(depth, HALO, da)

    ws_t = jnp.swapaxes(sgu_w, -1, -2)
    b_full = jnp.broadcast_to(sgu_b[..., None], sgu_w.shape)
    row = lambda a, l: a[l:l + 1]

    xs, projs, cpres, x1s, zs = [], [], [], [], []
    h = x.reshape(T, D)
    gathered = []
    w1_g = None
    for l in range(depth):
        xs.append(h)
        (x1, proj, cpre), late = _mix_fwd_call(
            l, h, row(norm1_g, l), win_g, row(sgu_ln_g, l), row(sgu_ln_b, l), sgu_w[l], b_full[l], cw_full[l],
            row(conv_b, l), row(conv_ln_g, l), row(conv_ln_b, l), wout_g,
            comm=gather_of(l, FFN_NAMES if w1_g is None else FFN_NAMES[1:]))
        w1_g, w2_g = late if len(late) == 2 else (w1_g, late[0])
        gathered.append((win_g, wout_g, w1_g, w2_g))
        (h, z), nxt = _ffn_fwd_call(l, x1, row(norm2_g, l), w1_g, w2_g,
                                    comm=gather_of(l + 1, MIX_NAMES + FFN_NAMES[:1]) if l + 1 < depth else None)
        if nxt:
            win_g, wout_g, w1_g = nxt
        projs.append(proj)
        cpres.append(cpre)
        x1s.append(x1)
        zs.append(z)
    dx, dxb, loss, d_final_g = _loss_call(h, final_g.reshape(1, D), loss_target.reshape(T, D))

    big = {name: [None] * depth for name in BIG_NAMES}
    reduced = {name: [None] * depth for name in BIG_NAMES}
    small = {name: [None] * depth for name in SMALL_NAMES[:-1] + ["conv_w"]}

    def pair_sums(l, group):
        got = _pair_exchange_call(f"{group[0]}_l{l}", [big[name][l] for name in group])
        return [_pair_sum_call(f"pair_sum_{name}_l{l}", core_arr, big[name][l], recv) for name, recv in zip(group, got)]

    def finish_reduce(l, group, parts):
        halves = [_chip_sum_call(f"chip_sum_{name}_l{l}", core_arr, part) for name, part in zip(group, parts)]
        for name, full in zip(group, _pair_gather_call(f"{group[0]}_l{l}", halves)):
            reduced[name][l] = full

    pending = None
    for l in reversed(range(depth)):
        win_g, wout_g, w1_g, w2_g = gathered[l]
        (dx1, dx1b, dz, f_t, h2_t, dg2), parts = _ffn_bwd_call(
            l, dx, dxb, x1s[l], zs[l], row(norm2_g, l), w1_g, w2_g,
            comm=_exchange_comm(pending[2]) if pending else None)
        if pending:
            finish_reduce(pending[0], pending[1], parts)
        big["w_ff2"][l] = _wgrad_call(f"wgrad_ff2_l{l}", f_t, dxb, False, 2)
        big["w_ff1"][l] = _wgrad_call(f"wgrad_ff1_l{l}", h2_t, dz, True, 2)
        ffn_sums = pair_sums(l, FFN_NAMES)
        ((dx, dxb, dproj, h1_t, mix_t, dg1, dlng, dlnb, dws, dbs, dcw, dcb, dcg, dcbeta), parts) = _mix_bwd_call(
            l, dx1, dx1b, xs[l], projs[l], cpres[l], row(norm1_g, l), win_g, row(sgu_ln_g, l), row(sgu_ln_b, l), sgu_w[l],
            ws_t[l], b_full[l], cw_full[l], row(conv_b, l), row(conv_ln_g, l), row(conv_ln_b, l), wout_g,
            comm=_exchange_comm(ffn_sums))
        finish_reduce(l, FFN_NAMES, parts)
        big["w_out"][l] = _wgrad_call(f"wgrad_out_l{l}", mix_t, dx1b, False, 1)
        big["w_in"][l] = _wgrad_call(f"wgrad_in_l{l}", h1_t, dproj, True, 1)
        pending = (l, MIX_NAMES, pair_sums(l, MIX_NAMES))
        small["norm1_g"][l] = dg1[0]
        small["sgu_ln_g"][l] = dlng[0]
        small["sgu_ln_b"][l] = dlnb[0]
        small["sgu_w"][l] = dws
        small["sgu_b"][l] = dbs[:, :, 0]
        small["conv_w"][l] = dcw[:CONV_TAPS]
        small["conv_b"][l] = dcb[0]
        small["conv_ln_g"][l] = dcg[0]
        small["conv_ln_b"][l] = dcbeta[0]
        small["norm2_g"][l] = dg2[0]
    grad_x = dx.reshape(x.shape)

    small_local = {name: jnp.stack(small[name]) for name in small}
    small_local["final_g"] = d_final_g[0]
    pieces = [_rows128(small_local[name]) for name in SMALL_NAMES]
    pieces.append(_rows128(small_local["conv_w"]))
    pieces.append(jnp.broadcast_to(loss, (8, LANES)))
    offsets = [0]
    for piece in pieces:
        offsets.append(offsets[-1] + piece.shape[0])
    summed, parts = _all_sum_small_call(jnp.concatenate(pieces, axis=0), comm=_exchange_comm(pending[2]))
    finish_reduce(pending[0], pending[1], parts)
    n_small = offsets[len(SMALL_NAMES)]
    loss_out = summed[offsets[-2], 0]
    small_grads = {name: summed[offsets[k]:offsets[k + 1]].reshape(w[name].shape)
                   for k, name in enumerate(SMALL_NAMES)}
    conv_w_full = summed[offsets[-3]:offsets[-2]].reshape(depth, CONV_TAPS, da)
    conv_w_grad = lax.dynamic_slice_in_dim(conv_w_full, chip * conv_w.shape[-1], conv_w.shape[-1], axis=2)

    grads, delta, new_m, new_v = {}, {}, {}, {}
    for name in BIG_NAMES:
        (grads[name], delta[name], new_m[name], new_v[name]), _ = _adamw_layers_call(
            f"adamw_{name}", w[name], reduced[name][0], reduced[name][1], m[name], v[name])
    pack = lambda src: jnp.concatenate([_rows128(src[name]) for name in SMALL_NAMES], axis=0)
    d_, m_, v_ = _adamw_call("adamw_small", pack(w), summed[:n_small], pack(m), pack(v))
    for k, name in enumerate(SMALL_NAMES):
        sl = slice(offsets[k], offsets[k + 1])
        grads[name] = small_grads[name]
        delta[name] = d_[sl].reshape(w[name].shape)
        new_m[name] = m_[sl].reshape(w[name].shape)
        new_v[name] = v_[sl].reshape(w[name].shape)
    cshape = conv_w.shape
    flat = lambda a: a.reshape(-1, cshape[-1])
    d_, m_, v_ = _adamw_call("adamw_conv_w", flat(conv_w), flat(conv_w_grad), flat(m["conv_w"]), flat(v["conv_w"]))
    grads["conv_w"] = conv_w_grad
    delta["conv_w"], new_m["conv_w"], new_v["conv_w"] = d_.reshape(cshape), m_.reshape(cshape), v_.reshape(cshape)

    order = ["norm1_g", "w_in", "sgu_ln_g", "sgu_ln_b", "sgu_w", "sgu_b", "conv_w", "conv_b", "conv_ln_g",
             "conv_ln_b", "w_out", "norm2_g", "w_ff1", "w_ff2", "final_g"]
    return (loss_out, grad_x, *[grads[n] for n in order], *[delta[n] for n in order],
            *[new_m[n] for n in order], *[new_v[n] for n in order])
```

```python
import functools

import jax
import jax.numpy as jnp
from jax import lax
from jax.experimental import pallas as pl
from jax.experimental.pallas import tpu as pltpu

F32 = jnp.float32
BF16 = jnp.bfloat16
MESH = pl.DeviceIdType.MESH

EPS = 1e-6
CHUNK = 128
CONV_TAPS = 31
HALO = 32
N_CHIPS = 4
N_DEV = 8
LANES = 128
SUBLANES = 8

ADAM_LR = 0.001
ADAM_B1 = 0.9
ADAM_B2 = 0.999
ADAM_EPS = 1e-08
ADAM_WD = 0.01
ADAM_STEP = 10

TM_MIX = 256
TM_FFN = 512
TM_FFN_FWD = 1024
V7X_VMEM_BYTES = 64 * 2 ** 20
VMEM_LIMIT = V7X_VMEM_BYTES - 8 * 2 ** 20


def _params(sem=None):
    return pltpu.CompilerParams(dimension_semantics=sem, vmem_limit_bytes=VMEM_LIMIT)


def _gelu_and_grad(x):
    gauss = jnp.exp(-0.5 * x * x)
    t = 1.0 / (1.0 + (0.3275911 * 0.7071067811865476) * jnp.abs(x))
    poly = t * (0.254829592 + t * (-0.284496736 + t * (1.421413741 + t * (-1.453152027 + t * 1.061405429))))
    erf_abs = 1.0 - poly * gauss
    cdf = 0.5 * (1.0 + jnp.where(x < 0, -erf_abs, erf_abs))
    return x * cdf, cdf + x * (gauss * 0.3989422804014327)


def _sigmoid(x):
    return 1.0 / (1.0 + jnp.exp(-x))


def _dot_nt(a, b):
    return lax.dot_general(a.astype(BF16), b.astype(BF16), (((1,), (1,)), ((), ())), preferred_element_type=F32)


def _rms_fwd(x, g):
    r = lax.rsqrt(jnp.mean(x * x, axis=-1, keepdims=True) + EPS)
    xh = x * r
    return r, xh, xh * g


def _rms_bwd(dh, xh, r, g):
    dxh = dh * g
    return r * (dxh - xh * jnp.mean(dxh * xh, axis=-1, keepdims=True))


def _ln_stats(x):
    mu = jnp.mean(x, axis=-1, keepdims=True)
    xc = x - mu
    rs = lax.rsqrt(jnp.mean(xc * xc, axis=-1, keepdims=True) + EPS)
    return xc * rs, rs


def _ln_bwd(dxh, xh, rs):
    return rs * (dxh - jnp.mean(dxh, axis=-1, keepdims=True) - xh * jnp.mean(dxh * xh, axis=-1, keepdims=True))


def _group_ln_fwd(c):
    parts, rss = [], []
    for j in range(c.shape[1] // CHUNK):
        xh, rs = _ln_stats(c[:, j * CHUNK:(j + 1) * CHUNK])
        parts.append(xh)
        rss.append(rs)
    return jnp.concatenate(parts, axis=1), rss


def _group_ln_bwd(dxh, xh, rss):
    parts = []
    for j, rs in enumerate(rss):
        cols = slice(j * CHUNK, (j + 1) * CHUNK)
        parts.append(_ln_bwd(dxh[:, cols], xh[:, cols], rs))
    return jnp.concatenate(parts, axis=1)


def _sublane_shifts(ext):
    n = ext.shape[0]
    return [ext if b == 0 else pltpu.roll(ext, n - b, 0) for b in range(SUBLANES)]


def _rows_from(shifts, off, tm):
    a, b = divmod(off, SUBLANES)
    return shifts[b][a * SUBLANES:a * SUBLANES + tm]


def _tril_mask():
    t = lax.broadcasted_iota(jnp.int32, (CHUNK, CHUNK), 0)
    s = lax.broadcasted_iota(jnp.int32, (CHUNK, CHUNK), 1)
    return t >= s


def _mix_forward(ua, va, vb, gb, g_halo, ws_ref, bfull_ref, lng, lnb, cw_ref, cb, cg, cbeta, mixed_scr, cpre=None,
                 kept=None):
    tm, da = vb.shape
    heads = da // CHUNK
    if kept is None:
        u, du_fac = _gelu_and_grad(ua)
        vg, dvg_fac = _gelu_and_grad(va)
        vh, v_rs = _ln_stats(vg)
    else:
        u, du_fac, dvg_fac, vh, v_rs = kept
    v = vh * lng + lnb
    mask = _tril_mask()
    wm = [jnp.where(mask, ws_ref[h], 0.0).astype(BF16) for h in range(heads)]
    vb16 = v.astype(BF16)
    for ci in range(tm // CHUNK):
        rows = slice(ci * CHUNK, (ci + 1) * CHUNK)
        for h in range(heads):
            cols = slice(h * CHUNK, (h + 1) * CHUNK)
            mixed_scr[rows, cols] = jnp.dot(wm[h], vb16[rows, cols], preferred_element_type=F32) + bfull_ref[h]
    mixed = mixed_scr[...]
    a = u * mixed

    sg = _sigmoid(gb)
    g = vb * sg
    g_shifts = _sublane_shifts(jnp.concatenate([g_halo, g], axis=0))
    if cpre is None:
        cpre = jnp.zeros_like(g) + cb
        for k in range(CONV_TAPS):
            cpre = cpre + _rows_from(g_shifts, HALO - (CONV_TAPS - 1) + k, tm) * cw_ref[k:k + 1, :]
    chh, c_rss = _group_ln_fwd(cpre)
    cn = chh * cg + cbeta
    sc = _sigmoid(cn)
    cout = cn * sc
    return dict(u=u, du_fac=du_fac, dvg_fac=dvg_fac, vh=vh, v_rs=v_rs, v16=vb16, mixed=mixed, a=a, sg=sg, g=g,
                g_shifts=g_shifts, cpre=cpre, chh=chh, c_rss=c_rss, cn=cn, sc=sc, cout=cout)


ANY = pl.BlockSpec(memory_space=pl.ANY)


def _call(body, name, grid, in_specs, out_specs, out_shape, scratch_shapes, sem, operands, comm=None):
    if comm is None:
        res = pl.pallas_call(body, name=name, grid=grid, in_specs=in_specs, out_specs=out_specs, out_shape=out_shape,
                             scratch_shapes=scratch_shapes, compiler_params=_params(sem))(*operands)
        return res, []
    n_in, n_out, n_scr = len(in_specs), len(out_specs), len(scratch_shapes)
    n_cin, n_cout = len(comm["operands"]), len(comm["out_shape"])

    def fused(*refs):
        ins, refs = refs[:n_in], refs[n_in:]
        c_ins, refs = refs[:n_cin], refs[n_cin:]
        outs, refs = refs[:n_out], refs[n_out:]
        c_outs, refs = refs[:n_cout], refs[n_cout:]
        scr, c_sems = refs[:n_scr], refs[n_scr:]
        start, finish = comm["make"](c_ins, c_outs, c_sems)
        if grid:
            first = functools.reduce(jnp.logical_and, [pl.program_id(a) == 0 for a in range(len(grid))])
            last = functools.reduce(jnp.logical_and, [pl.program_id(a) == grid[a] - 1 for a in range(len(grid))])
            pl.when(first)(start)
            body(*ins, *outs, *scr)
            pl.when(last)(finish)
        else:
            start()
            body(*ins, *outs, *scr)
            finish()

    res = pl.pallas_call(
        fused, name=name, grid=grid,
        in_specs=list(in_specs) + [ANY] * n_cin, out_specs=list(out_specs) + [ANY] * n_cout,
        out_shape=list(out_shape) + list(comm["out_shape"]),
        input_output_aliases={n_in + a: n_out + b for a, b in comm["aliases"].items()},
        scratch_shapes=list(scratch_shapes) + list(comm["scratch"]),
        compiler_params=_params(sem),
    )(*operands, *comm["operands"])
    return res[:n_out], res[n_out:]


def _alone(name, comm):
    n_cin, n_cout = len(comm["operands"]), len(comm["out_shape"])

    def body(*refs):
        start, finish = comm["make"](refs[:n_cin], refs[n_cin:n_cin + n_cout], refs[n_cin + n_cout:])
        start()
        finish()

    return pl.pallas_call(
        body, name=name, in_specs=[ANY] * n_cin, out_specs=[ANY] * n_cout, out_shape=list(comm["out_shape"]),
        input_output_aliases=dict(comm["aliases"]), scratch_shapes=list(comm["scratch"]),
    )(*comm["operands"])


def _mix_fwd_call(l, x, g1, win_g, lng, lnb, ws, bfull, cw, cb, cg, cbeta, wout_g, comm=None):
    T, D = x.shape
    tm = min(TM_MIX, T)
    n_t = T // tm
    da = win_g.shape[-1]
    wo_rows = wout_g.shape[1]

    def body(x_ref, g1_ref, win_ref, lng_ref, lnb_ref, ws_ref, bf_ref, cw_ref, cb_ref, cg_ref, cbeta_ref, wout_ref,
             x1_ref, proj_ref, cpre_ref, kept_ref, rs_ref, halo_scr, mixed_scr):
        i = pl.program_id(0)

        @pl.when(i == 0)
        def _():
            halo_scr[...] = jnp.zeros_like(halo_scr)

        x_t = x_ref[...]
        _, _, h1 = _rms_fwd(x_t, g1_ref[...])
        h1b = h1.astype(BF16)
        ps = []
        for p in range(N_CHIPS):
            pp = jnp.dot(h1b, win_ref[p], preferred_element_type=F32)
            proj_ref[:, p * da:(p + 1) * da] = pp
            ps.append(pp)
        f = _mix_forward(ps[0], ps[1], ps[2], ps[3], halo_scr[...], ws_ref, bf_ref, lng_ref[...], lnb_ref[...],
                         cw_ref, cb_ref[...], cg_ref[...], cbeta_ref[...], mixed_scr)
        halo_scr[...] = f["g"][tm - HALO:, :]
        cpre_ref[...] = f["cpre"]
        for n, key in enumerate(("u", "du_fac", "dvg_fac", "vh")):
            kept_ref[:, n * da:(n + 1) * da] = f[key]
        rs_ref[...] = jnp.broadcast_to(f["v_rs"], rs_ref.shape)
        mix =jnp.concatenate([f["a"], f["cout"]], axis=1).astype(BF16)
        acc = x_t
        for q in range(N_CHIPS):
            acc = acc + jnp.dot(mix[:, q * wo_rows:(q + 1) * wo_rows], wout_ref[q], preferred_element_type=F32)
        x1_ref[...] = acc

    full = lambda a: pl.BlockSpec(a.shape, lambda i: (0,) * a.ndim)
    return _call(
        body, f"mix_fwd_l{l}", (n_t,),
        in_specs=[
            pl.BlockSpec((tm, D), lambda i: (i, 0)),
            full(g1),
            pl.BlockSpec((N_CHIPS, D, da), lambda i: (0, 0, 0)),
            full(lng), full(lnb), full(ws), full(bfull), full(cw), full(cb), full(cg), full(cbeta),
            pl.BlockSpec((N_CHIPS, wo_rows, D), lambda i: (0, 0, 0)),
        ],
        out_specs=[
            pl.BlockSpec((tm, D), lambda i: (i, 0)),
            pl.BlockSpec((tm, N_CHIPS * da), lambda i: (i, 0)),
            pl.BlockSpec((tm, da), lambda i: (i, 0)),
            pl.BlockSpec((tm, 4 * da), lambda i: (i, 0)),
            pl.BlockSpec((tm, LANES), lambda i: (i, 0)),
        ],
        out_shape=[jax.ShapeDtypeStruct((T, D), F32), jax.ShapeDtypeStruct((T, N_CHIPS * da), F32),
                   jax.ShapeDtypeStruct((T, da), F32), jax.ShapeDtypeStruct((T, 4 * da), F32),
                   jax.ShapeDtypeStruct((T, LANES), F32)],
        scratch_shapes=[pltpu.VMEM((HALO, da), F32), pltpu.VMEM((tm, da), F32)],
        sem=("arbitrary",),
        operands=(x, g1, win_g, lng, lnb, ws, bfull, cw, cb, cg, cbeta, wout_g), comm=comm)


def _ffn_tile(T, tile=TM_FFN):
    return min(tile, max(T // 2, CHUNK))


def _ffn_fwd_call(l, x1, g2, w1_g, w2_g, comm=None):
    T, D = x1.shape
    tm = _ffn_tile(T, TM_FFN_FWD)
    n_t = T // tm
    ffb = w1_g.shape[-1]

    def body(x1_ref, g2_ref, w1_ref, w2_ref, x2_ref, z_ref, h2_scr, acc_scr):
        p = pl.program_id(1)

        @pl.when(p == 0)
        def _():
            _, _, h2 = _rms_fwd(x1_ref[...], g2_ref[...])
            h2_scr[...] = h2.astype(BF16)
            acc_scr[...] = jnp.zeros_like(acc_scr)

        z = jnp.dot(h2_scr[...], w1_ref[...], preferred_element_type=F32)
        z_ref[...] = z.astype(BF16)
        rz = jnp.maximum(z, 0.0)
        acc_scr[...] += jnp.dot((rz * rz).astype(BF16), w2_ref[...], preferred_element_type=F32)

        @pl.when(p == N_CHIPS - 1)
        def _():
            x2_ref[...] = x1_ref[...] + acc_scr[...]

    return _call(
        body, f"ffn_fwd_l{l}", (n_t, N_CHIPS),
        in_specs=[
            pl.BlockSpec((tm, D), lambda i, p: (i, 0)),
            pl.BlockSpec(g2.shape, lambda i, p: (0, 0)),
            pl.BlockSpec((None, D, ffb), lambda i, p: (p, 0, 0)),
            pl.BlockSpec((None, ffb, D), lambda i, p: (p, 0, 0)),
        ],
        out_specs=[
            pl.BlockSpec((tm, D), lambda i, p: (i, 0)),
            pl.BlockSpec((tm, ffb), lambda i, p: (i, p)),
        ],
        out_shape=[jax.ShapeDtypeStruct((T, D), F32), jax.ShapeDtypeStruct((T, N_CHIPS * ffb), BF16)],
        scratch_shapes=[pltpu.VMEM((tm, D), BF16), pltpu.VMEM((tm, D), F32)],
        sem=("arbitrary", "arbitrary"), operands=(x1, g2, w1_g, w2_g), comm=comm)


def _loss_call(x, gf, target):
    T, D = x.shape
    tm = min(TM_FFN, T)
    n_t = T // tm

    def body(x_ref, gf_ref, t_ref, dx_ref, dxb_ref, loss_ref, dgf_ref):
        i = pl.program_id(0)

        @pl.when(i == 0)
        def _():
            loss_ref[...] = jnp.zeros_like(loss_ref)
            dgf_ref[...] = jnp.zeros_like(dgf_ref)

        g = gf_ref[...]
        r, xh, y = _rms_fwd(x_ref[...], g)
        e = y - t_ref[...]
        per_tok = jnp.sum(e * e, axis=-1, keepdims=True) * (1.0 / D)
        loss_ref[...] += 0.5 * jnp.sum(per_tok, axis=0, keepdims=True)
        dy = e * (1.0 / D)
        dgf_ref[...] += jnp.sum(dy * xh, axis=0, keepdims=True)
        dx = _rms_bwd(dy, xh, r, g)
        dx_ref[...] = dx
        dxb_ref[...] = dx.astype(BF16)

    return pl.pallas_call(
        body, name="loss_head",
        grid=(n_t,),
        in_specs=[
            pl.BlockSpec((tm, D), lambda i: (i, 0)),
            pl.BlockSpec(gf.shape, lambda i: (0, 0)),
            pl.BlockSpec((tm, D), lambda i: (i, 0)),
        ],
        out_specs=[
            pl.BlockSpec((tm, D), lambda i: (i, 0)),
            pl.BlockSpec((tm, D), lambda i: (i, 0)),
            pl.BlockSpec((1, 1), lambda i: (0, 0)),
            pl.BlockSpec((1, D), lambda i: (0, 0)),
        ],
        out_shape=[jax.ShapeDtypeStruct((T, D), F32), jax.ShapeDtypeStruct((T, D), BF16),
                   jax.ShapeDtypeStruct((1, 1), F32), jax.ShapeDtypeStruct((1, D), F32)],
        compiler_params=_params(("arbitrary",)),
    )(x, gf, target)


def _ffn_bwd_call(l, dx2, dx2b, x1, z, g2, w1_g, w2_g, comm=None):
    T, D = x1.shape
    tm = _ffn_tile(T)
    n_t = T // tm
    ffb = w1_g.shape[-1]

    def body(dx2_ref, dx2b_ref, x1_ref, z_ref, g2_ref, w1_ref, w2_ref,
             dx1_ref, dx1b_ref, dz_ref, ft_ref, h2t_ref, dg2_ref, acc_scr):
        i = pl.program_id(0)
        p = pl.program_id(1)

        @pl.when(jnp.logical_and(i == 0, p == 0))
        def _():
            dg2_ref[...] = jnp.zeros_like(dg2_ref)

        @pl.when(p == 0)
        def _():
            _, _, h2 = _rms_fwd(x1_ref[...], g2_ref[...])
            h2t_ref[...] = h2.T.astype(BF16)
            acc_scr[...] = jnp.zeros_like(acc_scr)

        rz = jnp.maximum(z_ref[...].astype(F32), 0.0)
        ft_ref[...] = (rz * rz).T.astype(BF16)
        df = _dot_nt(dx2b_ref[...], w2_ref[...])
        dz = (df * (2.0 * rz)).astype(BF16)
        dz_ref[...] = dz
        acc_scr[...] += _dot_nt(dz, w1_ref[...])

        @pl.when(p == N_CHIPS - 1)
        def _():
            g = g2_ref[...]
            r, xh, _ = _rms_fwd(x1_ref[...], g)
            dh2 = acc_scr[...]
            dg2_ref[...] += jnp.sum(dh2 * xh, axis=0, keepdims=True)
            dx1 = dx2_ref[...] + _rms_bwd(dh2, xh, r, g)
            dx1_ref[...] = dx1
            dx1b_ref[...] = dx1.astype(BF16)

    return _call(
        body, f"ffn_bwd_l{l}", (n_t, N_CHIPS),
        in_specs=[
            pl.BlockSpec((tm, D), lambda i, p: (i, 0)),
            pl.BlockSpec((tm, D), lambda i, p: (i, 0)),
            pl.BlockSpec((tm, D), lambda i, p: (i, 0)),
            pl.BlockSpec((tm, ffb), lambda i, p: (i, p)),
            pl.BlockSpec(g2.shape, lambda i, p: (0, 0)),
            pl.BlockSpec((None, D, ffb), lambda i, p: (p, 0, 0)),
            pl.BlockSpec((None, ffb, D), lambda i, p: (p, 0, 0)),
        ],
        out_specs=[
            pl.BlockSpec((tm, D), lambda i, p: (i, 0)),
            pl.BlockSpec((tm, D), lambda i, p: (i, 0)),
            pl.BlockSpec((tm, ffb), lambda i, p: (i, p)),
            pl.BlockSpec((ffb, tm), lambda i, p: (p, i)),
            pl.BlockSpec((D, tm), lambda i, p: (0, i)),
            pl.BlockSpec((1, D), lambda i, p: (0, 0)),
        ],
        out_shape=[jax.ShapeDtypeStruct((T, D), F32), jax.ShapeDtypeStruct((T, D), BF16),
                   jax.ShapeDtypeStruct((T, N_CHIPS * ffb), BF16), jax.ShapeDtypeStruct((N_CHIPS * ffb, T), BF16),
                   jax.ShapeDtypeStruct((D, T), BF16), jax.ShapeDtypeStruct((1, D), F32)],
        scratch_shapes=[pltpu.VMEM((tm, D), F32)],
        sem=("arbitrary", "arbitrary"), operands=(dx2, dx2b, x1, z, g2, w1_g, w2_g), comm=comm)


def _mix_bwd_call(l, dx1, dx1b, x, proj, cpre, kept, v_rs, g1, win_g, lng, lnb, ws, wst, bfull, cw, cb, cg, cbeta, wout_g,
                  comm=None):
    T, D = x.shape
    tm = min(TM_MIX, T)
    n_t = T // tm
    da = win_g.shape[-1]
    heads = da // CHUNK
    wo_rows = wout_g.shape[1]
    halo_blocks = tm // HALO

    def body(dx1_ref, dx1b_ref, x_ref, proj_ref, vbh_ref, gbh_ref, cpre_ref, kept_ref, rs_ref, g1_ref, win_ref, lng_ref, lnb_ref, ws_ref, wst_ref,
             bf_ref, cw_ref, cb_ref, cg_ref, cbeta_ref, wout_ref,
             dx_ref, dxb_ref, dproj_ref, h1t_ref, mixt_ref,
             dg1_ref, dlng_ref, dlnb_ref, dws_ref, dbs_ref, dcw_ref, dcb_ref, dcg_ref, dcbeta_ref,
             carry_scr, mixed_scr, dv_scr):
        i = pl.program_id(0)
        tile = n_t - 1 - i

        @pl.when(i == 0)
        def _():
            carry_scr[...] = jnp.zeros_like(carry_scr)
            for ref in (dg1_ref, dlng_ref, dlnb_ref, dws_ref, dbs_ref, dcw_ref, dcb_ref, dcg_ref, dcbeta_ref):
                ref[...] = jnp.zeros_like(ref)

        g1v = g1_ref[...]
        r, xh, h1 = _rms_fwd(x_ref[...], g1v)
        h1t_ref[...] = h1.T.astype(BF16)

        vb = proj_ref[:, 0:da]
        gb = proj_ref[:, da:2 * da]
        g_halo = jnp.where(tile > 0, vbh_ref[...] * _sigmoid(gbh_ref[...]), 0.0)
        lng_v, cg_v = lng_ref[...], cg_ref[...]
        kept = tuple(kept_ref[:, n * da:(n + 1) * da] for n in range(4)) + (rs_ref[:, 0:1],)
        f = _mix_forward(None, None, vb, gb, g_halo, ws_ref, bf_ref, lng_v, lnb_ref[...], cw_ref, cb_ref[...], cg_v,
                         cbeta_ref[...], mixed_scr, cpre=cpre_ref[...], kept=kept)
        mixt_ref[0:da, :] = f["a"].T.astype(BF16)
        mixt_ref[da:2 * da, :] = f["cout"].T.astype(BF16)

        dxo = dx1b_ref[...]
        dmix = jnp.concatenate([_dot_nt(dxo, wout_ref[q]) for q in range(N_CHIPS)], axis=1)
        da_ = dmix[:, :da]
        dc_ = dmix[:, da:]

        dua = da_ * f["mixed"] * f["du_fac"]
        dmixed = (da_ * f["u"]).astype(BF16)
        mask_t = (lax.broadcasted_iota(jnp.int32, (CHUNK, CHUNK), 1)
                  >= lax.broadcasted_iota(jnp.int32, (CHUNK, CHUNK), 0))
        wmt =[jnp.where(mask_t, wst_ref[h], 0.0).astype(BF16) for h in range(heads)]
        mask = _tril_mask()
        v16 = f["v16"]
        for h in range(heads):
            cols = slice(h * CHUNK, (h + 1) * CHUNK)
            dws_h = jnp.zeros((CHUNK, CHUNK), F32)
            dbs_h = jnp.zeros((CHUNK, CHUNK), F32)
            for ci in range(tm // CHUNK):
                rows = slice(ci * CHUNK, (ci + 1) * CHUNK)
                dm = dmixed[rows, cols]
                dv_scr[rows, cols] = jnp.dot(wmt[h], dm, preferred_element_type=F32)
                dws_h = dws_h + _dot_nt(dm, v16[rows, cols])
                dbs_h = dbs_h + dm.astype(F32)
            dws_ref[h] += jnp.where(mask, dws_h, 0.0)
            dbs_ref[h] += jnp.broadcast_to(jnp.sum(dbs_h, axis=1, keepdims=True), (CHUNK, CHUNK))
        dv = dv_scr[...]
        dlng_ref[...] += jnp.sum(dv * f["vh"], axis=0, keepdims=True)
        dlnb_ref[...] += jnp.sum(dv, axis=0, keepdims=True)
        dva = _ln_bwd(dv * lng_v, f["vh"], f["v_rs"]) * f["dvg_fac"]

        cn, sc = f["cn"], f["sc"]
        dcn = dc_ * (sc * (1.0 + cn * (1.0 - sc)))
        dcg_ref[...] += jnp.sum(dcn * f["chh"], axis=0, keepdims=True)
        dcbeta_ref[...] += jnp.sum(dcn, axis=0, keepdims=True)
        dcpre = _group_ln_bwd(dcn * cg_v, f["chh"], f["c_rss"])
        dcb_ref[...] += jnp.sum(dcpre, axis=0, keepdims=True)
        d_shifts = _sublane_shifts(jnp.concatenate([dcpre, carry_scr[...]], axis=0))
        dg = jnp.zeros_like(dcpre)
        for k in range(CONV_TAPS):
            g_k = _rows_from(f["g_shifts"], HALO - (CONV_TAPS - 1) + k, tm)
            dcw_ref[k:k + 1, :] += jnp.sum(dcpre * g_k, axis=0, keepdims=True)
            dg = dg + _rows_from(d_shifts, CONV_TAPS - 1 - k, tm) * cw_ref[k:k + 1, :]
        carry_scr[...] = dcpre[:HALO, :]
        sg = f["sg"]
        dvb = dg * sg
        dgb = dg * vb * sg * (1.0 - sg)

        dps = [dua.astype(BF16), dva.astype(BF16), dvb.astype(BF16), dgb.astype(BF16)]
        dh1 = jnp.zeros((tm, D), F32)
        for p in range(N_CHIPS):
            dproj_ref[:, p * da:(p + 1) * da] = dps[p]
            dh1 = dh1 + _dot_nt(dps[p], win_ref[p])
        dg1_ref[...] += jnp.sum(dh1 * xh, axis=0, keepdims=True)
        dx = dx1_ref[...] + _rms_bwd(dh1, xh, r, g1v)
        dx_ref[...] = dx
        dxb_ref[...] = dx.astype(BF16)

    rev = lambda i: (n_t - 1 - i, 0)
    full = lambda a: pl.BlockSpec(a.shape, lambda i: (0,) * a.ndim)
    acc = lambda shape: pl.BlockSpec(shape, lambda i: (0,) * len(shape))
    halo_idx = lambda col: (lambda i: (jnp.maximum((n_t - 1 - i) * halo_blocks - 1, 0), col))
    small_shapes = [(1, D), (1, da), (1, da), (heads, CHUNK, CHUNK), (heads, CHUNK, CHUNK), (HALO, da),
                    (1, da), (1, da), (1, da)]
    return _call(
        body, f"mix_bwd_l{l}", (n_t,),
        in_specs=[
            pl.BlockSpec((tm, D), rev),
            pl.BlockSpec((tm, D), rev),
            pl.BlockSpec((tm, D), rev),
            pl.BlockSpec((tm, 2 * da), lambda i: (n_t - 1 - i, 1)),
            pl.BlockSpec((HALO, da), halo_idx(2)),
            pl.BlockSpec((HALO, da), halo_idx(3)),
            pl.BlockSpec((tm, da), rev),
            pl.BlockSpec((tm, 4 * da), rev),
            pl.BlockSpec((tm, LANES), rev),
            full(g1),
            pl.BlockSpec((N_CHIPS, D, da), lambda i: (0, 0, 0)),
            full(lng), full(lnb), full(ws), full(wst), full(bfull), full(cw), full(cb), full(cg), full(cbeta),
            pl.BlockSpec((N_CHIPS, wo_rows, D), lambda i: (0, 0, 0)),
        ],
        out_specs=[
            pl.BlockSpec((tm, D), rev),
            pl.BlockSpec((tm, D), rev),
            pl.BlockSpec((tm, N_CHIPS * da), rev),
            pl.BlockSpec((D, tm), lambda i: (0, n_t - 1 - i)),
            pl.BlockSpec((2 * da, tm), lambda i: (0, n_t - 1 - i)),
        ] + [acc(s) for s in small_shapes],
        out_shape=[jax.ShapeDtypeStruct((T, D), F32), jax.ShapeDtypeStruct((T, D), BF16),
                   jax.ShapeDtypeStruct((T, N_CHIPS * da), BF16), jax.ShapeDtypeStruct((D, T), BF16),
                   jax.ShapeDtypeStruct((2 * da, T), BF16)] + [jax.ShapeDtypeStruct(s, F32) for s in small_shapes],
        scratch_shapes=[pltpu.VMEM((HALO, da), F32), pltpu.VMEM((tm, da), F32), pltpu.VMEM((tm, da), F32)],
        sem=("arbitrary",),
        operands=(dx1, dx1b, x, proj, proj, proj, cpre, kept, v_rs, g1, win_g, lng, lnb, ws, wst, bfull, cw, cb, cg, cbeta, wout_g),
        comm=comm)


def _wgrad_call(name, at, b, split_cols, steps):
    M, T = at.shape
    N = b.shape[1]
    if split_cols:
        bn = N // (N_CHIPS * steps)
        in_specs = [pl.BlockSpec((M, T), lambda j: (0, 0)), pl.BlockSpec((T, bn), lambda j: (0, j))]
        out_spec = pl.BlockSpec((None, M, bn), lambda j: (j // steps, 0, j % steps))
        out_shape = jax.ShapeDtypeStruct((N_CHIPS, M, N // N_CHIPS), BF16)
    else:
        bm = M // (N_CHIPS * steps)
        in_specs = [pl.BlockSpec((bm, T), lambda j: (j, 0)), pl.BlockSpec((T, N), lambda j: (0, 0))]
        out_spec = pl.BlockSpec((None, bm, N), lambda j: (j // steps, j % steps, 0))
        out_shape = jax.ShapeDtypeStruct((N_CHIPS, M // N_CHIPS, N), BF16)

    def body(at_ref, b_ref, o_ref):
        o_ref[...] = jnp.dot(at_ref[...], b_ref[...], preferred_element_type=F32).astype(BF16)

    return pl.pallas_call(
        body, name=name, grid=(N_CHIPS * steps,), in_specs=in_specs, out_specs=out_spec, out_shape=out_shape,
        compiler_params=_params(("arbitrary",)),
    )(at, b)


def _rows_block(rows, cols):
    br = rows
    while br * cols * 4 > 2 ** 20 and br % 16 == 0:
        br //= 2
    return br


def _cast_own_call(name, chip, w, l):
    _, rows, cols = w.shape
    br = _rows_block(rows, cols)
    n_b = rows // br

    def body(chip_ref, w_ref, o_ref):
        o_ref[...] = w_ref[...].astype(BF16)

    return pl.pallas_call(
        body, name=name,
        grid_spec=pltpu.PrefetchScalarGridSpec(
            num_scalar_prefetch=1, grid=(n_b,),
            in_specs=[pl.BlockSpec((None, br, cols), lambda i, chip_ref: (l, i, 0))],
            out_specs=pl.BlockSpec((None, br, cols), lambda i, chip_ref: (chip_ref[0], i, 0))),
        out_shape=jax.ShapeDtypeStruct((N_CHIPS, rows, cols), BF16),
        compiler_params=_params(("arbitrary",)),
    )(chip, w)


def _pair_sum_call(name, core, g, got):
    _, rh, cols = got.shape
    br = _rows_block(rh, cols)
    n_b = rh // br

    def body(core_ref, g_ref, got_ref, o_ref):
        o_ref[...] = (g_ref[...].astype(F32) + got_ref[...].astype(F32)).astype(BF16)

    half = pl.BlockSpec((None, br, cols), lambda q, i, core_ref: (q, i, 0))
    return pl.pallas_call(
        body, name=name,
        grid_spec=pltpu.PrefetchScalarGridSpec(
            num_scalar_prefetch=1, grid=(N_CHIPS, n_b),
            in_specs=[pl.BlockSpec((None, br, cols), lambda q, i, core_ref: (q, core_ref[0] * n_b + i, 0)), half],
            out_specs=half),
        out_shape=jax.ShapeDtypeStruct(got.shape, BF16),
        compiler_params=_params(("arbitrary", "arbitrary")),
    )(core, g, got)


def _chip_sum_call(name, core, parts):
    _, rh, cols = parts.shape
    br = _rows_block(rh, cols)
    n_b = rh // br

    def body(core_ref, *refs):
        o_ref = refs[N_CHIPS]
        total = refs[0][...].astype(F32)
        for q in range(1, N_CHIPS):
            total = total + refs[q][...].astype(F32)
        o_ref[...] = total

    return pl.pallas_call(
        body, name=name,
        grid_spec=pltpu.PrefetchScalarGridSpec(
            num_scalar_prefetch=1, grid=(n_b,),
            in_specs=[pl.BlockSpec((None, br, cols), functools.partial(lambda i, core_ref, q: (q, i, 0), q=q))
                      for q in range(N_CHIPS)],
            out_specs=pl.BlockSpec((br, cols), lambda i, core_ref: (core_ref[0] * n_b + i, 0))),
        out_shape=jax.ShapeDtypeStruct((2 * rh, cols), F32),
        compiler_params=_params(("arbitrary",)),
    )(core, *([parts] * N_CHIPS))


def _adamw_layers_call(name, w, g0, g1, m, v, comm=None):
    _, rows, cols = w.shape
    br = _rows_block(rows, cols)

    def body(w_ref, g0_ref, g1_ref, m_ref, v_ref, g_ref, d_ref, nm_ref, nv_ref):
        gv = jnp.where(pl.program_id(0) == 0, g0_ref[...], g1_ref[...])
        g_ref[...] = gv
        m_new = ADAM_B1 * m_ref[...] + (1.0 - ADAM_B1) * gv
        v_new = ADAM_B2 * v_ref[...] + (1.0 - ADAM_B2) * (gv * gv)
        m_hat = m_new / (1.0 - ADAM_B1 ** ADAM_STEP)
        v_hat = v_new / (1.0 - ADAM_B2 ** ADAM_STEP)
        d_ref[...] = -ADAM_LR * (m_hat / (jnp.sqrt(v_hat) + ADAM_EPS) + ADAM_WD * w_ref[...])
        nm_ref[...] = m_new
        nv_ref[...] = v_new

    both = pl.BlockSpec((None, br, cols), lambda l, i: (l, i, 0))
    one = pl.BlockSpec((br, cols), lambda l, i: (i, 0))
    return _call(body, name, (2, rows // br), in_specs=[both, one, one, both, both], out_specs=[both] * 4,
                 out_shape=[jax.ShapeDtypeStruct(w.shape, F32)] * 4, scratch_shapes=[],
                 sem=("arbitrary", "arbitrary"), operands=(w, g0, g1, m, v), comm=comm)


def _adamw_call(name, w, g, m, v):
    rows, cols = w.shape
    br = _rows_block(rows, cols)

    def body(w_ref, g_ref, m_ref, v_ref, d_ref, nm_ref, nv_ref):
        gv = g_ref[...]
        m_new = ADAM_B1 * m_ref[...] + (1.0 - ADAM_B1) * gv
        v_new = ADAM_B2 * v_ref[...] + (1.0 - ADAM_B2) * (gv * gv)
        m_hat = m_new / (1.0 - ADAM_B1 ** ADAM_STEP)
        v_hat = v_new / (1.0 - ADAM_B2 ** ADAM_STEP)
        d_ref[...] = -ADAM_LR * (m_hat / (jnp.sqrt(v_hat) + ADAM_EPS) + ADAM_WD * w_ref[...])
        nm_ref[...] = m_new
        nv_ref[...] = v_new

    spec = pl.BlockSpec((br, cols), lambda i: (i, 0))
    return pl.pallas_call(
        body, name=name, grid=(rows // br,), in_specs=[spec] * 4, out_specs=[spec] * 3,
        out_shape=[jax.ShapeDtypeStruct((rows, cols), F32)] * 3,
        compiler_params=_params(("arbitrary",)),
    )(w, g, m, v)


def _place():
    x, y, c = lax.axis_index("x"), lax.axis_index("y"), lax.axis_index("c")
    chips = [(1 - x, y), (x, 1 - y), (1 - x, 1 - y)]
    return x, y, c, 2 * x + y, chips


def _half_rows(ref, core):
    rh = ref.shape[-2] // 2
    rows = pl.ds(pl.multiple_of(core * rh, rh), rh)
    return ref.at[rows] if len(ref.shape) == 2 else ref.at[:, rows]


def _gather_comm(bufs, cw_shard=None):
    n = len(bufs)
    with_cw = cw_shard is not None

    def make(c_ins, c_outs, sems):
        ins, outs = c_ins[:n], c_outs[:n]
        send_sems, recv_sems, cw_send, cw_recv, cw_local = sems
        x, y, c, p, chips = _place()
        sibling = (x, y, 1 - c)
        qs = [2 * cx + cy for cx, cy in chips]
        hops = len(chips)

        def remote(k, j, src, dst, to):
            return pltpu.make_async_remote_copy(src_ref=src, dst_ref=dst, send_sem=send_sems.at[k, j],
                                                recv_sem=recv_sems.at[k, j], device_id=to, device_id_type=MESH)

        def cw_copy(j, dst, to):
            return pltpu.make_async_remote_copy(src_ref=c_ins[n], dst_ref=dst, send_sem=cw_send.at[j],
                                                recv_sem=cw_recv.at[j], device_id=to, device_id_type=MESH)

        def over_ici():
            return [remote(k, j, _half_rows(ins[k].at[p], c), _half_rows(outs[k].at[p], c), (*chip, c))
                    for j, chip in enumerate(chips) for k in range(n)]

        def passed_on():
            return [remote(k, hops + j, _half_rows(outs[k].at[qs[j]], c), _half_rows(outs[k].at[qs[j]], c), sibling)
                    for j in range(hops) for k in range(n)]

        def cw_copies():
            return [cw_copy(j, c_outs[n].at[p], (*chip, c)) for j, chip in enumerate(chips)]

        def cw_own():
            return pltpu.make_async_copy(c_ins[n], c_outs[n].at[p], cw_local)

        def start():
            for cp in over_ici():
                cp.start()
            if with_cw:
                cw_own().start()
                for cp in cw_copies():
                    cp.start()

        def finish():
            for j in range(hops):
                for k in range(n):
                    landed = _half_rows(outs[k].at[qs[j]], c)
                    remote(k, j, landed, landed, sibling).wait_recv()
                    remote(k, hops + j, landed, landed, sibling).start()
            for j in range(hops):
                for k in range(n):
                    other = _half_rows(outs[k].at[qs[j]], 1 - c)
                    remote(k, hops + j, other, other, sibling).wait_recv()
            if with_cw:
                for j in range(hops):
                    cw_copy(j, c_outs[n].at[qs[j]], sibling).wait_recv()
                cw_own().wait()
                for cp in cw_copies():
                    cp.wait_send()
            for cp in over_ici() + passed_on():
                cp.wait_send()

        return start, finish

    out_shape = [jax.ShapeDtypeStruct(b.shape, b.dtype) for b in bufs]
    if with_cw:
        out_shape.append(jax.ShapeDtypeStruct((N_CHIPS,) + cw_shard.shape, cw_shard.dtype))
    return dict(operands=list(bufs) + ([cw_shard] if with_cw else []), out_shape=out_shape,
                aliases={k: k for k in range(n)}, make=make,
                scratch=[pltpu.SemaphoreType.DMA((n, 6)), pltpu.SemaphoreType.DMA((n, 6)),
                         pltpu.SemaphoreType.DMA((3,)), pltpu.SemaphoreType.DMA((3,)), pltpu.SemaphoreType.DMA])


def _pair_exchange_call(l, gs):
    n = len(gs)

    def body(*refs):
        g, got = refs[:n], refs[n:2 * n]
        send_sems, recv_sems = refs[2 * n:]
        x, y, c, _, _ = _place()
        sibling = (x, y, 1 - c)
        copies = [pltpu.make_async_remote_copy(src_ref=_half_rows(g[k], 1 - c), dst_ref=got[k],
                                               send_sem=send_sems.at[k], recv_sem=recv_sems.at[k],
                                               device_id=sibling, device_id_type=MESH) for k in range(n)]
        for cp in copies:
            cp.start()
        for cp in copies:
            cp.wait_send()
            cp.wait_recv()

    return pl.pallas_call(
        body, name=f"grad_pair_exchange_l{l}",
        in_specs=[ANY] * n, out_specs=[ANY] * n,
        out_shape=[jax.ShapeDtypeStruct((g.shape[0], g.shape[1] // 2, g.shape[2]), g.dtype) for g in gs],
        scratch_shapes=[pltpu.SemaphoreType.DMA((n,)), pltpu.SemaphoreType.DMA((n,))],
    )(*gs)


def _exchange_comm(sums):
    n = len(sums)

    def make(ins, outs, sems):
        send_sems, recv_sems, local_sems = sems
        x, y, c, p, chips = _place()
        qs = [2 * cx + cy for cx, cy in chips]

        def remote(k, j, src, dst, to):
            return pltpu.make_async_remote_copy(src_ref=src, dst_ref=dst, send_sem=send_sems.at[k, j],
                                                recv_sem=recv_sems.at[k, j], device_id=to, device_id_type=MESH)

        def own():
            return [pltpu.make_async_copy(ins[k].at[p], outs[k].at[p], local_sems.at[k]) for k in range(n)]

        def sent():
            return [remote(k, j, ins[k].at[qs[j]], outs[k].at[p], (*chip, c))
                    for j, chip in enumerate(chips) for k in range(n)]

        def start():
            for cp in own() + sent():
                cp.start()

        def finish():
            for j in range(len(chips)):
                for k in range(n):
                    remote(k, j, ins[k].at[qs[j]], outs[k].at[qs[j]], (x, y, c)).wait_recv()
            for cp in sent():
                cp.wait_send()
            for cp in own():
                cp.wait()

        return start, finish

    return dict(operands=list(sums), out_shape=[jax.ShapeDtypeStruct(s.shape, s.dtype) for s in sums], aliases={},
                make=make, scratch=[pltpu.SemaphoreType.DMA((n, 3)), pltpu.SemaphoreType.DMA((n, 3)),
                                    pltpu.SemaphoreType.DMA((n,))])


def _pair_gather_call(l, halves):
    n = len(halves)

    def body(*refs):
        ins, outs = refs[:n], refs[n:2 * n]
        send_sems, recv_sems = refs[2 * n:]
        x, y, c, _, _ = _place()
        sibling = (x, y, 1 - c)

        def remote(k, src, dst):
            return pltpu.make_async_remote_copy(src_ref=src, dst_ref=dst, send_sem=send_sems.at[k],
                                                recv_sem=recv_sems.at[k], device_id=sibling, device_id_type=MESH)

        sent = [remote(k, _half_rows(ins[k], c), _half_rows(outs[k], c)) for k in range(n)]
        for cp in sent:
            cp.start()
        for k in range(n):
            other = _half_rows(outs[k], 1 - c)
            remote(k, other, other).wait_recv()
        for cp in sent:
            cp.wait_send()

    return pl.pallas_call(
        body, name=f"grad_pair_gather_l{l}",
        in_specs=[ANY] * n, out_specs=[ANY] * n,
        out_shape=[jax.ShapeDtypeStruct(h.shape, h.dtype) for h in halves],
        input_output_aliases={k: k for k in range(n)},
        scratch_shapes=[pltpu.SemaphoreType.DMA((n,)), pltpu.SemaphoreType.DMA((n,))],
    )(*halves)


def _all_sum_small_call(block, comm=None):
    m_per, n = block.shape

    def body(x_ref, sum_ref, all_ref, send_sems, recv_sems, local_sem):
        x, y, c, _, chip_list = _place()
        me, sibling = (x, y, c), (x, y, 1 - c)

        def rows(px, py, pc):
            return all_ref.at[pl.ds((4 * px + 2 * py + pc) * m_per, m_per), :]

        def copy(k, blk, to, src=None):
            return pltpu.make_async_remote_copy(src_ref=rows(*blk) if src is None else src, dst_ref=rows(*blk),
                                                send_sem=send_sems.at[k], recv_sem=recv_sems.at[k],
                                                device_id=to, device_id_type=MESH)

        mine = pltpu.make_async_copy(x_ref, rows(*me), local_sem)
        mine.start()
        first = [copy(0, me, sibling, src=x_ref)]
        first += [copy(1 + j, me, (*chip, c), src=x_ref) for j, chip in enumerate(chip_list)]
        for cp in first:
            cp.start()
        passed = [copy(4 + j, (*chip, c), sibling) for j, chip in enumerate(chip_list)]
        for j, chip in enumerate(chip_list):
            copy(1 + j, (*chip, c), me).wait_recv()
            passed[j].start()
        copy(0, sibling, me).wait_recv()
        for j, chip in enumerate(chip_list):
            copy(4 + j, (*chip, 1 - c), me).wait_recv()
        for cp in first + passed:
            cp.wait_send()
        mine.wait()
        total = all_ref[0:m_per, :]
        for d in range(1, N_DEV):
            total = total + all_ref[d * m_per:(d + 1) * m_per, :]
        sum_ref[...] = total

    vmem = pl.BlockSpec(memory_space=pltpu.VMEM)
    res, c_res = _call(
        body, "small_all_sum", (), in_specs=[vmem], out_specs=[vmem, vmem],
        out_shape=[jax.ShapeDtypeStruct((m_per, n), F32), jax.ShapeDtypeStruct((N_DEV * m_per, n), F32)],
        scratch_shapes=[pltpu.SemaphoreType.DMA((7,)), pltpu.SemaphoreType.DMA((7,)), pltpu.SemaphoreType.DMA],
        sem=None, operands=(block,), comm=comm)
    return res[0], c_res


SMALL_NAMES = ["norm1_g", "sgu_ln_g", "sgu_ln_b", "sgu_w", "sgu_b", "conv_b", "conv_ln_g", "conv_ln_b", "norm2_g",
               "final_g"]
MIX_NAMES = ["w_in", "w_out"]
FFN_NAMES = ["w_ff1", "w_ff2"]
BIG_NAMES = MIX_NAMES + FFN_NAMES


def _rows128(a):
    return a.reshape(-1, LANES)


def kernel(x, norm1_g, w_in, sgu_ln_g, sgu_ln_b, sgu_w, sgu_b, conv_w, conv_b, conv_ln_g, conv_ln_b, w_out, norm2_g, w_ff1, w_ff2, final_g, loss_target, m_norm1_g, m_w_in, m_sgu_ln_g, m_sgu_ln_b, m_sgu_w, m_sgu_b, m_conv_w, m_conv_b, m_conv_ln_g, m_conv_ln_b, m_w_out, m_norm2_g, m_w_ff1, m_w_ff2, m_final_g, v_norm1_g, v_w_in, v_sgu_ln_g, v_sgu_ln_b, v_sgu_w, v_sgu_b, v_conv_w, v_conv_b, v_conv_ln_g, v_conv_ln_b, v_w_out, v_norm2_g, v_w_ff1, v_w_ff2, v_final_g):
    w = dict(norm1_g=norm1_g, w_in=w_in, sgu_ln_g=sgu_ln_g, sgu_ln_b=sgu_ln_b, sgu_w=sgu_w, sgu_b=sgu_b,
             conv_w=conv_w, conv_b=conv_b, conv_ln_g=conv_ln_g, conv_ln_b=conv_ln_b, w_out=w_out, norm2_g=norm2_g,
             w_ff1=w_ff1, w_ff2=w_ff2, final_g=final_g)
    m = dict(norm1_g=m_norm1_g, w_in=m_w_in, sgu_ln_g=m_sgu_ln_g, sgu_ln_b=m_sgu_ln_b, sgu_w=m_sgu_w, sgu_b=m_sgu_b,
             conv_w=m_conv_w, conv_b=m_conv_b, conv_ln_g=m_conv_ln_g, conv_ln_b=m_conv_ln_b, w_out=m_w_out,
             norm2_g=m_norm2_g, w_ff1=m_w_ff1, w_ff2=m_w_ff2, final_g=m_final_g)
    v = dict(norm1_g=v_norm1_g, w_in=v_w_in, sgu_ln_g=v_sgu_ln_g, sgu_ln_b=v_sgu_ln_b, sgu_w=v_sgu_w, sgu_b=v_sgu_b,
             conv_w=v_conv_w, conv_b=v_conv_b, conv_ln_g=v_conv_ln_g, conv_ln_b=v_conv_ln_b, w_out=v_w_out,
             norm2_g=v_norm2_g, w_ff1=v_w_ff1, w_ff2=v_w_ff2, final_g=v_final_g)
    depth = w_in.shape[0]
    assert depth == 2, "core c owns layer c of every gradient"
    T, D = x.shape[1], x.shape[2]
    heads = sgu_w.shape[1]
    da = heads * CHUNK
    core = lax.axis_index("c")
    chip = 2 * lax.axis_index("x") + lax.axis_index("y")

    core_arr = core.reshape(1).astype(jnp.int32)
    chip_arr = chip.reshape(1).astype(jnp.int32)
    cw_pad = jnp.pad(conv_w, ((0, 0), (0, HALO - CONV_TAPS), (0, 0)))
    own = [{name: _cast_own_call(f"cast_{name}_l{l}", chip_arr, w[name], l) for name in BIG_NAMES}
           for l in range(depth)]
    gather_of = lambda l, group: _gather_comm([own[l][name] for name in group])
    win_g, wout_g, cw_g = _alone("gather_mix_l0", _gather_comm([own[0][name] for name in MIX_NAMES], cw_pad))
    cw_full = jnp.transpose(cw_g, (1, 2, 0, 3)).reshape(depth, HALO, da)

    ws_t = jnp.swapaxes(sgu_w, -1, -2)
    b_full = jnp.broadcast_to(sgu_b[..., None], sgu_w.shape)
    row = lambda a, l: a[l:l + 1]

    xs, projs, cpres, x1s, zs = [], [], [], [], []
    h = x.reshape(T, D)
    gathered = []
    w1_g = None
    for l in range(depth):
        xs.append(h)
        (x1, proj, *cpre), late = _mix_fwd_call(
            l, h, row(norm1_g, l), win_g, row(sgu_ln_g, l), row(sgu_ln_b, l), sgu_w[l], b_full[l], cw_full[l],
            row(conv_b, l), row(conv_ln_g, l), row(conv_ln_b, l), wout_g,
            comm=gather_of(l, FFN_NAMES if w1_g is None else FFN_NAMES[1:]))
        w1_g, w2_g = late if len(late) == 2 else (w1_g, late[0])
        gathered.append((win_g, wout_g, w1_g, w2_g))
        (h, z), nxt = _ffn_fwd_call(l, x1, row(norm2_g, l), w1_g, w2_g,
                                    comm=gather_of(l + 1, MIX_NAMES + FFN_NAMES[:1]) if l + 1 < depth else None)
        if nxt:
            win_g, wout_g, w1_g = nxt
        projs.append(proj)
        cpres.append(cpre)
        x1s.append(x1)
        zs.append(z)
    dx, dxb, loss, d_final_g = _loss_call(h, final_g.reshape(1, D), loss_target.reshape(T, D))

    big = {name: [None] * depth for name in BIG_NAMES}
    reduced = {name: [None] * depth for name in BIG_NAMES}
    small = {name: [None] * depth for name in SMALL_NAMES[:-1] + ["conv_w"]}

    def pair_sums(l, group):
        got = _pair_exchange_call(f"{group[0]}_l{l}", [big[name][l] for name in group])
        return [_pair_sum_call(f"pair_sum_{name}_l{l}", core_arr, big[name][l], recv) for name, recv in zip(group, got)]

    def finish_reduce(l, group, parts):
        halves = [_chip_sum_call(f"chip_sum_{name}_l{l}", core_arr, part) for name, part in zip(group, parts)]
        for name, full in zip(group, _pair_gather_call(f"{group[0]}_l{l}", halves)):
            reduced[name][l] = full

    pending = None
    for l in reversed(range(depth)):
        win_g, wout_g, w1_g, w2_g = gathered[l]
        (dx1, dx1b, dz, f_t, h2_t, dg2), parts = _ffn_bwd_call(
            l, dx, dxb, x1s[l], zs[l], row(norm2_g, l), w1_g, w2_g,
            comm=_exchange_comm(pending[2]) if pending else None)
        if pending:
            finish_reduce(pending[0], pending[1], parts)
        big["w_ff2"][l] = _wgrad_call(f"wgrad_ff2_l{l}", f_t, dxb, False, 2)
        big["w_ff1"][l] = _wgrad_call(f"wgrad_ff1_l{l}", h2_t, dz, True, 2)
        ffn_sums = pair_sums(l, FFN_NAMES)
        ((dx, dxb, dproj, h1_t, mix_t, dg1, dlng, dlnb, dws, dbs, dcw, dcb, dcg, dcbeta), parts) = _mix_bwd_call(
            l, dx1, dx1b, xs[l], projs[l], *cpres[l], row(norm1_g, l), win_g, row(sgu_ln_g, l), row(sgu_ln_b, l), sgu_w[l],
            ws_t[l], b_full[l], cw_full[l], row(conv_b, l), row(conv_ln_g, l), row(conv_ln_b, l), wout_g,
            comm=_exchange_comm(ffn_sums))
        finish_reduce(l, FFN_NAMES, parts)
        big["w_out"][l] = _wgrad_call(f"wgrad_out_l{l}", mix_t, dx1b, False, 1)
        big["w_in"][l] = _wgrad_call(f"wgrad_in_l{l}", h1_t, dproj, True, 1)
        pending = (l, MIX_NAMES, pair_sums(l, MIX_NAMES))
        small["norm1_g"][l] = dg1[0]
        small["sgu_ln_g"][l] = dlng[0]
        small["sgu_ln_b"][l] = dlnb[0]
        small["sgu_w"][l] = dws
        small["sgu_b"][l] = dbs[:, :, 0]
        small["conv_w"][l] = dcw[:CONV_TAPS]
        small["conv_b"][l] = dcb[0]
        small["conv_ln_g"][l] = dcg[0]
        small["conv_ln_b"][l] = dcbeta[0]
        small["norm2_g"][l] = dg2[0]
    grad_x = dx.reshape(x.shape)

    small_local = {name: jnp.stack(small[name]) for name in small}
    small_local["final_g"] = d_final_g[0]
    pieces = [_rows128(small_local[name]) for name in SMALL_NAMES]
    pieces.append(_rows128(small_local["conv_w"]))
    pieces.append(jnp.broadcast_to(loss, (8, LANES)))
    offsets = [0]
    for piece in pieces:
        offsets.append(offsets[-1] + piece.shape[0])
    summed, parts = _all_sum_small_call(jnp.concatenate(pieces, axis=0), comm=_exchange_comm(pending[2]))
    finish_reduce(pending[0], pending[1], parts)
    n_small = offsets[len(SMALL_NAMES)]
    loss_out = summed[offsets[-2], 0]
    small_grads = {name: summed[offsets[k]:offsets[k + 1]].reshape(w[name].shape)
                   for k, name in enumerate(SMALL_NAMES)}
    conv_w_full = summed[offsets[-3]:offsets[-2]].reshape(depth, CONV_TAPS, da)
    conv_w_grad = lax.dynamic_slice_in_dim(conv_w_full, chip * conv_w.shape[-1], conv_w.shape[-1], axis=2)

    grads, delta, new_m, new_v = {}, {}, {}, {}
    for name in BIG_NAMES:
        (grads[name], delta[name], new_m[name], new_v[name]), _ = _adamw_layers_call(
            f"adamw_{name}", w[name], reduced[name][0], reduced[name][1], m[name], v[name])
    pack = lambda src: jnp.concatenate([_rows128(src[name]) for name in SMALL_NAMES], axis=0)
    d_, m_, v_ = _adamw_call("adamw_small", pack(w), summed[:n_small], pack(m), pack(v))
    for k, name in enumerate(SMALL_NAMES):
        sl = slice(offsets[k], offsets[k + 1])
        grads[name] = small_grads[name]
        delta[name] = d_[sl].reshape(w[name].shape)
        new_m[name] = m_[sl].reshape(w[name].shape)
        new_v[name] = v_[sl].reshape(w[name].shape)
    cshape = conv_w.shape
    flat = lambda a: a.reshape(-1, cshape[-1])
    d_, m_, v_ = _adamw_call("adamw_conv_w", flat(conv_w), flat(conv_w_grad), flat(m["conv_w"]), flat(v["conv_w"]))
    grads["conv_w"] = conv_w_grad
    delta["conv_w"], new_m["conv_w"], new_v["conv_w"] = d_.reshape(cshape), m_.reshape(cshape), v_.reshape(cshape)

    order = ["norm1_g", "w_in", "sgu_ln_g", "sgu_ln_b", "sgu_w", "sgu_b", "conv_w", "conv_b", "conv_ln_g",
             "conv_ln_b", "w_out", "norm2_g", "w_ff1", "w_ff2", "final_g"]
    return (loss_out, grad_x, *[grads[n] for n in order], *[delta[n] for n in order],
            *[new_m[n] for n in order], *[new_v[n] for n in order])
```

```python
import functools

import jax
import jax.numpy as jnp
from jax import lax
from jax.experimental import pallas as pl
from jax.experimental.pallas import tpu as pltpu

F32 = jnp.float32
BF16 = jnp.bfloat16
MESH = pl.DeviceIdType.MESH

EPS = 1e-6
CHUNK = 128
CONV_TAPS = 31
HALO = 32
N_CHIPS = 4
N_DEV = 8
LANES = 128
SUBLANES = 8

ADAM_LR = 0.001
ADAM_B1 = 0.9
ADAM_B2 = 0.999
ADAM_EPS = 1e-08
ADAM_WD = 0.01
ADAM_STEP = 10

TM_MIX = 256
TM_FFN = 512
TM_FFN_FWD = 1024
V7X_VMEM_BYTES = 64 * 2 ** 20
VMEM_LIMIT = V7X_VMEM_BYTES - 8 * 2 ** 20


def _params(sem=None):
    return pltpu.CompilerParams(dimension_semantics=sem, vmem_limit_bytes=VMEM_LIMIT)


def _gelu_and_grad(x):
    gauss = jnp.exp(-0.5 * x * x)
    t = 1.0 / (1.0 + (0.3275911 * 0.7071067811865476) * jnp.abs(x))
    poly = t * (0.254829592 + t * (-0.284496736 + t * (1.421413741 + t * (-1.453152027 + t * 1.061405429))))
    erf_abs = 1.0 - poly * gauss
    cdf = 0.5 * (1.0 + jnp.where(x < 0, -erf_abs, erf_abs))
    return x * cdf, cdf + x * (gauss * 0.3989422804014327)


def _sigmoid(x):
    return 1.0 / (1.0 + jnp.exp(-x))


def _dot_nt(a, b):
    return lax.dot_general(a.astype(BF16), b.astype(BF16), (((1,), (1,)), ((), ())), preferred_element_type=F32)


def _rms_fwd(x, g):
    r = lax.rsqrt(jnp.mean(x * x, axis=-1, keepdims=True) + EPS)
    xh = x * r
    return r, xh, xh * g


def _rms_bwd(dh, xh, r, g):
    dxh = dh * g
    return r * (dxh - xh * jnp.mean(dxh * xh, axis=-1, keepdims=True))


def _ln_stats(x):
    mu = jnp.mean(x, axis=-1, keepdims=True)
    xc = x - mu
    rs = lax.rsqrt(jnp.mean(xc * xc, axis=-1, keepdims=True) + EPS)
    return xc * rs, rs


def _ln_bwd(dxh, xh, rs):
    return rs * (dxh - jnp.mean(dxh, axis=-1, keepdims=True) - xh * jnp.mean(dxh * xh, axis=-1, keepdims=True))


def _group_ln_fwd(c):
    parts, rss = [], []
    for j in range(c.shape[1] // CHUNK):
        xh, rs = _ln_stats(c[:, j * CHUNK:(j + 1) * CHUNK])
        parts.append(xh)
        rss.append(rs)
    return jnp.concatenate(parts, axis=1), rss


def _group_ln_bwd(dxh, xh, rss):
    parts = []
    for j, rs in enumerate(rss):
        cols = slice(j * CHUNK, (j + 1) * CHUNK)
        parts.append(_ln_bwd(dxh[:, cols], xh[:, cols], rs))
    return jnp.concatenate(parts, axis=1)


def _sublane_shifts(ext):
    n = ext.shape[0]
    return [ext if b == 0 else pltpu.roll(ext, n - b, 0) for b in range(SUBLANES)]


def _rows_from(shifts, off, tm):
    a, b = divmod(off, SUBLANES)
    return shifts[b][a * SUBLANES:a * SUBLANES + tm]


def _tril_mask():
    t = lax.broadcasted_iota(jnp.int32, (CHUNK, CHUNK), 0)
    s = lax.broadcasted_iota(jnp.int32, (CHUNK, CHUNK), 1)
    return t >= s


def _mix_forward(ua, va, vb, gb, g_halo, ws_ref, bfull_ref, lng, lnb, cw_ref, cb, cg, cbeta, mixed_scr, cpre=None,
                 kept=None):
    tm, da = vb.shape
    heads = da // CHUNK
    if kept is None:
        u, du_fac = _gelu_and_grad(ua)
        vg, dvg_fac = _gelu_and_grad(va)
        vh, v_rs = _ln_stats(vg)
    else:
        u, du_fac, dvg_fac, vh, v_rs = kept
    v = vh * lng + lnb
    mask = _tril_mask()
    wm = [jnp.where(mask, ws_ref[h], 0.0).astype(BF16) for h in range(heads)]
    vb16 = v.astype(BF16)
    for ci in range(tm // CHUNK):
        rows = slice(ci * CHUNK, (ci + 1) * CHUNK)
        for h in range(heads):
            cols = slice(h * CHUNK, (h + 1) * CHUNK)
            mixed_scr[rows, cols] = jnp.dot(wm[h], vb16[rows, cols], preferred_element_type=F32) + bfull_ref[h]
    mixed = mixed_scr[...]
    a = u * mixed

    sg = _sigmoid(gb)
    g = vb * sg
    g_shifts = _sublane_shifts(jnp.concatenate([g_halo, g], axis=0))
    if cpre is None:
        cpre = jnp.zeros_like(g) + cb
        for k in range(CONV_TAPS):
            cpre = cpre + _rows_from(g_shifts, HALO - (CONV_TAPS - 1) + k, tm) * cw_ref[k:k + 1, :]
    chh, c_rss = _group_ln_fwd(cpre)
    cn = chh * cg + cbeta
    sc = _sigmoid(cn)
    cout = cn * sc
    return dict(u=u, du_fac=du_fac, dvg_fac=dvg_fac, vh=vh, v_rs=v_rs, v16=vb16, mixed=mixed, a=a, sg=sg, g=g,
                g_shifts=g_shifts, cpre=cpre, chh=chh, c_rss=c_rss, cn=cn, sc=sc, cout=cout)


ANY = pl.BlockSpec(memory_space=pl.ANY)


def _call(body, name, grid, in_specs, out_specs, out_shape, scratch_shapes, sem, operands, comm=None):
    if comm is None:
        res = pl.pallas_call(body, name=name, grid=grid, in_specs=in_specs, out_specs=out_specs, out_shape=out_shape,
                             scratch_shapes=scratch_shapes, compiler_params=_params(sem))(*operands)
        return res, []
    n_in, n_out, n_scr = len(in_specs), len(out_specs), len(scratch_shapes)
    n_cin, n_cout = len(comm["operands"]), len(comm["out_shape"])

    def fused(*refs):
        ins, refs = refs[:n_in], refs[n_in:]
        c_ins, refs = refs[:n_cin], refs[n_cin:]
        outs, refs = refs[:n_out], refs[n_out:]
        c_outs, refs = refs[:n_cout], refs[n_cout:]
        scr, c_sems = refs[:n_scr], refs[n_scr:]
        start, finish = comm["make"](c_ins, c_outs, c_sems)
        if grid:
            first = functools.reduce(jnp.logical_and, [pl.program_id(a) == 0 for a in range(len(grid))])
            last = functools.reduce(jnp.logical_and, [pl.program_id(a) == grid[a] - 1 for a in range(len(grid))])
            pl.when(first)(start)
            body(*ins, *outs, *scr)
            pl.when(last)(finish)
        else:
            start()
            body(*ins, *outs, *scr)
            finish()

    res = pl.pallas_call(
        fused, name=name, grid=grid,
        in_specs=list(in_specs) + [ANY] * n_cin, out_specs=list(out_specs) + [ANY] * n_cout,
        out_shape=list(out_shape) + list(comm["out_shape"]),
        input_output_aliases={n_in + a: n_out + b for a, b in comm["aliases"].items()},
        scratch_shapes=list(scratch_shapes) + list(comm["scratch"]),
        compiler_params=_params(sem),
    )(*operands, *comm["operands"])
    return res[:n_out], res[n_out:]


def _alone(name, comm):
    n_cin, n_cout = len(comm["operands"]), len(comm["out_shape"])

    def body(*refs):
        start, finish = comm["make"](refs[:n_cin], refs[n_cin:n_cin + n_cout], refs[n_cin + n_cout:])
        start()
        finish()

    return pl.pallas_call(
        body, name=name, in_specs=[ANY] * n_cin, out_specs=[ANY] * n_cout, out_shape=list(comm["out_shape"]),
        input_output_aliases=dict(comm["aliases"]), scratch_shapes=list(comm["scratch"]),
    )(*comm["operands"])


def _mix_fwd_call(l, x, g1, win_g, lng, lnb, ws, bfull, cw, cb, cg, cbeta, wout_g, comm=None):
    T, D = x.shape
    tm = min(TM_MIX, T)
    n_t = T // tm
    da = win_g.shape[-1]
    wo_rows = wout_g.shape[1]

    def body(x_ref, g1_ref, win_ref, lng_ref, lnb_ref, ws_ref, bf_ref, cw_ref, cb_ref, cg_ref, cbeta_ref, wout_ref,
             x1_ref, proj_ref, cpre_ref, kept_ref, rs_ref, halo_scr, mixed_scr):
        i = pl.program_id(0)

        @pl.when(i == 0)
        def _():
            halo_scr[...] = jnp.zeros_like(halo_scr)

        x_t = x_ref[...]
        _, _, h1 = _rms_fwd(x_t, g1_ref[...])
        h1b = h1.astype(BF16)
        ps = []
        for p in range(N_CHIPS):
            pp = jnp.dot(h1b, win_ref[p], preferred_element_type=F32)
            proj_ref[:, p * da:(p + 1) * da] = pp
            ps.append(pp)
        f = _mix_forward(ps[0], ps[1], ps[2], ps[3], halo_scr[...], ws_ref, bf_ref, lng_ref[...], lnb_ref[...],
                         cw_ref, cb_ref[...], cg_ref[...], cbeta_ref[...], mixed_scr)
        halo_scr[...] = f["g"][tm - HALO:, :]
        cpre_ref[...] = f["cpre"]
        for n, key in enumerate(("u", "du_fac", "dvg_fac", "vh")):
            kept_ref[:, n * da:(n + 1) * da] = f[key]
        rs_ref[...] = jnp.broadcast_to(f["v_rs"], rs_ref.shape)
        mix =jnp.concatenate([f["a"], f["cout"]], axis=1).astype(BF16)
        acc = x_t
        for q in range(N_CHIPS):
            acc = acc + jnp.dot(mix[:, q * wo_rows:(q + 1) * wo_rows], wout_ref[q], preferred_element_type=F32)
        x1_ref[...] = acc

    full = lambda a: pl.BlockSpec(a.shape, lambda i: (0,) * a.ndim)
    return _call(
        body, f"mix_fwd_l{l}", (n_t,),
        in_specs=[
            pl.BlockSpec((tm, D), lambda i: (i, 0)),
            full(g1),
            pl.BlockSpec((N_CHIPS, D, da), lambda i: (0, 0, 0)),
            full(lng), full(lnb), full(ws), full(bfull), full(cw), full(cb), full(cg), full(cbeta),
            pl.BlockSpec((N_CHIPS, wo_rows, D), lambda i: (0, 0, 0)),
        ],
        out_specs=[
            pl.BlockSpec((tm, D), lambda i: (i, 0)),
            pl.BlockSpec((tm, N_CHIPS * da), lambda i: (i, 0)),
            pl.BlockSpec((tm, da), lambda i: (i, 0)),
            pl.BlockSpec((tm, 4 * da), lambda i: (i, 0)),
            pl.BlockSpec((tm, LANES), lambda i: (i, 0)),
        ],
        out_shape=[jax.ShapeDtypeStruct((T, D), F32), jax.ShapeDtypeStruct((T, N_CHIPS * da), F32),
                   jax.ShapeDtypeStruct((T, da), F32), jax.ShapeDtypeStruct((T, 4 * da), F32),
                   jax.ShapeDtypeStruct((T, LANES), F32)],
        scratch_shapes=[pltpu.VMEM((HALO, da), F32), pltpu.VMEM((tm, da), F32)],
        sem=("arbitrary",),
        operands=(x, g1, win_g, lng, lnb, ws, bfull, cw, cb, cg, cbeta, wout_g), comm=comm)


def _ffn_tile(T, tile=TM_FFN):
    return min(tile, max(T // 2, CHUNK))


def _ffn_fwd_call(l, x1, g2, w1_g, w2_g, comm=None):
    T, D = x1.shape
    tm = _ffn_tile(T, TM_FFN_FWD)
    n_t = T // tm
    ffb = w1_g.shape[-1]

    def body(x1_ref, g2_ref, w1_ref, w2_ref, x2_ref, z_ref, h2_scr, acc_scr):
        p = pl.program_id(1)

        @pl.when(p == 0)
        def _():
            _, _, h2 = _rms_fwd(x1_ref[...], g2_ref[...])
            h2_scr[...] = h2.astype(BF16)
            acc_scr[...] = jnp.zeros_like(acc_scr)

        z = jnp.dot(h2_scr[...], w1_ref[...], preferred_element_type=F32)
        z_ref[...] = z.astype(BF16)
        rz = jnp.maximum(z, 0.0)
        acc_scr[...] += jnp.dot((rz * rz).astype(BF16), w2_ref[...], preferred_element_type=F32)

        @pl.when(p == N_CHIPS - 1)
        def _():
            x2_ref[...] = x1_ref[...] + acc_scr[...]

    return _call(
        body, f"ffn_fwd_l{l}", (n_t, N_CHIPS),
        in_specs=[
            pl.BlockSpec((tm, D), lambda i, p: (i, 0)),
            pl.BlockSpec(g2.shape, lambda i, p: (0, 0)),
            pl.BlockSpec((None, D, ffb), lambda i, p: (p, 0, 0)),
            pl.BlockSpec((None, ffb, D), lambda i, p: (p, 0, 0)),
        ],
        out_specs=[
            pl.BlockSpec((tm, D), lambda i, p: (i, 0)),
            pl.BlockSpec((tm, ffb), lambda i, p: (i, p)),
        ],
        out_shape=[jax.ShapeDtypeStruct((T, D), F32), jax.ShapeDtypeStruct((T, N_CHIPS * ffb), BF16)],
        scratch_shapes=[pltpu.VMEM((tm, D), BF16), pltpu.VMEM((tm, D), F32)],
        sem=("arbitrary", "arbitrary"), operands=(x1, g2, w1_g, w2_g), comm=comm)


def _loss_call(x, gf, target):
    T, D = x.shape
    tm = min(TM_FFN, T)
    n_t = T // tm

    def body(x_ref, gf_ref, t_ref, dx_ref, dxb_ref, loss_ref, dgf_ref):
        i = pl.program_id(0)

        @pl.when(i == 0)
        def _():
            loss_ref[...] = jnp.zeros_like(loss_ref)
            dgf_ref[...] = jnp.zeros_like(dgf_ref)

        g = gf_ref[...]
        r, xh, y = _rms_fwd(x_ref[...], g)
        e = y - t_ref[...]
        per_tok = jnp.sum(e * e, axis=-1, keepdims=True) * (1.0 / D)
        loss_ref[...] += 0.5 * jnp.sum(per_tok, axis=0, keepdims=True)
        dy = e * (1.0 / D)
        dgf_ref[...] += jnp.sum(dy * xh, axis=0, keepdims=True)
        dx = _rms_bwd(dy, xh, r, g)
        dx_ref[...] = dx
        dxb_ref[...] = dx.astype(BF16)

    return pl.pallas_call(
        body, name="loss_head",
        grid=(n_t,),
        in_specs=[
            pl.BlockSpec((tm, D), lambda i: (i, 0)),
            pl.BlockSpec(gf.shape, lambda i: (0, 0)),
            pl.BlockSpec((tm, D), lambda i: (i, 0)),
        ],
        out_specs=[
            pl.BlockSpec((tm, D), lambda i: (i, 0)),
            pl.BlockSpec((tm, D), lambda i: (i, 0)),
            pl.BlockSpec((1, 1), lambda i: (0, 0)),
            pl.BlockSpec((1, D), lambda i: (0, 0)),
        ],
        out_shape=[jax.ShapeDtypeStruct((T, D), F32), jax.ShapeDtypeStruct((T, D), BF16),
                   jax.ShapeDtypeStruct((1, 1), F32), jax.ShapeDtypeStruct((1, D), F32)],
        compiler_params=_params(("arbitrary",)),
    )(x, gf, target)


def _ffn_bwd_call(l, dx2, dx2b, x1, z, g2, w1_g, w2_g, comm=None):
    T, D = x1.shape
    tm = _ffn_tile(T)
    n_t = T // tm
    ffb = w1_g.shape[-1]

    def body(dx2_ref, dx2b_ref, x1_ref, z_ref, g2_ref, w1_ref, w2_ref,
             dx1_ref, dx1b_ref, dz_ref, ft_ref, h2t_ref, dg2_ref, acc_scr):
        i = pl.program_id(0)
        p = pl.program_id(1)

        @pl.when(jnp.logical_and(i == 0, p == 0))
        def _():
            dg2_ref[...] = jnp.zeros_like(dg2_ref)

        @pl.when(p == 0)
        def _():
            _, _, h2 = _rms_fwd(x1_ref[...], g2_ref[...])
            h2t_ref[...] = h2.T.astype(BF16)
            acc_scr[...] = jnp.zeros_like(acc_scr)

        rz = jnp.maximum(z_ref[...].astype(F32), 0.0)
        ft_ref[...] = (rz * rz).T.astype(BF16)
        df = _dot_nt(dx2b_ref[...], w2_ref[...])
        dz = (df * (2.0 * rz)).astype(BF16)
        dz_ref[...] = dz
        acc_scr[...] += _dot_nt(dz, w1_ref[...])

        @pl.when(p == N_CHIPS - 1)
        def _():
            g = g2_ref[...]
            r, xh, _ = _rms_fwd(x1_ref[...], g)
            dh2 = acc_scr[...]
            dg2_ref[...] += jnp.sum(dh2 * xh, axis=0, keepdims=True)
            dx1 = dx2_ref[...] + _rms_bwd(dh2, xh, r, g)
            dx1_ref[...] = dx1
            dx1b_ref[...] = dx1.astype(BF16)

    return _call(
        body, f"ffn_bwd_l{l}", (n_t, N_CHIPS),
        in_specs=[
            pl.BlockSpec((tm, D), lambda i, p: (i, 0)),
            pl.BlockSpec((tm, D), lambda i, p: (i, 0)),
            pl.BlockSpec((tm, D), lambda i, p: (i, 0)),
            pl.BlockSpec((tm, ffb), lambda i, p: (i, p)),
            pl.BlockSpec(g2.shape, lambda i, p: (0, 0)),
            pl.BlockSpec((None, D, ffb), lambda i, p: (p, 0, 0)),
            pl.BlockSpec((None, ffb, D), lambda i, p: (p, 0, 0)),
        ],
        out_specs=[
            pl.BlockSpec((tm, D), lambda i, p: (i, 0)),
            pl.BlockSpec((tm, D), lambda i, p: (i, 0)),
            pl.BlockSpec((tm, ffb), lambda i, p: (i, p)),
            pl.BlockSpec((ffb, tm), lambda i, p: (p, i)),
            pl.BlockSpec((D, tm), lambda i, p: (0, i)),
            pl.BlockSpec((1, D), lambda i, p: (0, 0)),
        ],
        out_shape=[jax.ShapeDtypeStruct((T, D), F32), jax.ShapeDtypeStruct((T, D), BF16),
                   jax.ShapeDtypeStruct((T, N_CHIPS * ffb), BF16), jax.ShapeDtypeStruct((N_CHIPS * ffb, T), BF16),
                   jax.ShapeDtypeStruct((D, T), BF16), jax.ShapeDtypeStruct((1, D), F32)],
        scratch_shapes=[pltpu.VMEM((tm, D), F32)],
        sem=("arbitrary", "arbitrary"), operands=(dx2, dx2b, x1, z, g2, w1_g, w2_g), comm=comm)


def _mix_bwd_call(l, dx1, dx1b, x, proj, cpre, kept, v_rs, g1, win_g, lng, lnb, ws, wst, bfull, cw, cb, cg, cbeta, wout_g,
                  comm=None):
    T, D = x.shape
    tm = min(TM_MIX, T)
    n_t = T // tm
    da = win_g.shape[-1]
    heads = da // CHUNK
    wo_rows = wout_g.shape[1]
    halo_blocks = tm // HALO

    def body(dx1_ref, dx1b_ref, x_ref, proj_ref, vbh_ref, gbh_ref, cpre_ref, kept_ref, rs_ref, g1_ref, win_ref, lng_ref, lnb_ref, ws_ref, wst_ref,
             bf_ref, cw_ref, cb_ref, cg_ref, cbeta_ref, wout_ref,
             dx_ref, dxb_ref, dproj_ref, h1t_ref, mixt_ref,
             dg1_ref, dlng_ref, dlnb_ref, dws_ref, dbs_ref, dcw_ref, dcb_ref, dcg_ref, dcbeta_ref,
             carry_scr, mixed_scr, dv_scr):
        i = pl.program_id(0)
        tile = n_t - 1 - i

        @pl.when(i == 0)
        def _():
            carry_scr[...] = jnp.zeros_like(carry_scr)
            for ref in (dg1_ref, dlng_ref, dlnb_ref, dws_ref, dbs_ref, dcw_ref, dcb_ref, dcg_ref, dcbeta_ref):
                ref[...] = jnp.zeros_like(ref)

        g1v = g1_ref[...]
        r, xh, h1 = _rms_fwd(x_ref[...], g1v)
        h1t_ref[...] = h1.T.astype(BF16)

        vb = proj_ref[:, 0:da]
        gb = proj_ref[:, da:2 * da]
        g_halo = jnp.where(tile > 0, vbh_ref[...] * _sigmoid(gbh_ref[...]), 0.0)
        lng_v, cg_v = lng_ref[...], cg_ref[...]
        kept = tuple(kept_ref[:, n * da:(n + 1) * da] for n in range(4)) + (rs_ref[:, 0:1],)
        f = _mix_forward(None, None, vb, gb, g_halo, ws_ref, bf_ref, lng_v, lnb_ref[...], cw_ref, cb_ref[...], cg_v,
                         cbeta_ref[...], mixed_scr, cpre=cpre_ref[...], kept=kept)
        mixt_ref[0:da, :] = f["a"].T.astype(BF16)
        mixt_ref[da:2 * da, :] = f["cout"].T.astype(BF16)

        dxo = dx1b_ref[...]
        dmix = jnp.concatenate([_dot_nt(dxo, wout_ref[q]) for q in range(N_CHIPS)], axis=1)
        da_ = dmix[:, :da]
        dc_ = dmix[:, da:]

        dua = da_ * f["mixed"] * f["du_fac"]
        dmixed = (da_ * f["u"]).astype(BF16)
        mask_t = (lax.broadcasted_iota(jnp.int32, (CHUNK, CHUNK), 1)
                  >= lax.broadcasted_iota(jnp.int32, (CHUNK, CHUNK), 0))
        wmt =[jnp.where(mask_t, wst_ref[h], 0.0).astype(BF16) for h in range(heads)]
        mask = _tril_mask()
        v16 = f["v16"]
        for h in range(heads):
            cols = slice(h * CHUNK, (h + 1) * CHUNK)
            dws_h = jnp.zeros((CHUNK, CHUNK), F32)
            dbs_h = jnp.zeros((CHUNK, CHUNK), F32)
            for ci in range(tm // CHUNK):
                rows = slice(ci * CHUNK, (ci + 1) * CHUNK)
                dm = dmixed[rows, cols]
                dv_scr[rows, cols] = jnp.dot(wmt[h], dm, preferred_element_type=F32)
                dws_h = dws_h + _dot_nt(dm, v16[rows, cols])
                dbs_h = dbs_h + dm.astype(F32)
            dws_ref[h] += jnp.where(mask, dws_h, 0.0)
            dbs_ref[h] += jnp.broadcast_to(jnp.sum(dbs_h, axis=1, keepdims=True), (CHUNK, CHUNK))
        dv = dv_scr[...]
        dlng_ref[...] += jnp.sum(dv * f["vh"], axis=0, keepdims=True)
        dlnb_ref[...] += jnp.sum(dv, axis=0, keepdims=True)
        dva = _ln_bwd(dv * lng_v, f["vh"], f["v_rs"]) * f["dvg_fac"]

        cn, sc = f["cn"], f["sc"]
        dcn = dc_ * (sc * (1.0 + cn * (1.0 - sc)))
        dcg_ref[...] += jnp.sum(dcn * f["chh"], axis=0, keepdims=True)
        dcbeta_ref[...] += jnp.sum(dcn, axis=0, keepdims=True)
        dcpre = _group_ln_bwd(dcn * cg_v, f["chh"], f["c_rss"])
        dcb_ref[...] += jnp.sum(dcpre, axis=0, keepdims=True)
        d_shifts = _sublane_shifts(jnp.concatenate([dcpre, carry_scr[...]], axis=0))
        dg = jnp.zeros_like(dcpre)
        for k in range(CONV_TAPS):
            g_k = _rows_from(f["g_shifts"], HALO - (CONV_TAPS - 1) + k, tm)
            dcw_ref[k:k + 1, :] += jnp.sum(dcpre * g_k, axis=0, keepdims=True)
            dg = dg + _rows_from(d_shifts, CONV_TAPS - 1 - k, tm) * cw_ref[k:k + 1, :]
        carry_scr[...] = dcpre[:HALO, :]
        sg = f["sg"]
        dvb = dg * sg
        dgb = dg * vb * sg * (1.0 - sg)

        dps = [dua.astype(BF16), dva.astype(BF16), dvb.astype(BF16), dgb.astype(BF16)]
        dh1 = jnp.zeros((tm, D), F32)
        for p in range(N_CHIPS):
            dproj_ref[:, p * da:(p + 1) * da] = dps[p]
            dh1 = dh1 + _dot_nt(dps[p], win_ref[p])
        dg1_ref[...] += jnp.sum(dh1 * xh, axis=0, keepdims=True)
        dx = dx1_ref[...] + _rms_bwd(dh1, xh, r, g1v)
        dx_ref[...] = dx
        dxb_ref[...] = dx.astype(BF16)

    rev = lambda i: (n_t - 1 - i, 0)
    full = lambda a: pl.BlockSpec(a.shape, lambda i: (0,) * a.ndim)
    acc = lambda shape: pl.BlockSpec(shape, lambda i: (0,) * len(shape))
    halo_idx = lambda col: (lambda i: (jnp.maximum((n_t - 1 - i) * halo_blocks - 1, 0), col))
    small_shapes = [(1, D), (1, da), (1, da), (heads, CHUNK, CHUNK), (heads, CHUNK, CHUNK), (HALO, da),
                    (1, da), (1, da), (1, da)]
    return _call(
        body, f"mix_bwd_l{l}", (n_t,),
        in_specs=[
            pl.BlockSpec((tm, D), rev),
            pl.BlockSpec((tm, D), rev),
            pl.BlockSpec((tm, D), rev),
            pl.BlockSpec((tm, 2 * da), lambda i: (n_t - 1 - i, 1)),
            pl.BlockSpec((HALO, da), halo_idx(2)),
            pl.BlockSpec((HALO, da), halo_idx(3)),
            pl.BlockSpec((tm, da), rev),
            pl.BlockSpec((tm, 4 * da), rev),
            pl.BlockSpec((tm, LANES), rev),
            full(g1),
            pl.BlockSpec((N_CHIPS, D, da), lambda i: (0, 0, 0)),
            full(lng), full(lnb), full(ws), full(wst), full(bfull), full(cw), full(cb), full(cg), full(cbeta),
            pl.BlockSpec((N_CHIPS, wo_rows, D), lambda i: (0, 0, 0)),
        ],
        out_specs=[
            pl.BlockSpec((tm, D), rev),
            pl.BlockSpec((tm, D), rev),
            pl.BlockSpec((tm, N_CHIPS * da), rev),
            pl.BlockSpec((D, tm), lambda i: (0, n_t - 1 - i)),
            pl.BlockSpec((2 * da, tm), lambda i: (0, n_t - 1 - i)),
        ] + [acc(s) for s in small_shapes],
        out_shape=[jax.ShapeDtypeStruct((T, D), F32), jax.ShapeDtypeStruct((T, D), BF16),
                   jax.ShapeDtypeStruct((T, N_CHIPS * da), BF16), jax.ShapeDtypeStruct((D, T), BF16),
                   jax.ShapeDtypeStruct((2 * da, T), BF16)] + [jax.ShapeDtypeStruct(s, F32) for s in small_shapes],
        scratch_shapes=[pltpu.VMEM((HALO, da), F32), pltpu.VMEM((tm, da), F32), pltpu.VMEM((tm, da), F32)],
        sem=("arbitrary",),
        operands=(dx1, dx1b, x, proj, proj, proj, cpre, kept, v_rs, g1, win_g, lng, lnb, ws, wst, bfull, cw, cb, cg, cbeta, wout_g),
        comm=comm)


def _wgrad_call(name, at, b, split_cols, steps):
    M, T = at.shape
    N = b.shape[1]
    if split_cols:
        bn = N // (N_CHIPS * steps)
        in_specs = [pl.BlockSpec((M, T), lambda j: (0, 0)), pl.BlockSpec((T, bn), lambda j: (0, j))]
        out_spec = pl.BlockSpec((None, M, bn), lambda j: (j // steps, 0, j % steps))
        out_shape = jax.ShapeDtypeStruct((N_CHIPS, M, N // N_CHIPS), BF16)
    else:
        bm = M // (N_CHIPS * steps)
        in_specs = [pl.BlockSpec((bm, T), lambda j: (j, 0)), pl.BlockSpec((T, N), lambda j: (0, 0))]
        out_spec = pl.BlockSpec((None, bm, N), lambda j: (j // steps, j % steps, 0))
        out_shape = jax.ShapeDtypeStruct((N_CHIPS, M // N_CHIPS, N), BF16)

    def body(at_ref, b_ref, o_ref):
        o_ref[...] = jnp.dot(at_ref[...], b_ref[...], preferred_element_type=F32).astype(BF16)

    return pl.pallas_call(
        body, name=name, grid=(N_CHIPS * steps,), in_specs=in_specs, out_specs=out_spec, out_shape=out_shape,
        compiler_params=_params(("arbitrary",)),
    )(at, b)


def _rows_block(rows, cols):
    br = rows
    while br * cols * 4 > 2 ** 20 and br % 16 == 0:
        br //= 2
    return br


def _cast_own_call(name, chip, w, l):
    _, rows, cols = w.shape
    br = _rows_block(rows, cols)
    n_b = rows // br

    def body(chip_ref, w_ref, o_ref):
        o_ref[...] = w_ref[...].astype(BF16)

    return pl.pallas_call(
        body, name=name,
        grid_spec=pltpu.PrefetchScalarGridSpec(
            num_scalar_prefetch=1, grid=(n_b,),
            in_specs=[pl.BlockSpec((None, br, cols), lambda i, chip_ref: (l, i, 0))],
            out_specs=pl.BlockSpec((None, br, cols), lambda i, chip_ref: (chip_ref[0], i, 0))),
        out_shape=jax.ShapeDtypeStruct((N_CHIPS, rows, cols), BF16),
        compiler_params=_params(("arbitrary",)),
    )(chip, w)


def _pair_sum_call(name, core, g, got):
    _, rh, cols = got.shape
    br = _rows_block(rh, cols)
    n_b = rh // br

    def body(core_ref, g_ref, got_ref, o_ref):
        o_ref[...] = (g_ref[...].astype(F32) + got_ref[...].astype(F32)).astype(BF16)

    half = pl.BlockSpec((None, br, cols), lambda q, i, core_ref: (q, i, 0))
    return pl.pallas_call(
        body, name=name,
        grid_spec=pltpu.PrefetchScalarGridSpec(
            num_scalar_prefetch=1, grid=(N_CHIPS, n_b),
            in_specs=[pl.BlockSpec((None, br, cols), lambda q, i, core_ref: (q, core_ref[0] * n_b + i, 0)), half],
            out_specs=half),
        out_shape=jax.ShapeDtypeStruct(got.shape, BF16),
        compiler_params=_params(("arbitrary", "arbitrary")),
    )(core, g, got)


def _chip_sum_call(name, core, parts):
    _, rh, cols = parts.shape
    br = _rows_block(rh, cols)
    n_b = rh // br

    def body(core_ref, *refs):
        o_ref = refs[N_CHIPS]
        total = refs[0][...].astype(F32)
        for q in range(1, N_CHIPS):
            total = total + refs[q][...].astype(F32)
        o_ref[...] = total

    return pl.pallas_call(
        body, name=name,
        grid_spec=pltpu.PrefetchScalarGridSpec(
            num_scalar_prefetch=1, grid=(n_b,),
            in_specs=[pl.BlockSpec((None, br, cols), functools.partial(lambda i, core_ref, q: (q, i, 0), q=q))
                      for q in range(N_CHIPS)],
            out_specs=pl.BlockSpec((br, cols), lambda i, core_ref: (core_ref[0] * n_b + i, 0))),
        out_shape=jax.ShapeDtypeStruct((2 * rh, cols), F32),
        compiler_params=_params(("arbitrary",)),
    )(core, *([parts] * N_CHIPS))


def _adamw_layers_call(name, w, g0, g1, m, v, comm=None):
    _, rows, cols = w.shape
    br = _rows_block(rows, cols)

    def body(w_ref, g0_ref, g1_ref, m_ref, v_ref, g_ref, d_ref, nm_ref, nv_ref):
        gv = jnp.where(pl.program_id(0) == 0, g0_ref[...], g1_ref[...])
        g_ref[...] = gv
        m_new = ADAM_B1 * m_ref[...] + (1.0 - ADAM_B1) * gv
        v_new = ADAM_B2 * v_ref[...] + (1.0 - ADAM_B2) * (gv * gv)
        m_hat = m_new / (1.0 - ADAM_B1 ** ADAM_STEP)
        v_hat = v_new / (1.0 - ADAM_B2 ** ADAM_STEP)
        d_ref[...] = -ADAM_LR * (m_hat / (jnp.sqrt(v_hat) + ADAM_EPS) + ADAM_WD * w_ref[...])
        nm_ref[...] = m_new
        nv_ref[...] = v_new

    both = pl.BlockSpec((None, br, cols), lambda l, i: (l, i, 0))
    n_b = rows // br
    layer0 = pl.BlockSpec((br, cols), lambda l, i: (i * (1 - l) + (n_b - 1) * l, 0))
    layer1 = pl.BlockSpec((br, cols), lambda l, i: (i * l, 0))
    return _call(body, name, (2, n_b), in_specs=[both, layer0, layer1, both, both], out_specs=[both] * 4,
                 out_shape=[jax.ShapeDtypeStruct(w.shape, F32)] * 4, scratch_shapes=[],
                 sem=("arbitrary", "arbitrary"), operands=(w, g0, g1, m, v), comm=comm)


def _adamw_call(name, w, g, m, v):
    rows, cols = w.shape
    br = _rows_block(rows, cols)

    def body(w_ref, g_ref, m_ref, v_ref, d_ref, nm_ref, nv_ref):
        gv = g_ref[...]
        m_new = ADAM_B1 * m_ref[...] + (1.0 - ADAM_B1) * gv
        v_new = ADAM_B2 * v_ref[...] + (1.0 - ADAM_B2) * (gv * gv)
        m_hat = m_new / (1.0 - ADAM_B1 ** ADAM_STEP)
        v_hat = v_new / (1.0 - ADAM_B2 ** ADAM_STEP)
        d_ref[...] = -ADAM_LR * (m_hat / (jnp.sqrt(v_hat) + ADAM_EPS) + ADAM_WD * w_ref[...])
        nm_ref[...] = m_new
        nv_ref[...] = v_new

    spec = pl.BlockSpec((br, cols), lambda i: (i, 0))
    return pl.pallas_call(
        body, name=name, grid=(rows // br,), in_specs=[spec] * 4, out_specs=[spec] * 3,
        out_shape=[jax.ShapeDtypeStruct((rows, cols), F32)] * 3,
        compiler_params=_params(("arbitrary",)),
    )(w, g, m, v)


def _place():
    x, y, c = lax.axis_index("x"), lax.axis_index("y"), lax.axis_index("c")
    chips = [(1 - x, y), (x, 1 - y), (1 - x, 1 - y)]
    return x, y, c, 2 * x + y, chips


def _half_rows(ref, core):
    rh = ref.shape[-2] // 2
    rows = pl.ds(pl.multiple_of(core * rh, rh), rh)
    return ref.at[rows] if len(ref.shape) == 2 else ref.at[:, rows]


def _gather_comm(bufs, cw_shard=None):
    n = len(bufs)
    with_cw = cw_shard is not None

    def make(c_ins, c_outs, sems):
        ins, outs = c_ins[:n], c_outs[:n]
        send_sems, recv_sems, cw_send, cw_recv, cw_local = sems
        x, y, c, p, chips = _place()
        sibling = (x, y, 1 - c)
        qs = [2 * cx + cy for cx, cy in chips]
        hops = len(chips)

        def remote(k, j, src, dst, to):
            return pltpu.make_async_remote_copy(src_ref=src, dst_ref=dst, send_sem=send_sems.at[k, j],
                                                recv_sem=recv_sems.at[k, j], device_id=to, device_id_type=MESH)

        def cw_copy(j, dst, to):
            return pltpu.make_async_remote_copy(src_ref=c_ins[n], dst_ref=dst, send_sem=cw_send.at[j],
                                                recv_sem=cw_recv.at[j], device_id=to, device_id_type=MESH)

        def over_ici():
            return [remote(k, j, _half_rows(ins[k].at[p], c), _half_rows(outs[k].at[p], c), (*chip, c))
                    for j, chip in enumerate(chips) for k in range(n)]

        def passed_on():
            return [remote(k, hops + j, _half_rows(outs[k].at[qs[j]], c), _half_rows(outs[k].at[qs[j]], c), sibling)
                    for j in range(hops) for k in range(n)]

        def cw_copies():
            return [cw_copy(j, c_outs[n].at[p], (*chip, c)) for j, chip in enumerate(chips)]

        def cw_own():
            return pltpu.make_async_copy(c_ins[n], c_outs[n].at[p], cw_local)

        def start():
            for cp in over_ici():
                cp.start()
            if with_cw:
                cw_own().start()
                for cp in cw_copies():
                    cp.start()

        def finish():
            for j in range(hops):
                for k in range(n):
                    landed = _half_rows(outs[k].at[qs[j]], c)
                    remote(k, j, landed, landed, sibling).wait_recv()
                    remote(k, hops + j, landed, landed, sibling).start()
            for j in range(hops):
                for k in range(n):
                    other = _half_rows(outs[k].at[qs[j]], 1 - c)
                    remote(k, hops + j, other, other, sibling).wait_recv()
            if with_cw:
                for j in range(hops):
                    cw_copy(j, c_outs[n].at[qs[j]], sibling).wait_recv()
                cw_own().wait()
                for cp in cw_copies():
                    cp.wait_send()
            for cp in over_ici() + passed_on():
                cp.wait_send()

        return start, finish

    out_shape = [jax.ShapeDtypeStruct(b.shape, b.dtype) for b in bufs]
    if with_cw:
        out_shape.append(jax.ShapeDtypeStruct((N_CHIPS,) + cw_shard.shape, cw_shard.dtype))
    return dict(operands=list(bufs) + ([cw_shard] if with_cw else []), out_shape=out_shape,
                aliases={k: k for k in range(n)}, make=make,
                scratch=[pltpu.SemaphoreType.DMA((n, 6)), pltpu.SemaphoreType.DMA((n, 6)),
                         pltpu.SemaphoreType.DMA((3,)), pltpu.SemaphoreType.DMA((3,)), pltpu.SemaphoreType.DMA])


def _pair_exchange_call(l, gs):
    n = len(gs)

    def body(*refs):
        g, got = refs[:n], refs[n:2 * n]
        send_sems, recv_sems = refs[2 * n:]
        x, y, c, _, _ = _place()
        sibling = (x, y, 1 - c)
        copies = [pltpu.make_async_remote_copy(src_ref=_half_rows(g[k], 1 - c), dst_ref=got[k],
                                               send_sem=send_sems.at[k], recv_sem=recv_sems.at[k],
                                               device_id=sibling, device_id_type=MESH) for k in range(n)]
        for cp in copies:
            cp.start()
        for cp in copies:
            cp.wait_send()
            cp.wait_recv()

    return pl.pallas_call(
        body, name=f"grad_pair_exchange_l{l}",
        in_specs=[ANY] * n, out_specs=[ANY] * n,
        out_shape=[jax.ShapeDtypeStruct((g.shape[0], g.shape[1] // 2, g.shape[2]), g.dtype) for g in gs],
        scratch_shapes=[pltpu.SemaphoreType.DMA((n,)), pltpu.SemaphoreType.DMA((n,))],
    )(*gs)


def _exchange_comm(sums):
    n = len(sums)

    def make(ins, outs, sems):
        send_sems, recv_sems, local_sems = sems
        x, y, c, p, chips = _place()
        qs = [2 * cx + cy for cx, cy in chips]

        def remote(k, j, src, dst, to):
            return pltpu.make_async_remote_copy(src_ref=src, dst_ref=dst, send_sem=send_sems.at[k, j],
                                                recv_sem=recv_sems.at[k, j], device_id=to, device_id_type=MESH)

        def own():
            return [pltpu.make_async_copy(ins[k].at[p], outs[k].at[p], local_sems.at[k]) for k in range(n)]

        def sent():
            return [remote(k, j, ins[k].at[qs[j]], outs[k].at[p], (*chip, c))
                    for j, chip in enumerate(chips) for k in range(n)]

        def start():
            for cp in own() + sent():
                cp.start()

        def finish():
            for j in range(len(chips)):
                for k in range(n):
                    remote(k, j, ins[k].at[qs[j]], outs[k].at[qs[j]], (x, y, c)).wait_recv()
            for cp in sent():
                cp.wait_send()
            for cp in own():
                cp.wait()

        return start, finish

    return dict(operands=list(sums), out_shape=[jax.ShapeDtypeStruct(s.shape, s.dtype) for s in sums], aliases={},
                make=make, scratch=[pltpu.SemaphoreType.DMA((n, 3)), pltpu.SemaphoreType.DMA((n, 3)),
                                    pltpu.SemaphoreType.DMA((n,))])


def _pair_gather_call(l, halves):
    n = len(halves)

    def body(*refs):
        ins, outs = refs[:n], refs[n:2 * n]
        send_sems, recv_sems = refs[2 * n:]
        x, y, c, _, _ = _place()
        sibling = (x, y, 1 - c)

        def remote(k, src, dst):
            return pltpu.make_async_remote_copy(src_ref=src, dst_ref=dst, send_sem=send_sems.at[k],
                                                recv_sem=recv_sems.at[k], device_id=sibling, device_id_type=MESH)

        sent = [remote(k, _half_rows(ins[k], c), _half_rows(outs[k], c)) for k in range(n)]
        for cp in sent:
            cp.start()
        for k in range(n):
            other = _half_rows(outs[k], 1 - c)
            remote(k, other, other).wait_recv()
        for cp in sent:
            cp.wait_send()

    return pl.pallas_call(
        body, name=f"grad_pair_gather_l{l}",
        in_specs=[ANY] * n, out_specs=[ANY] * n,
        out_shape=[jax.ShapeDtypeStruct(h.shape, h.dtype) for h in halves],
        input_output_aliases={k: k for k in range(n)},
        scratch_shapes=[pltpu.SemaphoreType.DMA((n,)), pltpu.SemaphoreType.DMA((n,))],
    )(*halves)


def _all_sum_small_call(block, comm=None):
    m_per, n = block.shape

    def body(x_ref, sum_ref, all_ref, send_sems, recv_sems, local_sem):
        x, y, c, _, chip_list = _place()
        me, sibling = (x, y, c), (x, y, 1 - c)

        def rows(px, py, pc):
            return all_ref.at[pl.ds((4 * px + 2 * py + pc) * m_per, m_per), :]

        def copy(k, blk, to, src=None):
            return pltpu.make_async_remote_copy(src_ref=rows(*blk) if src is None else src, dst_ref=rows(*blk),
                                                send_sem=send_sems.at[k], recv_sem=recv_sems.at[k],
                                                device_id=to, device_id_type=MESH)

        mine = pltpu.make_async_copy(x_ref, rows(*me), local_sem)
        mine.start()
        first = [copy(0, me, sibling, src=x_ref)]
        first += [copy(1 + j, me, (*chip, c), src=x_ref) for j, chip in enumerate(chip_list)]
        for cp in first:
            cp.start()
        passed = [copy(4 + j, (*chip, c), sibling) for j, chip in enumerate(chip_list)]
        for j, chip in enumerate(chip_list):
            copy(1 + j, (*chip, c), me).wait_recv()
            passed[j].start()
        copy(0, sibling, me).wait_recv()
        for j, chip in enumerate(chip_list):
            copy(4 + j, (*chip, 1 - c), me).wait_recv()
        for cp in first + passed:
            cp.wait_send()
        mine.wait()
        total = all_ref[0:m_per, :]
        for d in range(1, N_DEV):
            total = total + all_ref[d * m_per:(d + 1) * m_per, :]
        sum_ref[...] = total

    vmem = pl.BlockSpec(memory_space=pltpu.VMEM)
    res, c_res = _call(
        body, "small_all_sum", (), in_specs=[vmem], out_specs=[vmem, vmem],
        out_shape=[jax.ShapeDtypeStruct((m_per, n), F32), jax.ShapeDtypeStruct((N_DEV * m_per, n), F32)],
        scratch_shapes=[pltpu.SemaphoreType.DMA((7,)), pltpu.SemaphoreType.DMA((7,)), pltpu.SemaphoreType.DMA],
        sem=None, operands=(block,), comm=comm)
    return res[0], c_res


SMALL_NAMES = ["norm1_g", "sgu_ln_g", "sgu_ln_b", "sgu_w", "sgu_b", "conv_b", "conv_ln_g", "conv_ln_b", "norm2_g",
               "final_g"]
MIX_NAMES = ["w_in", "w_out"]
FFN_NAMES = ["w_ff1", "w_ff2"]
BIG_NAMES = MIX_NAMES + FFN_NAMES


def _rows128(a):
    return a.reshape(-1, LANES)


def kernel(x, norm1_g, w_in, sgu_ln_g, sgu_ln_b, sgu_w, sgu_b, conv_w, conv_b, conv_ln_g, conv_ln_b, w_out, norm2_g, w_ff1, w_ff2, final_g, loss_target, m_norm1_g, m_w_in, m_sgu_ln_g, m_sgu_ln_b, m_sgu_w, m_sgu_b, m_conv_w, m_conv_b, m_conv_ln_g, m_conv_ln_b, m_w_out, m_norm2_g, m_w_ff1, m_w_ff2, m_final_g, v_norm1_g, v_w_in, v_sgu_ln_g, v_sgu_ln_b, v_sgu_w, v_sgu_b, v_conv_w, v_conv_b, v_conv_ln_g, v_conv_ln_b, v_w_out, v_norm2_g, v_w_ff1, v_w_ff2, v_final_g):
    w = dict(norm1_g=norm1_g, w_in=w_in, sgu_ln_g=sgu_ln_g, sgu_ln_b=sgu_ln_b, sgu_w=sgu_w, sgu_b=sgu_b,
             conv_w=conv_w, conv_b=conv_b, conv_ln_g=conv_ln_g, conv_ln_b=conv_ln_b, w_out=w_out, norm2_g=norm2_g,
             w_ff1=w_ff1, w_ff2=w_ff2, final_g=final_g)
    m = dict(norm1_g=m_norm1_g, w_in=m_w_in, sgu_ln_g=m_sgu_ln_g, sgu_ln_b=m_sgu_ln_b, sgu_w=m_sgu_w, sgu_b=m_sgu_b,
             conv_w=m_conv_w, conv_b=m_conv_b, conv_ln_g=m_conv_ln_g, conv_ln_b=m_conv_ln_b, w_out=m_w_out,
             norm2_g=m_norm2_g, w_ff1=m_w_ff1, w_ff2=m_w_ff2, final_g=m_final_g)
    v = dict(norm1_g=v_norm1_g, w_in=v_w_in, sgu_ln_g=v_sgu_ln_g, sgu_ln_b=v_sgu_ln_b, sgu_w=v_sgu_w, sgu_b=v_sgu_b,
             conv_w=v_conv_w, conv_b=v_conv_b, conv_ln_g=v_conv_ln_g, conv_ln_b=v_conv_ln_b, w_out=v_w_out,
             norm2_g=v_norm2_g, w_ff1=v_w_ff1, w_ff2=v_w_ff2, final_g=v_final_g)
    depth = w_in.shape[0]
    assert depth == 2, "core c owns layer c of every gradient"
    T, D = x.shape[1], x.shape[2]
    heads = sgu_w.shape[1]
    da = heads * CHUNK
    core = lax.axis_index("c")
    chip = 2 * lax.axis_index("x") + lax.axis_index("y")

    core_arr = core.reshape(1).astype(jnp.int32)
    chip_arr = chip.reshape(1).astype(jnp.int32)
    cw_pad = jnp.pad(conv_w, ((0, 0), (0, HALO - CONV_TAPS), (0, 0)))
    own = [{name: _cast_own_call(f"cast_{name}_l{l}", chip_arr, w[name], l) for name in BIG_NAMES}
           for l in range(depth)]
    gather_of = lambda l, group: _gather_comm([own[l][name] for name in group])
    win_g, wout_g, cw_g = _alone("gather_mix_l0", _gather_comm([own[0][name] for name in MIX_NAMES], cw_pad))
    cw_full = jnp.transpose(cw_g, (1, 2, 0, 3)).reshape(depth, HALO, da)

    ws_t = jnp.swapaxes(sgu_w, -1, -2)
    b_full = jnp.broadcast_to(sgu_b[..., None], sgu_w.shape)
    row = lambda a, l: a[l:l + 1]

    xs, projs, kept_acts, x1s, zs = [], [], [], [], []
    h = x.reshape(T, D)
    gathered = []
    w1_g = None
    for l in range(depth):
        xs.append(h)
        (x1, proj, *kept), late = _mix_fwd_call(
            l, h, row(norm1_g, l), win_g, row(sgu_ln_g, l), row(sgu_ln_b, l), sgu_w[l], b_full[l], cw_full[l],
            row(conv_b, l), row(conv_ln_g, l), row(conv_ln_b, l), wout_g,
            comm=gather_of(l, FFN_NAMES if w1_g is None else FFN_NAMES[1:]))
        w1_g, w2_g = late if len(late) == 2 else (w1_g, late[0])
        gathered.append((win_g, wout_g, w1_g, w2_g))
        (h, z), nxt = _ffn_fwd_call(l, x1, row(norm2_g, l), w1_g, w2_g,
                                    comm=gather_of(l + 1, MIX_NAMES + FFN_NAMES[:1]) if l + 1 < depth else None)
        if nxt:
            win_g, wout_g, w1_g = nxt
        projs.append(proj)
        kept_acts.append(kept)
        x1s.append(x1)
        zs.append(z)
    dx, dxb, loss, d_final_g = _loss_call(h, final_g.reshape(1, D), loss_target.reshape(T, D))

    big = {name: [None] * depth for name in BIG_NAMES}
    reduced = {name: [None] * depth for name in BIG_NAMES}
    small = {name: [None] * depth for name in SMALL_NAMES[:-1] + ["conv_w"]}

    def pair_sums(l, group):
        got = _pair_exchange_call(f"{group[0]}_l{l}", [big[name][l] for name in group])
        return [_pair_sum_call(f"pair_sum_{name}_l{l}", core_arr, big[name][l], recv) for name, recv in zip(group, got)]

    def finish_reduce(l, group, parts):
        halves = [_chip_sum_call(f"chip_sum_{name}_l{l}", core_arr, part) for name, part in zip(group, parts)]
        for name, full in zip(group, _pair_gather_call(f"{group[0]}_l{l}", halves)):
            reduced[name][l] = full

    pending = None
    for l in reversed(range(depth)):
        win_g, wout_g, w1_g, w2_g = gathered[l]
        (dx1, dx1b, dz, f_t, h2_t, dg2), parts = _ffn_bwd_call(
            l, dx, dxb, x1s[l], zs[l], row(norm2_g, l), w1_g, w2_g,
            comm=_exchange_comm(pending[2]) if pending else None)
        if pending:
            finish_reduce(pending[0], pending[1], parts)
        big["w_ff2"][l] = _wgrad_call(f"wgrad_ff2_l{l}", f_t, dxb, False, 2)
        big["w_ff1"][l] = _wgrad_call(f"wgrad_ff1_l{l}", h2_t, dz, True, 2)
        ffn_sums = pair_sums(l, FFN_NAMES)
        ((dx, dxb, dproj, h1_t, mix_t, dg1, dlng, dlnb, dws, dbs, dcw, dcb, dcg, dcbeta), parts) = _mix_bwd_call(
            l, dx1, dx1b, xs[l], projs[l], *kept_acts[l], row(norm1_g, l), win_g, row(sgu_ln_g, l), row(sgu_ln_b, l), sgu_w[l],
            ws_t[l], b_full[l], cw_full[l], row(conv_b, l), row(conv_ln_g, l), row(conv_ln_b, l), wout_g,
            comm=_exchange_comm(ffn_sums))
        finish_reduce(l, FFN_NAMES, parts)
        big["w_out"][l] = _wgrad_call(f"wgrad_out_l{l}", mix_t, dx1b, False, 1)
        big["w_in"][l] = _wgrad_call(f"wgrad_in_l{l}", h1_t, dproj, True, 1)
        pending = (l, MIX_NAMES, pair_sums(l, MIX_NAMES))
        small["norm1_g"][l] = dg1[0]
        small["sgu_ln_g"][l] = dlng[0]
        small["sgu_ln_b"][l] = dlnb[0]
        small["sgu_w"][l] = dws
        small["sgu_b"][l] = dbs[:, :, 0]
        small["conv_w"][l] = dcw[:CONV_TAPS]
        small["conv_b"][l] = dcb[0]
        small["conv_ln_g"][l] = dcg[0]
        small["conv_ln_b"][l] = dcbeta[0]
        small["norm2_g"][l] = dg2[0]
    grad_x = dx.reshape(x.shape)

    small_local = {name: jnp.stack(small[name]) for name in small}
    small_local["final_g"] = d_final_g[0]
    pieces = [_rows128(small_local[name]) for name in SMALL_NAMES]
    pieces.append(_rows128(small_local["conv_w"]))
    pieces.append(jnp.broadcast_to(loss, (8, LANES)))
    offsets = [0]
    for piece in pieces:
        offsets.append(offsets[-1] + piece.shape[0])
    summed, parts = _all_sum_small_call(jnp.concatenate(pieces, axis=0), comm=_exchange_comm(pending[2]))
    finish_reduce(pending[0], pending[1], parts)
    n_small = offsets[len(SMALL_NAMES)]
    loss_out = summed[offsets[-2], 0]
    small_grads = {name: summed[offsets[k]:offsets[k + 1]].reshape(w[name].shape)
                   for k, name in enumerate(SMALL_NAMES)}
    conv_w_full = summed[offsets[-3]:offsets[-2]].reshape(depth, CONV_TAPS, da)
    conv_w_grad = lax.dynamic_slice_in_dim(conv_w_full, chip * conv_w.shape[-1], conv_w.shape[-1], axis=2)

    grads, delta, new_m, new_v = {}, {}, {}, {}
    for name in BIG_NAMES:
        (grads[name], delta[name], new_m[name], new_v[name]), _ = _adamw_layers_call(
            f"adamw_{name}", w[name], reduced[name][0], reduced[name][1], m[name], v[name])
    pack = lambda src: jnp.concatenate([_rows128(src[name]) for name in SMALL_NAMES], axis=0)
    d_, m_, v_ = _adamw_call("adamw_small", pack(w), summed[:n_small], pack(m), pack(v))
    for k, name in enumerate(SMALL_NAMES):
        sl = slice(offsets[k], offsets[k + 1])
        grads[name] = small_grads[name]
        delta[name] = d_[sl].reshape(w[name].shape)
        new_m[name] = m_[sl].reshape(w[name].shape)
        new_v[name] = v_[sl].reshape(w[name].shape)
    cshape = conv_w.shape
    flat = lambda a: a.reshape(-1, cshape[-1])
    d_, m_, v_ = _adamw_call("adamw_conv_w", flat(conv_w), flat(conv_w_grad), flat(m["conv_w"]), flat(v["conv_w"]))
    grads["conv_w"] = conv_w_grad
    delta["conv_w"], new_m["conv_w"], new_v["conv_w"] = d_.reshape(cshape), m_.reshape(cshape), v_.reshape(cshape)

    order = ["norm1_g", "w_in", "sgu_ln_g", "sgu_ln_b", "sgu_w", "sgu_b", "conv_w", "conv_b", "conv_ln_g",
             "conv_ln_b", "w_out", "norm2_g", "w_ff1", "w_ff2", "final_g"]
    return (loss_out, grad_x, *[grads[n] for n in order], *[delta[n] for n in order],
            *[new_m[n] for n in order], *[new_v[n] for n in order])
```
